```python
import jax, jax.numpy as jnp
from jax import lax
import numpy as np

D_MODEL = 1024
BATCH = 2
SEQ = 8192
DEPTH = 2

HEAD_DIM = 64
ROPE_THETA = 10000.0

RET_HEADS = 4
RET_DK = 64
RET_DV = 64
RET_CHUNK = 128

SWA_Q_HEADS = 6
SWA_KV_HEADS = 2
SWA_WINDOW = 128
SWA_BLOCK = 128

MLA_HEADS = 6
MLA_Q_RANK = 384
MLA_KV_RANK = 256
MLA_NOPE = 64
MLA_ROPE = 32
MLA_V = 64
MLA_BLOCK = 128

RET_W = RET_HEADS * RET_DV
SWA_W = SWA_Q_HEADS * HEAD_DIM
MLA_W = MLA_HEADS * MLA_V
D_MIX = RET_W + SWA_W + MLA_W

IN_SIZES = (
    RET_HEADS * RET_DK,
    RET_HEADS * RET_DK,
    RET_W,
    RET_W,
    SWA_Q_HEADS * HEAD_DIM,
    SWA_KV_HEADS * HEAD_DIM,
    SWA_KV_HEADS * HEAD_DIM,
    MLA_Q_RANK,
    MLA_KV_RANK,
    MLA_ROPE,
)
IN_COLS = sum(IN_SIZES)

N_EXPERTS = 16
N_GROUPS = 4
EXPERTS_PER_GROUP = N_EXPERTS // N_GROUPS
TOP_K = 2
D_EXPERT = 256

DEEPNORM_ALPHA = (2 * DEPTH) ** 0.25
DEEPNORM_BETA = (8 * DEPTH) ** -0.25
LN_EPS = 1e-5
RMS_EPS = 1e-6
NEG_INF = -1e30

kernel_name = "hymba_style_ret_swa_mla_grouped_moe_deepnorm"


def _split_points():
    return [int(s) for s in np.cumsum(np.array(IN_SIZES))[:-1]]


def layer_norm(x, g, b):
    xf = x.astype(jnp.float32)
    mu = jnp.mean(xf, -1, keepdims=True)
    var = jnp.mean(jnp.square(xf - mu), -1, keepdims=True)
    return ((xf - mu) * lax.rsqrt(var + LN_EPS) * g + b).astype(x.dtype)


def rms_norm(x, g):
    xf = x.astype(jnp.float32)
    return (xf * lax.rsqrt(jnp.mean(xf * xf, -1, keepdims=True) + RMS_EPS) * g).astype(x.dtype)


def rope(x, pos):
    d = x.shape[-1]
    inv = ROPE_THETA ** (-jnp.arange(0, d, 2, dtype=jnp.float32) / d)
    ang = pos.astype(jnp.float32)[:, None] * inv[None, :]
    cos = jnp.cos(ang)[:, None, :]
    sin = jnp.sin(ang)[:, None, :]
    x1, x2 = x[..., : d // 2], x[..., d // 2:]
    return jnp.concatenate([x1 * cos - x2 * sin, x1 * sin + x2 * cos], -1).astype(x.dtype)


def retention(q, k, v, g, gn_w):
    B, S, H, dk = q.shape
    dv = RET_DV
    C = RET_CHUNK
    NC = S // C
    gamma = 1.0 - 2.0 ** (-5.0 - jnp.arange(H, dtype=jnp.float32))
    log_g = jnp.log(gamma)
    idx = jnp.arange(C, dtype=jnp.float32)
    rel = idx[:, None] - idx[None, :]
    decay_in = jnp.where(rel[None] >= 0,
                         jnp.exp(jnp.maximum(rel, 0.0)[None] * log_g[:, None, None]), 0.0)
    xi = jnp.exp((idx[None, :] + 1.0) * log_g[:, None]).T
    zeta = jnp.exp((C - 1.0 - idx[None, :]) * log_g[:, None]).T
    chunk_decay = jnp.exp(C * log_g)

    qc = q.reshape(B, NC, C, H, dk).astype(jnp.float32)
    kc = (k * (dk ** -0.5)).reshape(B, NC, C, H, dk).astype(jnp.float32)
    vc = v.reshape(B, NC, C, H, dv).astype(jnp.float32)

    scores = jnp.einsum('bnihd,bnjhd->bnhij', qc, kc) * decay_in[None, None]
    inner = jnp.einsum('bnhij,bnjhe->bnihe', scores, vc)

    upd = jnp.einsum('bnjhd,bnjhe->bnhde', kc * zeta[None, None, :, :, None], vc)

    def step(state, u):
        return chunk_decay[None, :, None, None] * state + u, state

    init = jnp.zeros((B, H, dk, dv), jnp.float32)
    _, prev_states = lax.scan(step, init, jnp.moveaxis(upd, 1, 0))
    prev_states = jnp.moveaxis(prev_states, 0, 1)
    cross = jnp.einsum('bnihd,bnhde->bnihe', qc * xi[None, None, :, :, None], prev_states)

    o = (inner + cross).reshape(B, S, H, dv)
    mu = jnp.mean(o, -1, keepdims=True)
    var = jnp.mean(jnp.square(o - mu), -1, keepdims=True)
    o = (o - mu) * lax.rsqrt(var + LN_EPS) * gn_w.astype(jnp.float32).reshape(H, dv)
    o = o.reshape(B, S, H * dv) * jax.nn.silu(g.astype(jnp.float32))
    return o.astype(q.dtype)


def sliding_window_sink_attention(q, k, v, sinks):
    B, S, Hq, d = q.shape
    Hkv = k.shape[2]
    G = Hq // Hkv
    L = SWA_BLOCK
    NB = S // L
    qb = q.reshape(B, NB, L, Hkv, G, d)
    kb = k.reshape(B, NB, L, Hkv, d)
    vb = v.reshape(B, NB, L, Hkv, d)
    pad = ((0, 0), (1, 0), (0, 0), (0, 0), (0, 0))
    kwin = jnp.concatenate([jnp.pad(kb, pad)[:, :NB], kb], axis=2)
    vwin = jnp.concatenate([jnp.pad(vb, pad)[:, :NB], vb], axis=2)

    s = jnp.einsum('bnihgd,bnjhd->bnhgij', qb, kwin).astype(jnp.float32) * (d ** -0.5)
    i = jnp.arange(L)[:, None]
    j = jnp.arange(2 * L)[None, :]
    band = (j <= i + L) & (j > i + L - SWA_WINDOW)
    blk = jnp.arange(NB)[:, None]
    valid = (blk > 0) | (jnp.arange(2 * L)[None, :] >= L)
    mask = band[None] & valid[:, None, :]
    s = jnp.where(mask[None, :, None, None], s, NEG_INF)

    sink = sinks.astype(jnp.float32).reshape(1, 1, Hkv, G, 1, 1)
    m = jnp.maximum(jnp.max(s, -1, keepdims=True), sink)
    p = jnp.exp(s - m)
    denom = jnp.sum(p, -1, keepdims=True) + jnp.exp(sink - m)
    o = jnp.einsum('bnhgij,bnjhd->bnihgd', (p / denom).astype(v.dtype), vwin)
    return o.reshape(B, S, Hq * d)


def latent_attention(c_q, c_kv, k_rope, q_norm_w, kv_norm_w, w_uq, w_ukv, pos):
    B, S, _ = c_q.shape
    H = MLA_HEADS
    dq = MLA_NOPE + MLA_ROPE
    q = (rms_norm(c_q, q_norm_w) @ w_uq).reshape(B, S, H, dq)
    q = jnp.concatenate([q[..., :MLA_NOPE], rope(q[..., MLA_NOPE:], pos)], -1)
    kv = (rms_norm(c_kv, kv_norm_w) @ w_ukv).reshape(B, S, H, MLA_NOPE + MLA_V)
    k_nope, v = kv[..., :MLA_NOPE], kv[..., MLA_NOPE:]
    k_pe = rope(k_rope[:, :, None, :], pos)
    k = jnp.concatenate([k_nope, jnp.broadcast_to(k_pe, (B, S, H, MLA_ROPE))], -1)
    scale = dq ** -0.5
    L = MLA_BLOCK
    NB = S // L
    qb = jnp.moveaxis(q.reshape(B, NB, L, H, dq), 1, 0)
    kpos = jnp.arange(S)

    def block(args):
        q_blk, n = args
        s = jnp.einsum('bihd,bjhd->bhij', q_blk, k).astype(jnp.float32) * scale
        qpos = n * L + jnp.arange(L)
        s = jnp.where(kpos[None, :] <= qpos[:, None], s, NEG_INF)
        p = jax.nn.softmax(s, axis=-1)
        return jnp.einsum('bhij,bjhe->bihe', p.astype(v.dtype), v)

    o = lax.map(block, (qb, jnp.arange(NB)))
    return jnp.moveaxis(o, 0, 1).reshape(B, S, H * MLA_V)


def grouped_moe(x, router_w, router_bias, w_gate_up, w_down):
    B, S, D = x.shape
    t = x.reshape(-1, D)
    T = t.shape[0]
    scores = jax.nn.sigmoid((t @ router_w).astype(jnp.float32))
    biased = scores + router_bias.astype(jnp.float32)
    grp = biased.reshape(T, N_GROUPS, EXPERTS_PER_GROUP)
    grp_score = jnp.sum(lax.top_k(grp, TOP_K)[0], -1)
    sel_group = jnp.argmax(grp_score, -1)
    in_group = jnp.repeat(jax.nn.one_hot(sel_group, N_GROUPS) > 0, EXPERTS_PER_GROUP, axis=-1)
    _, top_idx = lax.top_k(jnp.where(in_group, biased, -jnp.inf), TOP_K)
    sel = jnp.take_along_axis(scores, top_idx, -1)
    weights = sel / jnp.sum(sel, -1, keepdims=True)
    gates = jnp.sum(jax.nn.one_hot(top_idx, N_EXPERTS, dtype=jnp.float32) * weights[..., None], 1)
    h = jnp.einsum('td,edf->tef', t, w_gate_up)
    a = jax.nn.silu(h[..., :D_EXPERT]) * h[..., D_EXPERT:] * gates[..., None].astype(h.dtype)
    y = jnp.einsum('tef,efd->td', a, w_down)
    return y.reshape(B, S, D)


def setup_inputs(seed: int = 0) -> dict:
    key = jax.random.key(seed)
    ks = jax.random.split(key, 20)
    f32 = jnp.float32
    nrm = lambda k, shape, s: jax.random.normal(k, shape, f32) * s
    return {
        "x": nrm(ks[0], (BATCH, SEQ, D_MODEL), 1.0),
        "w_in": nrm(ks[1], (DEPTH, D_MODEL, IN_COLS), D_MODEL ** -0.5),
        "ret_gn_w": 1.0 + nrm(ks[2], (DEPTH, RET_W), 0.02),
        "swa_sinks": nrm(ks[3], (DEPTH, SWA_Q_HEADS), 0.5),
        "mla_q_norm_w": 1.0 + nrm(ks[4], (DEPTH, MLA_Q_RANK), 0.02),
        "mla_kv_norm_w": 1.0 + nrm(ks[5], (DEPTH, MLA_KV_RANK), 0.02),
        "mla_w_uq": nrm(ks[6], (DEPTH, MLA_Q_RANK, MLA_HEADS * (MLA_NOPE + MLA_ROPE)), MLA_Q_RANK ** -0.5),
        "mla_w_ukv": nrm(ks[7], (DEPTH, MLA_KV_RANK, MLA_HEADS * (MLA_NOPE + MLA_V)), MLA_KV_RANK ** -0.5),
        "w_out": nrm(ks[8], (DEPTH, D_MIX, D_MODEL), DEEPNORM_BETA * D_MIX ** -0.5),
        "ln1_g": 1.0 + nrm(ks[9], (DEPTH, D_MODEL), 0.02),
        "ln1_b": nrm(ks[10], (DEPTH, D_MODEL), 0.02),
        "router_w": nrm(ks[11], (D_MODEL, N_EXPERTS), D_MODEL ** -0.5),
        "router_bias": nrm(ks[12], (N_EXPERTS,), 0.01),
        "exp_w_gate_up": nrm(ks[13], (DEPTH, N_EXPERTS, D_MODEL, 2 * D_EXPERT), D_MODEL ** -0.5),
        "exp_w_down": nrm(ks[14], (DEPTH, N_EXPERTS, D_EXPERT, D_MODEL), DEEPNORM_BETA * D_EXPERT ** -0.5),
        "ln2_g": 1.0 + nrm(ks[15], (DEPTH, D_MODEL), 0.02),
        "ln2_b": nrm(ks[16], (DEPTH, D_MODEL), 0.02),
    }


def reference(x, w_in, ret_gn_w, swa_sinks, mla_q_norm_w, mla_kv_norm_w, mla_w_uq, mla_w_ukv,
              w_out, ln1_g, ln1_b, router_w, router_bias, exp_w_gate_up, exp_w_down, ln2_g, ln2_b):
    B, S, _ = x.shape
    pos = jnp.arange(S)
    splits = _split_points()
    for l in range(DEPTH):
        (r_q, r_k, r_v, r_g, s_q, s_k, s_v, c_q, c_kv, k_rope) = jnp.split(x @ w_in[l], splits, axis=-1)
        rq = rope(r_q.reshape(B, S, RET_HEADS, RET_DK), pos)
        rk = rope(r_k.reshape(B, S, RET_HEADS, RET_DK), pos)
        ret_o = retention(rq, rk, r_v, r_g, ret_gn_w[l])
        swa_o = sliding_window_sink_attention(
            s_q.reshape(B, S, SWA_Q_HEADS, HEAD_DIM),
            s_k.reshape(B, S, SWA_KV_HEADS, HEAD_DIM),
            s_v.reshape(B, S, SWA_KV_HEADS, HEAD_DIM),
            swa_sinks[l])
        mla_o = latent_attention(c_q, c_kv, k_rope, mla_q_norm_w[l], mla_kv_norm_w[l],
                                 mla_w_uq[l], mla_w_ukv[l], pos)
        mix = jnp.concatenate([ret_o, swa_o, mla_o], axis=-1) @ w_out[l]
        x = layer_norm(DEEPNORM_ALPHA * x + mix, ln1_g[l], ln1_b[l])
        ffn = grouped_moe(x, router_w, router_bias, exp_w_gate_up[l], exp_w_down[l])
        x = layer_norm(DEEPNORM_ALPHA * x + ffn, ln2_g[l], ln2_b[l])
    return x
```

```python
import functools
import math

import numpy as np
import jax
import jax.numpy as jnp
from jax import lax
from jax.experimental import pallas as pl
from jax.experimental.pallas import tpu as pltpu

D_MODEL = 1024
HEAD_DIM = 64
ROPE_THETA = 10000.0

RET_HEADS = 4
RET_DK = 64
RET_DV = 64
RET_W = RET_HEADS * RET_DV

SWA_Q_HEADS = 6
SWA_KV_HEADS = 2
SWA_WINDOW = 128
SWA_W = SWA_Q_HEADS * HEAD_DIM
SWA_KV_W = SWA_KV_HEADS * HEAD_DIM

MLA_HEADS = 6
MLA_Q_RANK = 384
MLA_KV_RANK = 256
MLA_NOPE = 64
MLA_ROPE = 32
MLA_V = 64

N_EXPERTS = 16
N_GROUPS = 4
EXPERTS_PER_GROUP = N_EXPERTS // N_GROUPS
D_EXPERT = 256

LN_EPS = 1e-5
RMS_EPS = 1e-6
NEG_INF = -1e30

LANES = 128
MLA_PAD_W = MLA_HEADS * LANES
VMEM_LIMIT = 56 * 1024 * 1024

F32 = jnp.float32
BF16 = jnp.bfloat16

_C_RQ, _C_RK, _C_RV, _C_RG = 0, 256, 512, 768
_C_SQ, _C_SK, _C_SV = 1024, 1408, 1536
_C_CQ, _C_CKV, _C_KR, _C_END = 1664, 2048, 2304, 2432

_SWA_HEAD_ORDER = (0, 3, 1, 4, 2, 5)


def _params(*sem):
    return pltpu.CompilerParams(dimension_semantics=sem, vmem_limit_bytes=VMEM_LIMIT)


def _dot(a, b):
    return jnp.dot(a, b, preferred_element_type=F32)


def _dot_nt(a, b):
    return lax.dot_general(a, b, (((1,), (1,)), ((), ())), preferred_element_type=F32)


def _dot_tn(a, b):
    return lax.dot_general(a, b, (((0,), (0,)), ((), ())), preferred_element_type=F32)


def _split_dot(x, w_bf16):
    hi = x.astype(BF16)
    lo = (x - hi.astype(F32)).astype(BF16)
    return _dot(hi, w_bf16) + _dot(lo, w_bf16)


def _rope(h, cos, sin_signed, half):
    fwd = pltpu.roll(h, LANES - half, 1)
    bwd = pltpu.roll(h, half, 1)
    lane = lax.broadcasted_iota(jnp.int32, h.shape, 1)
    rot = jnp.where((lane % (2 * half)) < half, fwd, bwd)
    return h * cos + rot * sin_signed


def _layer_norm(y, g, b):
    mu = jnp.mean(y, -1, keepdims=True)
    d = y - mu
    var = jnp.mean(d * d, -1, keepdims=True)
    return d * lax.rsqrt(var + LN_EPS) * g + b


def _rms_norm(x, g):
    return x * lax.rsqrt(jnp.mean(x * x, -1, keepdims=True) + RMS_EPS) * g


def _inproj_kernel(x_ref, w1_ref, cr_ref, sr_ref, cm_ref, sm_ref, qnw_ref, kvnw_ref,
                   wuq_ref, wuk_ref, wuv_ref,
                   rq_ref, rk_ref, rv_ref, rg_ref, sq_ref, sk_ref, sv_ref, mq_ref, mk_ref, mv_ref,
                   *, mla_q_scale):
    xb = x_ref[...].astype(BF16)

    def proj(a, b):
        return _dot(xb, w1_ref[:, a:b])

    cr, sr = cr_ref[...], sr_ref[...]
    for g in range(RET_W // LANES):
        lo, hi = g * LANES, (g + 1) * LANES
        rq_ref[:, lo:hi] = _rope(proj(_C_RQ + lo, _C_RQ + hi), cr, sr, RET_DK // 2).astype(BF16)
        rk = _rope(proj(_C_RK + lo, _C_RK + hi), cr, sr, RET_DK // 2)
        rk_ref[:, lo:hi] = (rk * (RET_DK ** -0.5)).astype(BF16)
    rv_ref[...] = proj(_C_RV, _C_RG).astype(BF16)
    rg_ref[...] = proj(_C_RG, _C_SQ).astype(BF16)

    sq_ref[...] = (proj(_C_SQ, _C_SK) * (HEAD_DIM ** -0.5)).astype(BF16)
    sk_ref[...] = proj(_C_SK, _C_SV).astype(BF16)
    sv_ref[...] = proj(_C_SV, _C_CQ).astype(BF16)

    cm, sm = cm_ref[...], sm_ref[...]
    cq = _rms_norm(proj(_C_CQ, _C_CKV), qnw_ref[...]).astype(BF16)
    q = _dot(cq, wuq_ref[...])
    ckv = _rms_norm(proj(_C_CKV, _C_KR), kvnw_ref[...]).astype(BF16)
    kn = _dot(ckv, wuk_ref[...])
    v = _dot(ckv, wuv_ref[...])
    kpe = _rope(proj(_C_KR, _C_END), cm, sm, MLA_ROPE // 2)
    lane = lax.broadcasted_iota(jnp.int32, kpe.shape, 1)
    ones_col = (lane == MLA_V).astype(F32)
    for h in range(MLA_HEADS):
        lo, hi = h * LANES, (h + 1) * LANES
        mq_ref[:, lo:hi] = (_rope(q[:, lo:hi], cm, sm, MLA_ROPE // 2) * mla_q_scale).astype(BF16)
        mk_ref[:, lo:hi] = (kn[:, lo:hi] + kpe).astype(BF16)
        mv_ref[:, lo:hi] = (v[:, lo:hi] + ones_col).astype(BF16)


def _inproj(x, w1, cr, sr, cm, sm, qnw, kvnw, wuq, wuk, wuv, *, seq, tm=512):
    T = x.shape[0]
    nt = T // tm
    npos = seq // tm
    row = lambda w: pl.BlockSpec((tm, w), lambda i: (i, 0))
    pos = lambda: pl.BlockSpec((tm, LANES), lambda i: (i % npos, 0))
    full = lambda a: pl.BlockSpec(a.shape, lambda i: (0, 0))
    widths = (RET_W, RET_W, RET_W, RET_W, SWA_W, SWA_KV_W, SWA_KV_W, MLA_PAD_W, MLA_PAD_W, MLA_PAD_W)
    mla_q_scale = (MLA_NOPE + MLA_ROPE) ** -0.5 * math.log2(math.e)
    return pl.pallas_call(
        functools.partial(_inproj_kernel, mla_q_scale=mla_q_scale),
        grid=(nt,),
        in_specs=[row(D_MODEL), full(w1), pos(), pos(), pos(), pos(), full(qnw), full(kvnw),
                  full(wuq), full(wuk), full(wuv)],
        out_specs=[row(w) for w in widths],
        out_shape=[jax.ShapeDtypeStruct((T, w), BF16) for w in widths],
        compiler_params=_params("parallel"),
        name="inproj",
    )(x, w1, cr, sr, cm, sm, qnw, kvnw, wuq, wuk, wuv)


def _retention_kernel(q_ref, k_ref, v_ref, g_ref, gnw_ref, dec_ref, xi_ref, zeta_ref, dm_ref, bm_ref,
                      avg_ref, o_ref, state_ref, *, chunk, n_chunks):
    @pl.when(pl.program_id(1) == 0)
    def _():
        state_ref[...] = jnp.zeros_like(state_ref)

    lane_head = lax.broadcasted_iota(jnp.int32, (chunk, RET_W), 1) // RET_DV
    avg = avg_ref[...]
    for c in range(n_chunks):
        rows = slice(c * chunk, (c + 1) * chunk)
        q, k, v = q_ref[rows, :], k_ref[rows, :], v_ref[rows, :]
        zero = jnp.zeros_like(q)
        q_heads = jnp.concatenate([jnp.where(lane_head == h, q, zero) for h in range(RET_HEADS)], 0)
        scores = (_dot_nt(q_heads, k) * dec_ref[...]).astype(BF16)
        inner_heads = _dot(scores, v)
        o = _dot((q.astype(F32) * xi_ref[...]).astype(BF16), state_ref[...].astype(BF16))
        for h in range(RET_HEADS):
            o = o + jnp.where(lane_head == h, inner_heads[h * chunk:(h + 1) * chunk], 0.0)
        upd = _dot_tn((k.astype(F32) * zeta_ref[...]).astype(BF16), v)
        state_ref[...] = dm_ref[...] * state_ref[...] + bm_ref[...] * upd

        mu = _split_dot(o, avg)
        d = o - mu
        var = _split_dot(d * d, avg)
        on = d * lax.rsqrt(var + LN_EPS) * gnw_ref[...]
        gate = g_ref[rows, :].astype(F32)
        o_ref[rows, :] = (on * (gate * jax.nn.sigmoid(gate))).astype(BF16)


def _retention_tables(chunk):
    H = RET_HEADS
    gamma = 1.0 - 2.0 ** (-5.0 - np.arange(H, dtype=np.float64))
    log_g = np.log(gamma)
    idx = np.arange(chunk, dtype=np.float64)
    rel = idx[:, None] - idx[None, :]
    dec = np.where(rel[None] >= 0, np.exp(np.maximum(rel, 0.0)[None] * log_g[:, None, None]), 0.0)
    dec = dec.reshape(H * chunk, chunk)
    lane_head = np.arange(RET_W) // RET_DV
    xi = np.exp((idx[:, None] + 1.0) * log_g[lane_head][None, :])
    zeta = np.exp((chunk - 1.0 - idx[:, None]) * log_g[lane_head][None, :])
    same = lane_head[:, None] == lane_head[None, :]
    dm = np.where(same, np.exp(chunk * log_g)[lane_head][:, None], 0.0)
    bm = same.astype(np.float64)
    avg = bm / RET_DV
    f = lambda a: jnp.asarray(a, F32)
    return f(dec), f(xi), f(zeta), f(dm), f(bm), jnp.asarray(avg, BF16)


def _retention(rq, rk, rv, rg, gnw, *, batch, seq, blk=512, chunk=128):
    T = rq.shape[0]
    nb = seq // blk
    tables = _retention_tables(chunk)
    row = pl.BlockSpec((blk, RET_W), lambda b, i: (b * nb + i, 0))
    full = lambda a: pl.BlockSpec(a.shape, lambda b, i: (0, 0))
    return pl.pallas_call(
        functools.partial(_retention_kernel, chunk=chunk, n_chunks=blk // chunk),
        grid=(batch, nb),
        in_specs=[row, row, row, row, full(gnw)] + [full(t) for t in tables],
        out_specs=row,
        out_shape=jax.ShapeDtypeStruct((T, RET_W), BF16),
        scratch_shapes=[pltpu.VMEM((RET_W, RET_W), F32)],
        compiler_params=_params("parallel", "arbitrary"),
        name="retention",
    )(rq, rk, rv, rg, gnw, *tables)


def _swa_kernel(q_ref, kc_ref, kp_ref, vc_ref, vp_ref, sink_ref, o_ref, *, blk, seq):
    L = SWA_WINDOW
    n_pairs = SWA_Q_HEADS // 2
    lane_lo = lax.broadcasted_iota(jnp.int32, (L, LANES), 1) < HEAD_DIM
    row = lax.broadcasted_iota(jnp.int32, (SWA_Q_HEADS * L, 2 * L), 0) % L
    col = lax.broadcasted_iota(jnp.int32, (SWA_Q_HEADS * L, 2 * L), 1)
    band = (col > row) & (col <= row + L)
    sink = sink_ref[...]
    for sb in range(blk // L):
        r0 = sb * L
        pos0 = (pl.program_id(0) * blk + r0) % seq
        if sb == 0:
            k_prev, v_prev = kp_ref[...], vp_ref[...]
        else:
            k_prev, v_prev = kc_ref[r0 - L:r0, :], vc_ref[r0 - L:r0, :]
        k_win = jnp.concatenate([k_prev, kc_ref[r0:r0 + L, :]], 0)
        v_win = jnp.concatenate([v_prev, vc_ref[r0:r0 + L, :]], 0)
        pairs = [q_ref[r0:r0 + L, g * LANES:(g + 1) * LANES] for g in range(n_pairs)]
        zero = jnp.zeros_like(pairs[0])
        q_heads = jnp.concatenate([jnp.where(lane_lo, p, zero) for p in pairs]
                                  + [jnp.where(lane_lo, zero, p) for p in pairs], 0)
        s = _dot_nt(q_heads, k_win)
        first_key = jnp.where(pos0 > 0, 0, L)
        valid = band & (col >= first_key)
        s = jnp.where(valid, s, NEG_INF)
        m = jnp.maximum(jnp.max(s, -1, keepdims=True), sink)
        p = jnp.exp(s - m)
        denom = jnp.sum(p, -1, keepdims=True) + jnp.exp(sink - m)
        o = _dot((p / denom).astype(BF16), v_win)
        for g in range(n_pairs):
            o_lo = o[g * L:(g + 1) * L]
            o_hi = o[(g + n_pairs) * L:(g + n_pairs + 1) * L]
            o_ref[r0:r0 + L, g * LANES:(g + 1) * LANES] = jnp.where(lane_lo, o_lo, o_hi).astype(BF16)


def _swa(sq, sk, sv, sink_col, *, seq, blk=512):
    T = sq.shape[0]
    L = SWA_WINDOW
    per = blk // L
    cur = lambda w: pl.BlockSpec((blk, w), lambda i: (i, 0))
    prev = pl.BlockSpec((L, SWA_KV_W), lambda i: (jnp.maximum(i * per - 1, 0), 0))
    return pl.pallas_call(
        functools.partial(_swa_kernel, blk=blk, seq=seq),
        grid=(T // blk,),
        in_specs=[cur(SWA_W), cur(SWA_KV_W), prev, cur(SWA_KV_W), prev,
                  pl.BlockSpec(sink_col.shape, lambda i: (0, 0))],
        out_specs=cur(SWA_W),
        out_shape=jax.ShapeDtypeStruct((T, SWA_W), BF16),
        compiler_params=_params("parallel"),
        name="swa",
    )(sq, sk, sk, sv, sv, sink_col)


def _mla_kernel(q_ref, k_ref, v_ref, o_ref, *, tq):
    i = pl.program_id(2)
    q = q_ref[...]
    row = lax.broadcasted_iota(jnp.int32, (tq, tq), 0)
    col = lax.broadcasted_iota(jnp.int32, (tq, tq), 1)

    def step(j, carry, masked):
        m, acc = carry
        start = pl.multiple_of(j * tq, tq)
        s = _dot_nt(q, k_ref[pl.ds(start, tq), :])
        if masked:
            s = jnp.where(col <= row, s, NEG_INF)
        m_new = jnp.maximum(m, jnp.max(s, -1, keepdims=True))
        p = jnp.exp2(s - m_new).astype(BF16)
        acc = jnp.exp2(m - m_new) * acc + _dot(p, v_ref[pl.ds(start, tq), :])
        return m_new, acc

    carry = (jnp.full((tq, 1), NEG_INF, F32), jnp.zeros((tq, LANES), F32))
    carry = lax.fori_loop(0, i, functools.partial(step, masked=False), carry)
    _, acc = step(i, carry, True)
    o_ref[...] = (acc / acc[:, MLA_V:MLA_V + 1]).astype(BF16)


def _mla(mq, mk, mv, *, batch, seq, tq=512):
    T = mq.shape[0]
    nq = seq // tq
    qo = pl.BlockSpec((tq, LANES), lambda b, h, i: (b * nq + i, h))
    kv = pl.BlockSpec((seq, LANES), lambda b, h, i: (b, h))
    return pl.pallas_call(
        functools.partial(_mla_kernel, tq=tq),
        grid=(batch, MLA_HEADS, nq),
        in_specs=[qo, kv, kv],
        out_specs=qo,
        out_shape=jax.ShapeDtypeStruct((T, MLA_PAD_W), BF16),
        compiler_params=_params("parallel", "parallel", "arbitrary"),
        name="mla",
    )(mq, mk, mv)


def _group_partner(x, d, width, lane):
    return jnp.where((lane % width) + d < width,
                     pltpu.roll(x, LANES - d, 1), pltpu.roll(x, width - d, 1))


def _router_gates(logits, bias):
    lane = lax.broadcasted_iota(jnp.int32, logits.shape, 1)
    is_expert = lane < N_EXPERTS
    scores = jax.nn.sigmoid(logits)
    biased = jnp.where(is_expert, scores + bias, NEG_INF)
    G = EXPERTS_PER_GROUP
    member = lane % G
    rank = jnp.zeros(logits.shape, jnp.int32)
    for d in range(1, G):
        other = _group_partner(biased, d, G, lane)
        other_first = (member + d) % G < member
        rank = rank + ((other > biased) | ((other == biased) & other_first)).astype(jnp.int32)
    top2 = rank < 2
    grp_score = jnp.where(top2, biased, 0.0)
    part = grp_score
    for d in range(1, G):
        grp_score = grp_score + _group_partner(part, d, G, lane)
    beaten = jnp.zeros(logits.shape, jnp.bool_)
    grp = lane // G
    for dg in range(1, N_GROUPS):
        other = _group_partner(grp_score, dg * G, N_EXPERTS, lane)
        other_first = (grp + dg) % N_GROUPS < grp
        beaten = beaten | (other > grp_score) | ((other == grp_score) & other_first)
    sel = is_expert & top2 & jnp.logical_not(beaten)
    picked = jnp.where(sel, scores, 0.0)
    return picked / jnp.sum(picked, -1, keepdims=True)


def _outproj_kernel(ret_ref, swa_ref, mla_ref, x_ref, wr_ref, ws_ref, wm_ref, g_ref, b_ref,
                    rwh_ref, rwl_ref, rb_ref, x1_ref, x1b_ref, gates_ref, *, alpha):
    mix = _dot(ret_ref[...], wr_ref[...]) + _dot(swa_ref[...], ws_ref[...]) + _dot(mla_ref[...], wm_ref[...])
    x1 = _layer_norm(alpha * x_ref[...] + mix, g_ref[...], b_ref[...])
    x1_ref[...] = x1
    hi = x1.astype(BF16)
    x1b_ref[...] = hi
    lo = (x1 - hi.astype(F32)).astype(BF16)
    logits = _dot(hi, rwh_ref[...]) + _dot(lo, rwh_ref[...]) + _dot(hi, rwl_ref[...])
    gates_ref[...] = _router_gates(logits, rb_ref[...])


def _outproj(ret_o, swa_o, mla_o, x, wr, ws, wm, g, b, rwh, rwl, rb, *, alpha, tm=512):
    T = x.shape[0]
    row = lambda w: pl.BlockSpec((tm, w), lambda i: (i, 0))
    full = lambda a: pl.BlockSpec(a.shape, lambda i: (0, 0))
    return pl.pallas_call(
        functools.partial(_outproj_kernel, alpha=alpha),
        grid=(T // tm,),
        in_specs=[row(RET_W), row(SWA_W), row(MLA_PAD_W), row(D_MODEL), full(wr), full(ws), full(wm),
                  full(g), full(b), full(rwh), full(rwl), full(rb)],
        out_specs=[row(D_MODEL), row(D_MODEL), row(LANES)],
        out_shape=[jax.ShapeDtypeStruct((T, D_MODEL), F32), jax.ShapeDtypeStruct((T, D_MODEL), BF16),
                   jax.ShapeDtypeStruct((T, LANES), F32)],
        compiler_params=_params("parallel"),
        name="outproj",
    )(ret_o, swa_o, mla_o, x, wr, ws, wm, g, b, rwh, rwl, rb)


def _moe_kernel(xb_ref, x_ref, gates_ref, wgu_ref, wd_ref, g_ref, b_ref, o_ref, acc_ref, *, alpha):
    e = pl.program_id(1)

    @pl.when(e == 0)
    def _():
        acc_ref[...] = jnp.zeros_like(acc_ref)

    gates = gates_ref[...]
    lane = lax.broadcasted_iota(jnp.int32, gates.shape, 1)
    gate = jnp.sum(jnp.where(lane == e, gates, 0.0), -1, keepdims=True)
    h = _dot(xb_ref[...], wgu_ref[0])
    up, lin = h[:, :D_EXPERT], h[:, D_EXPERT:]
    a = up * jax.nn.sigmoid(up) * lin * gate
    acc_ref[...] += _dot(a.astype(BF16), wd_ref[0])

    @pl.when(e == N_EXPERTS - 1)
    def _():
        o_ref[...] = _layer_norm(alpha * x_ref[...] + acc_ref[...], g_ref[...], b_ref[...])


def _moe(x1b, x1, gates, wgu, wd, g, b, *, alpha, tm=1024):
    T = x1.shape[0]
    row = lambda w: pl.BlockSpec((tm, w), lambda i, e: (i, 0))
    full = lambda a: pl.BlockSpec(a.shape, lambda i, e: (0, 0))
    return pl.pallas_call(
        functools.partial(_moe_kernel, alpha=alpha),
        grid=(T // tm, N_EXPERTS),
        in_specs=[row(D_MODEL), row(D_MODEL), row(LANES),
                  pl.BlockSpec((1, D_MODEL, 2 * D_EXPERT), lambda i, e: (e, 0, 0)),
                  pl.BlockSpec((1, D_EXPERT, D_MODEL), lambda i, e: (e, 0, 0)),
                  full(g), full(b)],
        out_specs=row(D_MODEL),
        out_shape=jax.ShapeDtypeStruct((T, D_MODEL), F32),
        scratch_shapes=[pltpu.VMEM((tm, D_MODEL), F32)],
        compiler_params=_params("parallel", "arbitrary"),
        name="moe",
    )(x1b, x1, gates, wgu, wd, g, b)


def _rope_tables(seq):
    pos = np.arange(seq, dtype=np.float64)[:, None]
    lane = np.arange(LANES)

    def table(dim, active, offset):
        j = (lane - offset) % dim
        inv = ROPE_THETA ** (-(2.0 * (j % (dim // 2))) / dim)
        ang = pos * inv[None, :]
        sign = np.where(j < dim // 2, -1.0, 1.0)
        cos = np.where(active[None, :], np.cos(ang), 1.0)
        sin = np.where(active[None, :], np.sin(ang) * sign[None, :], 0.0)
        return jnp.asarray(cos, F32), jnp.asarray(sin, F32)

    cr, sr = table(RET_DK, np.ones(LANES, bool), 0)
    cm, sm = table(MLA_ROPE, (lane >= MLA_NOPE) & (lane < MLA_NOPE + MLA_ROPE), MLA_NOPE)
    return cr, sr, cm, sm


def _layer_weights(l, w_in, mla_w_uq, mla_w_ukv, w_out):
    w = w_in[l]
    sizes = (256, 256, 256, 256, SWA_W, SWA_KV_W, SWA_KV_W, MLA_Q_RANK, MLA_KV_RANK, MLA_ROPE)
    offs = np.concatenate([[0], np.cumsum(sizes)])
    seg = [w[:, offs[i]:offs[i + 1]] for i in range(len(sizes))]
    sq = jnp.concatenate([seg[4][:, h * HEAD_DIM:(h + 1) * HEAD_DIM] for h in _SWA_HEAD_ORDER], 1)
    kr = jnp.pad(seg[9], ((0, 0), (MLA_NOPE, LANES - MLA_NOPE - MLA_ROPE)))
    w1 = jnp.concatenate(seg[:4] + [sq] + seg[5:9] + [kr], 1).astype(BF16)

    def pad_heads(cols):
        return jnp.concatenate([jnp.pad(c, ((0, 0), (0, LANES - c.shape[1]))) for c in cols], 1)

    dq = MLA_NOPE + MLA_ROPE
    wuq = pad_heads([mla_w_uq[l][:, h * dq:(h + 1) * dq] for h in range(MLA_HEADS)]).astype(BF16)
    dkv = MLA_NOPE + MLA_V
    wuk = pad_heads([mla_w_ukv[l][:, h * dkv:h * dkv + MLA_NOPE] for h in range(MLA_HEADS)]).astype(BF16)
    wuv = pad_heads([mla_w_ukv[l][:, h * dkv + MLA_NOPE:(h + 1) * dkv] for h in range(MLA_HEADS)]).astype(BF16)

    wo = w_out[l]
    wr = wo[:RET_W].astype(BF16)
    ws = jnp.concatenate([wo[RET_W + h * HEAD_DIM:RET_W + (h + 1) * HEAD_DIM] for h in _SWA_HEAD_ORDER],
                         0).astype(BF16)
    base = RET_W + SWA_W
    wm = jnp.concatenate([jnp.pad(wo[base + h * MLA_V:base + (h + 1) * MLA_V], ((0, LANES - MLA_V), (0, 0)))
                          for h in range(MLA_HEADS)], 0).astype(BF16)
    return w1, wuq, wuk, wuv, wr, ws, wm


def kernel(x, w_in, ret_gn_w, swa_sinks, mla_q_norm_w, mla_kv_norm_w, mla_w_uq, mla_w_ukv, w_out,
           ln1_g, ln1_b, router_w, router_bias, exp_w_gate_up, exp_w_down, ln2_g, ln2_b):
    batch, seq, _ = x.shape
    depth = w_in.shape[0]
    alpha = (2 * depth) ** 0.25
    cr, sr, cm, sm = _rope_tables(seq)

    rw = jnp.pad(router_w, ((0, 0), (0, LANES - N_EXPERTS)))
    rwh = rw.astype(BF16)
    rwl = (rw - rwh.astype(F32)).astype(BF16)
    rb = jnp.pad(router_bias.astype(F32), (0, LANES - N_EXPERTS)).reshape(1, LANES)

    t = x.reshape(batch * seq, D_MODEL)
    for l in range(depth):
        w1, wuq, wuk, wuv, wr, ws, wm = _layer_weights(l, w_in, mla_w_uq, mla_w_ukv, w_out)
        rq, rk, rv, rg, sq, sk, sv, mq, mk, mv = _inproj(
            t, w1, cr, sr, cm, sm, mla_q_norm_w[l].reshape(1, -1), mla_kv_norm_w[l].reshape(1, -1),
            wuq, wuk, wuv, seq=seq)
        ret_o = _retention(rq, rk, rv, rg, ret_gn_w[l].reshape(1, -1), batch=batch, seq=seq)
        sink_col = jnp.repeat(swa_sinks[l].astype(F32), SWA_WINDOW).reshape(-1, 1)
        swa_o = _swa(sq, sk, sv, sink_col, seq=seq)
        mla_o = _mla(mq, mk, mv, batch=batch, seq=seq)
        x1, x1b, gates = _outproj(ret_o, swa_o, mla_o, t, wr, ws, wm,
                                  ln1_g[l].reshape(1, -1), ln1_b[l].reshape(1, -1), rwh, rwl, rb, alpha=alpha)
        t = _moe(x1b, x1, gates, exp_w_gate_up[l].astype(BF16), exp_w_down[l].astype(BF16),
                 ln2_g[l].reshape(1, -1), ln2_b[l].reshape(1, -1), alpha=alpha)
    return t.reshape(batch, seq, D_MODEL)
```

```python
import functools
import math

import numpy as np
import jax
import jax.numpy as jnp
from jax import lax
from jax.experimental import pallas as pl
from jax.experimental.pallas import tpu as pltpu

D_MODEL = 1024
HEAD_DIM = 64
ROPE_THETA = 10000.0

RET_HEADS = 4
RET_DK = 64
RET_DV = 64
RET_W = RET_HEADS * RET_DV

SWA_Q_HEADS = 6
SWA_KV_HEADS = 2
SWA_WINDOW = 128
SWA_W = SWA_Q_HEADS * HEAD_DIM
SWA_KV_W = SWA_KV_HEADS * HEAD_DIM

MLA_HEADS = 6
MLA_Q_RANK = 384
MLA_KV_RANK = 256
MLA_NOPE = 64
MLA_ROPE = 32
MLA_V = 64

N_EXPERTS = 16
N_GROUPS = 4
EXPERTS_PER_GROUP = N_EXPERTS // N_GROUPS
D_EXPERT = 256

LN_EPS = 1e-5
RMS_EPS = 1e-6
NEG_INF = -1e30

LANES = 128
MLA_PAD_W = MLA_HEADS * LANES
VMEM_LIMIT = 56 * 1024 * 1024

F32 = jnp.float32
BF16 = jnp.bfloat16

_C_RQ, _C_RK, _C_RV, _C_RG = 0, 256, 512, 768
_C_SQ, _C_SK, _C_SV = 1024, 1408, 1536
_C_CQ, _C_CKV, _C_KR, _C_END = 1664, 2048, 2304, 2432

_SWA_HEAD_ORDER = (0, 3, 1, 4, 2, 5)


def _params(*sem):
    return pltpu.CompilerParams(dimension_semantics=sem, vmem_limit_bytes=VMEM_LIMIT)


def _dot(a, b):
    return jnp.dot(a, b, preferred_element_type=F32)


def _dot_nt(a, b):
    return lax.dot_general(a, b, (((1,), (1,)), ((), ())), preferred_element_type=F32)


def _dot_tn(a, b):
    return lax.dot_general(a, b, (((0,), (0,)), ((), ())), preferred_element_type=F32)


def _split_dot(x, w_bf16):
    hi = x.astype(BF16)
    lo = (x - hi.astype(F32)).astype(BF16)
    return _dot(hi, w_bf16) + _dot(lo, w_bf16)


def _rope(h, cos, sin_signed, half):
    fwd = pltpu.roll(h, LANES - half, 1)
    bwd = pltpu.roll(h, half, 1)
    lane = lax.broadcasted_iota(jnp.int32, h.shape, 1)
    rot = jnp.where((lane % (2 * half)) < half, fwd, bwd)
    return h * cos + rot * sin_signed


def _layer_norm(y, g, b):
    mu = jnp.mean(y, -1, keepdims=True)
    d = y - mu
    var = jnp.mean(d * d, -1, keepdims=True)
    return d * lax.rsqrt(var + LN_EPS) * g + b


def _rms_norm(x, g):
    return x * lax.rsqrt(jnp.mean(x * x, -1, keepdims=True) + RMS_EPS) * g


def _inproj_kernel(x_ref, w1_ref, cr_ref, sr_ref, cm_ref, sm_ref, qnw_ref, kvnw_ref,
                   wuq_ref, wuk_ref, wuv_ref,
                   rq_ref, rk_ref, rv_ref, rg_ref, sq_ref, sk_ref, sv_ref, mq_ref, mk_ref, mv_ref,
                   *, mla_q_scale):
    xb = x_ref[...].astype(BF16)

    def proj(a, b):
        return _dot(xb, w1_ref[:, a:b])

    cr, sr = cr_ref[...], sr_ref[...]
    for g in range(RET_W // LANES):
        lo, hi = g * LANES, (g + 1) * LANES
        rq_ref[:, lo:hi] = _rope(proj(_C_RQ + lo, _C_RQ + hi), cr, sr, RET_DK // 2).astype(BF16)
        rk = _rope(proj(_C_RK + lo, _C_RK + hi), cr, sr, RET_DK // 2)
        rk_ref[:, lo:hi] = (rk * (RET_DK ** -0.5)).astype(BF16)
    rv_ref[...] = proj(_C_RV, _C_RG).astype(BF16)
    rg_ref[...] = proj(_C_RG, _C_SQ).astype(BF16)

    sq_ref[...] = (proj(_C_SQ, _C_SK) * (HEAD_DIM ** -0.5)).astype(BF16)
    sk_ref[...] = proj(_C_SK, _C_SV).astype(BF16)
    sv_ref[...] = proj(_C_SV, _C_CQ).astype(BF16)

    cm, sm = cm_ref[...], sm_ref[...]
    cq = _rms_norm(proj(_C_CQ, _C_CKV), qnw_ref[...]).astype(BF16)
    q = _dot(cq, wuq_ref[...])
    ckv = _rms_norm(proj(_C_CKV, _C_KR), kvnw_ref[...]).astype(BF16)
    kn = _dot(ckv, wuk_ref[...])
    v = _dot(ckv, wuv_ref[...])
    kpe = _rope(proj(_C_KR, _C_END), cm, sm, MLA_ROPE // 2)
    lane = lax.broadcasted_iota(jnp.int32, kpe.shape, 1)
    ones_col = (lane == MLA_V).astype(F32)
    for h in range(MLA_HEADS):
        lo, hi = h * LANES, (h + 1) * LANES
        mq_ref[:, lo:hi] = (_rope(q[:, lo:hi], cm, sm, MLA_ROPE // 2) * mla_q_scale).astype(BF16)
        mk_ref[:, lo:hi] = (kn[:, lo:hi] + kpe).astype(BF16)
        mv_ref[:, lo:hi] = (v[:, lo:hi] + ones_col).astype(BF16)


def _inproj(x, w1, cr, sr, cm, sm, qnw, kvnw, wuq, wuk, wuv, *, seq, tm=512):
    T = x.shape[0]
    nt = T // tm
    npos = seq // tm
    row = lambda w: pl.BlockSpec((tm, w), lambda i: (i, 0))
    pos = lambda: pl.BlockSpec((tm, LANES), lambda i: (i % npos, 0))
    full = lambda a: pl.BlockSpec(a.shape, lambda i: (0, 0))
    widths = (RET_W, RET_W, RET_W, RET_W, SWA_W, SWA_KV_W, SWA_KV_W, MLA_PAD_W, MLA_PAD_W, MLA_PAD_W)
    mla_q_scale = (MLA_NOPE + MLA_ROPE) ** -0.5 * math.log2(math.e)
    return pl.pallas_call(
        functools.partial(_inproj_kernel, mla_q_scale=mla_q_scale),
        grid=(nt,),
        in_specs=[row(D_MODEL), full(w1), pos(), pos(), pos(), pos(), full(qnw), full(kvnw),
                  full(wuq), full(wuk), full(wuv)],
        out_specs=[row(w) for w in widths],
        out_shape=[jax.ShapeDtypeStruct((T, w), BF16) for w in widths],
        compiler_params=_params("parallel"),
        name="inproj",
    )(x, w1, cr, sr, cm, sm, qnw, kvnw, wuq, wuk, wuv)


def _retention_kernel(q_ref, k_ref, v_ref, g_ref, gnw_ref, dec_ref, xi_ref, zeta_ref, dm_ref, bm_ref,
                      avg_ref, o_ref, state_ref, *, chunk, n_chunks):
    @pl.when(pl.program_id(1) == 0)
    def _():
        state_ref[...] = jnp.zeros_like(state_ref)

    lane_head = lax.broadcasted_iota(jnp.int32, (chunk, RET_W), 1) // RET_DV
    avg = avg_ref[...]
    for c in range(n_chunks):
        rows = slice(c * chunk, (c + 1) * chunk)
        q, k, v = q_ref[rows, :], k_ref[rows, :], v_ref[rows, :]
        zero = jnp.zeros_like(q)
        q_heads = jnp.concatenate([jnp.where(lane_head == h, q, zero) for h in range(RET_HEADS)], 0)
        scores = (_dot_nt(q_heads, k) * dec_ref[...]).astype(BF16)
        inner_heads = _dot(scores, v)
        o = _dot((q.astype(F32) * xi_ref[...]).astype(BF16), state_ref[...].astype(BF16))
        for h in range(RET_HEADS):
            o = o + jnp.where(lane_head == h, inner_heads[h * chunk:(h + 1) * chunk], 0.0)
        upd = _dot_tn((k.astype(F32) * zeta_ref[...]).astype(BF16), v)
        state_ref[...] = dm_ref[...] * state_ref[...] + bm_ref[...] * upd

        mu = _split_dot(o, avg)
        d = o - mu
        var = _split_dot(d * d, avg)
        on = d * lax.rsqrt(var + LN_EPS) * gnw_ref[...]
        gate = g_ref[rows, :].astype(F32)
        o_ref[rows, :] = (on * (gate * jax.nn.sigmoid(gate))).astype(BF16)


def _retention_tables(chunk):
    H = RET_HEADS
    gamma = 1.0 - 2.0 ** (-5.0 - np.arange(H, dtype=np.float64))
    log_g = np.log(gamma)
    idx = np.arange(chunk, dtype=np.float64)
    rel = idx[:, None] - idx[None, :]
    dec = np.where(rel[None] >= 0, np.exp(np.maximum(rel, 0.0)[None] * log_g[:, None, None]), 0.0)
    dec = dec.reshape(H * chunk, chunk)
    lane_head = np.arange(RET_W) // RET_DV
    xi = np.exp((idx[:, None] + 1.0) * log_g[lane_head][None, :])
    zeta = np.exp((chunk - 1.0 - idx[:, None]) * log_g[lane_head][None, :])
    same = lane_head[:, None] == lane_head[None, :]
    dm = np.where(same, np.exp(chunk * log_g)[lane_head][:, None], 0.0)
    bm = same.astype(np.float64)
    avg = bm / RET_DV
    f = lambda a: jnp.asarray(a, F32)
    return f(dec), f(xi), f(zeta), f(dm), f(bm), jnp.asarray(avg, BF16)


def _retention(rq, rk, rv, rg, gnw, *, batch, seq, blk=512, chunk=128):
    T = rq.shape[0]
    nb = seq // blk
    tables = _retention_tables(chunk)
    row = pl.BlockSpec((blk, RET_W), lambda b, i: (b * nb + i, 0))
    full = lambda a: pl.BlockSpec(a.shape, lambda b, i: (0, 0))
    return pl.pallas_call(
        functools.partial(_retention_kernel, chunk=chunk, n_chunks=blk // chunk),
        grid=(batch, nb),
        in_specs=[row, row, row, row, full(gnw)] + [full(t) for t in tables],
        out_specs=row,
        out_shape=jax.ShapeDtypeStruct((T, RET_W), BF16),
        scratch_shapes=[pltpu.VMEM((RET_W, RET_W), F32)],
        compiler_params=_params("parallel", "arbitrary"),
        name="retention",
    )(rq, rk, rv, rg, gnw, *tables)


def _swa_kernel(q_ref, kc_ref, kp_ref, vc_ref, vp_ref, sink_ref, o_ref, *, blk, seq):
    L = SWA_WINDOW
    n_pairs = SWA_Q_HEADS // 2
    lane_lo = lax.broadcasted_iota(jnp.int32, (L, LANES), 1) < HEAD_DIM
    row = lax.broadcasted_iota(jnp.int32, (SWA_Q_HEADS * L, 2 * L), 0) % L
    col = lax.broadcasted_iota(jnp.int32, (SWA_Q_HEADS * L, 2 * L), 1)
    band = (col > row) & (col <= row + L)
    sink = sink_ref[...]
    for sb in range(blk // L):
        r0 = sb * L
        pos0 = (pl.program_id(0) * blk + r0) % seq
        if sb == 0:
            k_prev, v_prev = kp_ref[...], vp_ref[...]
        else:
            k_prev, v_prev = kc_ref[r0 - L:r0, :], vc_ref[r0 - L:r0, :]
        k_win = jnp.concatenate([k_prev, kc_ref[r0:r0 + L, :]], 0)
        v_win = jnp.concatenate([v_prev, vc_ref[r0:r0 + L, :]], 0)
        pairs = [q_ref[r0:r0 + L, g * LANES:(g + 1) * LANES] for g in range(n_pairs)]
        zero = jnp.zeros_like(pairs[0])
        q_heads = jnp.concatenate([jnp.where(lane_lo, p, zero) for p in pairs]
                                  + [jnp.where(lane_lo, zero, p) for p in pairs], 0)
        s = _dot_nt(q_heads, k_win)
        first_key = jnp.where(pos0 > 0, 0, L)
        valid = band & (col >= first_key)
        s = jnp.where(valid, s, NEG_INF)
        m = jnp.maximum(jnp.max(s, -1, keepdims=True), sink)
        p = jnp.exp(s - m)
        denom = jnp.sum(p, -1, keepdims=True) + jnp.exp(sink - m)
        o = _dot((p / denom).astype(BF16), v_win)
        for g in range(n_pairs):
            o_lo = o[g * L:(g + 1) * L]
            o_hi = o[(g + n_pairs) * L:(g + n_pairs + 1) * L]
            o_ref[r0:r0 + L, g * LANES:(g + 1) * LANES] = jnp.where(lane_lo, o_lo, o_hi).astype(BF16)


def _swa(sq, sk, sv, sink_col, *, seq, blk=512):
    T = sq.shape[0]
    L = SWA_WINDOW
    per = blk // L
    cur = lambda w: pl.BlockSpec((blk, w), lambda i: (i, 0))
    prev = pl.BlockSpec((L, SWA_KV_W), lambda i: (jnp.maximum(i * per - 1, 0), 0))
    return pl.pallas_call(
        functools.partial(_swa_kernel, blk=blk, seq=seq),
        grid=(T // blk,),
        in_specs=[cur(SWA_W), cur(SWA_KV_W), prev, cur(SWA_KV_W), prev,
                  pl.BlockSpec(sink_col.shape, lambda i: (0, 0))],
        out_specs=cur(SWA_W),
        out_shape=jax.ShapeDtypeStruct((T, SWA_W), BF16),
        compiler_params=_params("parallel"),
        name="swa",
    )(sq, sk, sk, sv, sv, sink_col)


def _mla_kernel(q_ref, k_ref, v_ref, o_ref, *, tq, heads):
    i = pl.program_id(2)
    row = lax.broadcasted_iota(jnp.int32, (tq, tq), 0)
    col = lax.broadcasted_iota(jnp.int32, (tq, tq), 1)

    def step(j, carry, masked):
        start = pl.multiple_of(j * tq, tq)
        out = []
        for h in range(heads):
            m, acc = carry[h]
            lanes = slice(h * LANES, (h + 1) * LANES)
            s = _dot_nt(q_ref[:, lanes], k_ref[pl.ds(start, tq), lanes])
            if masked:
                s = jnp.where(col <= row, s, NEG_INF)
            m_new = jnp.maximum(m, jnp.max(s, -1, keepdims=True))
            p = jnp.exp2(s - m_new).astype(BF16)
            acc = jnp.exp2(m - m_new) * acc + _dot(p, v_ref[pl.ds(start, tq), lanes])
            out.append((m_new, acc))
        return tuple(out)

    init = (jnp.full((tq, 1), NEG_INF, F32), jnp.zeros((tq, LANES), F32))
    carry = lax.fori_loop(0, i, functools.partial(step, masked=False), (init,) * heads)
    carry = step(i, carry, True)
    for h in range(heads):
        acc = carry[h][1]
        o_ref[:, h * LANES:(h + 1) * LANES] = (acc / acc[:, MLA_V:MLA_V + 1]).astype(BF16)


def _mla(mq, mk, mv, *, batch, seq, tq=512, heads=3):
    T = mq.shape[0]
    nq = seq // tq
    qo = pl.BlockSpec((tq, heads * LANES), lambda b, h, i: (b * nq + i, h))
    kv = pl.BlockSpec((seq, heads * LANES), lambda b, h, i: (b, h))
    return pl.pallas_call(
        functools.partial(_mla_kernel, tq=tq, heads=heads),
        grid=(batch, MLA_HEADS // heads, nq),
        in_specs=[qo, kv, kv],
        out_specs=qo,
        out_shape=jax.ShapeDtypeStruct((T, MLA_PAD_W), BF16),
        compiler_params=_params("parallel", "parallel", "arbitrary"),
        name="mla",
    )(mq, mk, mv)


def _group_partner(x, d, width, lane):
    return jnp.where((lane % width) + d < width,
                     pltpu.roll(x, LANES - d, 1), pltpu.roll(x, width - d, 1))


def _router_gates(logits, bias):
    lane = lax.broadcasted_iota(jnp.int32, logits.shape, 1)
    is_expert = lane < N_EXPERTS
    scores = jax.nn.sigmoid(logits)
    biased = jnp.where(is_expert, scores + bias, NEG_INF)
    G = EXPERTS_PER_GROUP
    member = lane % G
    rank = jnp.zeros(logits.shape, jnp.int32)
    for d in range(1, G):
        other = _group_partner(biased, d, G, lane)
        other_first = (member + d) % G < member
        rank = rank + ((other > biased) | ((other == biased) & other_first)).astype(jnp.int32)
    top2 = rank < 2
    grp_score = jnp.where(top2, biased, 0.0)
    part = grp_score
    for d in range(1, G):
        grp_score = grp_score + _group_partner(part, d, G, lane)
    beaten = jnp.zeros(logits.shape, jnp.bool_)
    grp = lane // G
    for dg in range(1, N_GROUPS):
        other = _group_partner(grp_score, dg * G, N_EXPERTS, lane)
        other_first = (grp + dg) % N_GROUPS < grp
        beaten = beaten | (other > grp_score) | ((other == grp_score) & other_first)
    sel = is_expert & top2 & jnp.logical_not(beaten)
    picked = jnp.where(sel, scores, 0.0)
    return picked / jnp.sum(picked, -1, keepdims=True)


def _outproj_kernel(ret_ref, swa_ref, mla_ref, x_ref, wr_ref, ws_ref, wm_ref, g_ref, b_ref,
                    rwh_ref, rwl_ref, rb_ref, x1_ref, x1b_ref, gates_ref, *, alpha):
    mix = _dot(ret_ref[...], wr_ref[...]) + _dot(swa_ref[...], ws_ref[...]) + _dot(mla_ref[...], wm_ref[...])
    x1 = _layer_norm(alpha * x_ref[...] + mix, g_ref[...], b_ref[...])
    x1_ref[...] = x1
    hi = x1.astype(BF16)
    x1b_ref[...] = hi
    lo = (x1 - hi.astype(F32)).astype(BF16)
    logits = _dot(hi, rwh_ref[...]) + _dot(lo, rwh_ref[...]) + _dot(hi, rwl_ref[...])
    gates_ref[...] = _router_gates(logits, rb_ref[...])


def _outproj(ret_o, swa_o, mla_o, x, wr, ws, wm, g, b, rwh, rwl, rb, *, alpha, tm=512):
    T = x.shape[0]
    row = lambda w: pl.BlockSpec((tm, w), lambda i: (i, 0))
    full = lambda a: pl.BlockSpec(a.shape, lambda i: (0, 0))
    return pl.pallas_call(
        functools.partial(_outproj_kernel, alpha=alpha),
        grid=(T // tm,),
        in_specs=[row(RET_W), row(SWA_W), row(MLA_PAD_W), row(D_MODEL), full(wr), full(ws), full(wm),
                  full(g), full(b), full(rwh), full(rwl), full(rb)],
        out_specs=[row(D_MODEL), row(D_MODEL), row(LANES)],
        out_shape=[jax.ShapeDtypeStruct((T, D_MODEL), F32), jax.ShapeDtypeStruct((T, D_MODEL), BF16),
                   jax.ShapeDtypeStruct((T, LANES), F32)],
        compiler_params=_params("parallel"),
        name="outproj",
    )(ret_o, swa_o, mla_o, x, wr, ws, wm, g, b, rwh, rwl, rb)


def _moe_kernel(xb_ref, x_ref, gates_ref, wgu_ref, wd_ref, g_ref, b_ref, o_ref, acc_ref, *, alpha):
    e = pl.program_id(1)

    @pl.when(e == 0)
    def _():
        acc_ref[...] = jnp.zeros_like(acc_ref)

    gates = gates_ref[...]
    lane = lax.broadcasted_iota(jnp.int32, gates.shape, 1)
    gate = jnp.sum(jnp.where(lane == e, gates, 0.0), -1, keepdims=True)
    h = _dot(xb_ref[...], wgu_ref[0])
    up, lin = h[:, :D_EXPERT], h[:, D_EXPERT:]
    a = up * jax.nn.sigmoid(up) * lin * gate
    acc_ref[...] += _dot(a.astype(BF16), wd_ref[0])

    @pl.when(e == N_EXPERTS - 1)
    def _():
        o_ref[...] = _layer_norm(alpha * x_ref[...] + acc_ref[...], g_ref[...], b_ref[...])


def _moe(x1b, x1, gates, wgu, wd, g, b, *, alpha, tm=1024):
    T = x1.shape[0]
    row = lambda w: pl.BlockSpec((tm, w), lambda i, e: (i, 0))
    full = lambda a: pl.BlockSpec(a.shape, lambda i, e: (0, 0))
    return pl.pallas_call(
        functools.partial(_moe_kernel, alpha=alpha),
        grid=(T // tm, N_EXPERTS),
        in_specs=[row(D_MODEL), row(D_MODEL), row(LANES),
                  pl.BlockSpec((1, D_MODEL, 2 * D_EXPERT), lambda i, e: (e, 0, 0)),
                  pl.BlockSpec((1, D_EXPERT, D_MODEL), lambda i, e: (e, 0, 0)),
                  full(g), full(b)],
        out_specs=row(D_MODEL),
        out_shape=jax.ShapeDtypeStruct((T, D_MODEL), F32),
        scratch_shapes=[pltpu.VMEM((tm, D_MODEL), F32)],
        compiler_params=_params("parallel", "arbitrary"),
        name="moe",
    )(x1b, x1, gates, wgu, wd, g, b)


def _rope_tables(seq):
    pos = np.arange(seq, dtype=np.float64)[:, None]
    lane = np.arange(LANES)

    def table(dim, active, offset):
        j = (lane - offset) % dim
        inv = ROPE_THETA ** (-(2.0 * (j % (dim // 2))) / dim)
        ang = pos * inv[None, :]
        sign = np.where(j < dim // 2, -1.0, 1.0)
        cos = np.where(active[None, :], np.cos(ang), 1.0)
        sin = np.where(active[None, :], np.sin(ang) * sign[None, :], 0.0)
        return jnp.asarray(cos, F32), jnp.asarray(sin, F32)

    cr, sr = table(RET_DK, np.ones(LANES, bool), 0)
    cm, sm = table(MLA_ROPE, (lane >= MLA_NOPE) & (lane < MLA_NOPE + MLA_ROPE), MLA_NOPE)
    return cr, sr, cm, sm


def _layer_weights(l, w_in, mla_w_uq, mla_w_ukv, w_out):
    w = w_in[l]
    sizes = (256, 256, 256, 256, SWA_W, SWA_KV_W, SWA_KV_W, MLA_Q_RANK, MLA_KV_RANK, MLA_ROPE)
    offs = np.concatenate([[0], np.cumsum(sizes)])
    seg = [w[:, offs[i]:offs[i + 1]] for i in range(len(sizes))]
    sq = jnp.concatenate([seg[4][:, h * HEAD_DIM:(h + 1) * HEAD_DIM] for h in _SWA_HEAD_ORDER], 1)
    kr = jnp.pad(seg[9], ((0, 0), (MLA_NOPE, LANES - MLA_NOPE - MLA_ROPE)))
    w1 = jnp.concatenate(seg[:4] + [sq] + seg[5:9] + [kr], 1).astype(BF16)

    def pad_heads(cols):
        return jnp.concatenate([jnp.pad(c, ((0, 0), (0, LANES - c.shape[1]))) for c in cols], 1)

    dq = MLA_NOPE + MLA_ROPE
    wuq = pad_heads([mla_w_uq[l][:, h * dq:(h + 1) * dq] for h in range(MLA_HEADS)]).astype(BF16)
    dkv = MLA_NOPE + MLA_V
    wuk = pad_heads([mla_w_ukv[l][:, h * dkv:h * dkv + MLA_NOPE] for h in range(MLA_HEADS)]).astype(BF16)
    wuv = pad_heads([mla_w_ukv[l][:, h * dkv + MLA_NOPE:(h + 1) * dkv] for h in range(MLA_HEADS)]).astype(BF16)

    wo = w_out[l]
    wr = wo[:RET_W].astype(BF16)
    ws = jnp.concatenate([wo[RET_W + h * HEAD_DIM:RET_W + (h + 1) * HEAD_DIM] for h in _SWA_HEAD_ORDER],
                         0).astype(BF16)
    base = RET_W + SWA_W
    wm = jnp.concatenate([jnp.pad(wo[base + h * MLA_V:base + (h + 1) * MLA_V], ((0, LANES - MLA_V), (0, 0)))
                          for h in range(MLA_HEADS)], 0).astype(BF16)
    return w1, wuq, wuk, wuv, wr, ws, wm


def kernel(x, w_in, ret_gn_w, swa_sinks, mla_q_norm_w, mla_kv_norm_w, mla_w_uq, mla_w_ukv, w_out,
           ln1_g, ln1_b, router_w, router_bias, exp_w_gate_up, exp_w_down, ln2_g, ln2_b):
    batch, seq, _ = x.shape
    depth = w_in.shape[0]
    alpha = (2 * depth) ** 0.25
    cr, sr, cm, sm = _rope_tables(seq)

    rw = jnp.pad(router_w, ((0, 0), (0, LANES - N_EXPERTS)))
    rwh = rw.astype(BF16)
    rwl = (rw - rwh.astype(F32)).astype(BF16)
    rb = jnp.pad(router_bias.astype(F32), (0, LANES - N_EXPERTS)).reshape(1, LANES)

    t = x.reshape(batch * seq, D_MODEL)
    for l in range(depth):
        w1, wuq, wuk, wuv, wr, ws, wm = _layer_weights(l, w_in, mla_w_uq, mla_w_ukv, w_out)
        rq, rk, rv, rg, sq, sk, sv, mq, mk, mv = _inproj(
            t, w1, cr, sr, cm, sm, mla_q_norm_w[l].reshape(1, -1), mla_kv_norm_w[l].reshape(1, -1),
            wuq, wuk, wuv, seq=seq)
        ret_o = _retention(rq, rk, rv, rg, ret_gn_w[l].reshape(1, -1), batch=batch, seq=seq)
        sink_col = jnp.repeat(swa_sinks[l].astype(F32), SWA_WINDOW).reshape(-1, 1)
        swa_o = _swa(sq, sk, sv, sink_col, seq=seq)
        mla_o = _mla(mq, mk, mv, batch=batch, seq=seq)
        x1, x1b, gates = _outproj(ret_o, swa_o, mla_o, t, wr, ws, wm,
                                  ln1_g[l].reshape(1, -1), ln1_b[l].reshape(1, -1), rwh, rwl, rb, alpha=alpha)
        t = _moe(x1b, x1, gates, exp_w_gate_up[l].astype(BF16), exp_w_down[l].astype(BF16),
                 ln2_g[l].reshape(1, -1), ln2_b[l].reshape(1, -1), alpha=alpha)
    return t.reshape(batch, seq, D_MODEL)
```

```python
import functools
import math

import numpy as np
import jax
import jax.numpy as jnp
from jax import lax
from jax.experimental import pallas as pl
from jax.experimental.pallas import tpu as pltpu

D_MODEL = 1024
HEAD_DIM = 64
ROPE_THETA = 10000.0

RET_HEADS = 4
RET_DK = 64
RET_DV = 64
RET_W = RET_HEADS * RET_DV

SWA_Q_HEADS = 6
SWA_KV_HEADS = 2
SWA_WINDOW = 128
SWA_W = SWA_Q_HEADS * HEAD_DIM
SWA_KV_W = SWA_KV_HEADS * HEAD_DIM

MLA_HEADS = 6
MLA_Q_RANK = 384
MLA_KV_RANK = 256
MLA_NOPE = 64
MLA_ROPE = 32
MLA_V = 64

N_EXPERTS = 16
N_GROUPS = 4
EXPERTS_PER_GROUP = N_EXPERTS // N_GROUPS
D_EXPERT = 256

LN_EPS = 1e-5
RMS_EPS = 1e-6
NEG_INF = -1e30

LANES = 128
MLA_PAD_W = MLA_HEADS * LANES
MLA_TILE = 512
VMEM_LIMIT = 56 * 1024 * 1024

F32 = jnp.float32
BF16 = jnp.bfloat16

_C_RQ, _C_RK, _C_RV, _C_RG = 0, 256, 512, 768
_C_SQ, _C_KR, _C_SK, _C_SV = 1024, 1408, 1536, 1664
_C_CKV, _C_CQ, _C_END = 1792, 2048, 2432

_SWA_HEAD_ORDER = (0, 3, 1, 4, 2, 5)


def _params(*sem, flags=None):
    return pltpu.CompilerParams(dimension_semantics=sem, vmem_limit_bytes=VMEM_LIMIT, flags=flags)


def _dot(a, b):
    return jnp.dot(a, b, preferred_element_type=F32)


def _dot_nt(a, b):
    return lax.dot_general(a, b, (((1,), (1,)), ((), ())), preferred_element_type=F32)


def _dot_tn(a, b):
    return lax.dot_general(a, b, (((0,), (0,)), ((), ())), preferred_element_type=F32)


def _split_dot(x, w_bf16):
    hi = x.astype(BF16)
    lo = (x - hi.astype(F32)).astype(BF16)
    return _dot(hi, w_bf16) + _dot(lo, w_bf16)


def _rope(h, cos, sin_signed, half):
    fwd = pltpu.roll(h, LANES - half, 1)
    bwd = pltpu.roll(h, half, 1)
    lane = lax.broadcasted_iota(jnp.int32, h.shape, 1)
    rot = jnp.where((lane % (2 * half)) < half, fwd, bwd)
    return h * cos + rot * sin_signed


def _layer_norm(y, g, b):
    mu = jnp.mean(y, -1, keepdims=True)
    d = y - mu
    var = jnp.mean(d * d, -1, keepdims=True)
    return d * lax.rsqrt(var + LN_EPS) * g + b


def _rms_norm(x, g):
    return x * lax.rsqrt(jnp.mean(x * x, -1, keepdims=True) + RMS_EPS) * g


def _inproj_kernel(x_ref, w1_ref, cr_ref, sr_ref, cm_ref, sm_ref, qnw_ref, kvnw_ref,
                   wuq_ref, wuk_ref, wuvt_ref,
                   rq_ref, rk_ref, rv_ref, rg_ref, sq_ref, sk_ref, sv_ref, mq_ref, mk_ref, mvt_ref,
                   *, mla_q_scale):
    xb = x_ref[...].astype(BF16)

    def proj(a, b):
        return _dot(xb, w1_ref[:, a:b])

    cr, sr = cr_ref[...], sr_ref[...]
    rq, rk = proj(_C_RQ, _C_RK), proj(_C_RK, _C_RV)
    for g in range(RET_W // LANES):
        lanes = slice(g * LANES, (g + 1) * LANES)
        rq_ref[:, lanes] = _rope(rq[:, lanes], cr, sr, RET_DK // 2).astype(BF16)
        rk_ref[:, lanes] = (_rope(rk[:, lanes], cr, sr, RET_DK // 2) * (RET_DK ** -0.5)).astype(BF16)
    rv_ref[...] = proj(_C_RV, _C_RG).astype(BF16)
    rg_ref[...] = proj(_C_RG, _C_SQ).astype(BF16)

    sq_kr = proj(_C_SQ, _C_SK)
    sq_ref[...] = (sq_kr[:, :SWA_W] * (HEAD_DIM ** -0.5)).astype(BF16)
    sk_sv = proj(_C_SK, _C_CKV)
    sk_ref[...] = sk_sv[:, :SWA_KV_W].astype(BF16)
    sv_ref[...] = sk_sv[:, SWA_KV_W:].astype(BF16)

    cm, sm = cm_ref[...], sm_ref[...]
    cq = _rms_norm(proj(_C_CQ, _C_END), qnw_ref[...]).astype(BF16)
    q = _dot(cq, wuq_ref[...])
    ckv = _rms_norm(proj(_C_CKV, _C_CQ), kvnw_ref[...]).astype(BF16)
    kn = _dot(ckv, wuk_ref[...])
    kpe = _rope(sq_kr[:, SWA_W:], cm, sm, MLA_ROPE // 2)
    for h in range(MLA_HEADS):
        lo, hi = h * LANES, (h + 1) * LANES
        mq_ref[:, lo:hi] = (_rope(q[:, lo:hi], cm, sm, MLA_ROPE // 2) * mla_q_scale).astype(BF16)
        mk_ref[:, lo:hi] = (kn[:, lo:hi] + kpe).astype(BF16)
    vt = _dot_nt(wuvt_ref[...], ckv)
    feat = lax.broadcasted_iota(jnp.int32, vt.shape, 0) % LANES
    mvt_ref[0] = (vt + (feat == MLA_V).astype(F32)).astype(BF16)


def _inproj(x, w1, cr, sr, cm, sm, qnw, kvnw, wuq, wuk, wuvt, *, seq, tm):
    T = x.shape[0]
    nt = T // tm
    npos = seq // tm
    row = lambda w: pl.BlockSpec((tm, w), lambda i: (i, 0))
    pos = lambda: pl.BlockSpec((tm, LANES), lambda i: (i % npos, 0))
    full = lambda a: pl.BlockSpec(a.shape, lambda i: (0, 0))
    widths = (RET_W, RET_W, RET_W, RET_W, SWA_W, SWA_KV_W, SWA_KV_W, MLA_PAD_W, MLA_PAD_W)
    mla_q_scale = (MLA_NOPE + MLA_ROPE) ** -0.5 * math.log2(math.e)
    return pl.pallas_call(
        functools.partial(_inproj_kernel, mla_q_scale=mla_q_scale),
        grid=(nt,),
        in_specs=[row(D_MODEL), full(w1), pos(), pos(), pos(), pos(), full(qnw), full(kvnw),
                  full(wuq), full(wuk), full(wuvt)],
        out_specs=[row(w) for w in widths] + [pl.BlockSpec((1, MLA_PAD_W, tm), lambda i: (i, 0, 0))],
        out_shape=[jax.ShapeDtypeStruct((T, w), BF16) for w in widths]
        + [jax.ShapeDtypeStruct((nt, MLA_PAD_W, tm), BF16)],
        compiler_params=_params("parallel"),
        name="inproj",
    )(x, w1, cr, sr, cm, sm, qnw, kvnw, wuq, wuk, wuvt)


def _retention_kernel(q_ref, k_ref, v_ref, g_ref, gnw_ref, dec_ref, xi_ref, zeta_ref, dm_ref, bm_ref,
                      avg_ref, o_ref, state_ref, *, chunk, n_chunks):
    @pl.when(pl.program_id(1) == 0)
    def _():
        state_ref[...] = jnp.zeros_like(state_ref)

    lane_head = lax.broadcasted_iota(jnp.int32, (chunk, RET_W), 1) // RET_DV
    avg = avg_ref[...]
    for c in range(n_chunks):
        rows = slice(c * chunk, (c + 1) * chunk)
        q, k, v = q_ref[rows, :], k_ref[rows, :], v_ref[rows, :]
        zero = jnp.zeros_like(q)
        q_heads = jnp.concatenate([jnp.where(lane_head == h, q, zero) for h in range(RET_HEADS)], 0)
        scores = (_dot_nt(q_heads, k) * dec_ref[...]).astype(BF16)
        inner_heads = _dot(scores, v)
        o = _dot((q.astype(F32) * xi_ref[...]).astype(BF16), state_ref[...].astype(BF16))
        for h in range(RET_HEADS):
            o = o + jnp.where(lane_head == h, inner_heads[h * chunk:(h + 1) * chunk], 0.0)
        upd = _dot_tn((k.astype(F32) * zeta_ref[...]).astype(BF16), v)
        state_ref[...] = dm_ref[...] * state_ref[...] + bm_ref[...] * upd

        mu = _split_dot(o, avg)
        d = o - mu
        var = _split_dot(d * d, avg)
        on = d * lax.rsqrt(var + LN_EPS) * gnw_ref[...]
        gate = g_ref[rows, :].astype(F32)
        o_ref[rows, :] = (on * (gate * jax.nn.sigmoid(gate))).astype(BF16)


def _retention_tables(chunk):
    H = RET_HEADS
    gamma = 1.0 - 2.0 ** (-5.0 - np.arange(H, dtype=np.float64))
    log_g = np.log(gamma)
    idx = np.arange(chunk, dtype=np.float64)
    rel = idx[:, None] - idx[None, :]
    dec = np.where(rel[None] >= 0, np.exp(np.maximum(rel, 0.0)[None] * log_g[:, None, None]), 0.0)
    dec = dec.reshape(H * chunk, chunk)
    lane_head = np.arange(RET_W) // RET_DV
    xi = np.exp((idx[:, None] + 1.0) * log_g[lane_head][None, :])
    zeta = np.exp((chunk - 1.0 - idx[:, None]) * log_g[lane_head][None, :])
    same = lane_head[:, None] == lane_head[None, :]
    dm = np.where(same, np.exp(chunk * log_g)[lane_head][:, None], 0.0)
    bm = same.astype(np.float64)
    avg = bm / RET_DV
    f = lambda a: jnp.asarray(a, F32)
    return f(dec), f(xi), f(zeta), f(dm), f(bm), jnp.asarray(avg, BF16)


def _retention(rq, rk, rv, rg, gnw, *, batch, seq, blk=512, chunk=128):
    T = rq.shape[0]
    nb = seq // blk
    tables = _retention_tables(chunk)
    row = pl.BlockSpec((blk, RET_W), lambda b, i: (b * nb + i, 0))
    full = lambda a: pl.BlockSpec(a.shape, lambda b, i: (0, 0))
    return pl.pallas_call(
        functools.partial(_retention_kernel, chunk=chunk, n_chunks=blk // chunk),
        grid=(batch, nb),
        in_specs=[row, row, row, row, full(gnw)] + [full(t) for t in tables],
        out_specs=row,
        out_shape=jax.ShapeDtypeStruct((T, RET_W), BF16),
        scratch_shapes=[pltpu.VMEM((RET_W, RET_W), F32)],
        compiler_params=_params("parallel", "arbitrary"),
        name="retention",
    )(rq, rk, rv, rg, gnw, *tables)


def _swa_kernel(q_ref, kc_ref, kp_ref, vc_ref, vp_ref, sink_ref, o_ref, *, blk, seq):
    L = SWA_WINDOW
    n_pairs = SWA_Q_HEADS // 2
    lane_lo = lax.broadcasted_iota(jnp.int32, (L, LANES), 1) < HEAD_DIM
    row = lax.broadcasted_iota(jnp.int32, (SWA_Q_HEADS * L, 2 * L), 0) % L
    col = lax.broadcasted_iota(jnp.int32, (SWA_Q_HEADS * L, 2 * L), 1)
    band = (col > row) & (col <= row + L)
    sink = sink_ref[...]
    for sb in range(blk // L):
        r0 = sb * L
        pos0 = (pl.program_id(0) * blk + r0) % seq
        if sb == 0:
            k_prev, v_prev = kp_ref[...], vp_ref[...]
        else:
            k_prev, v_prev = kc_ref[r0 - L:r0, :], vc_ref[r0 - L:r0, :]
        k_win = jnp.concatenate([k_prev, kc_ref[r0:r0 + L, :]], 0)
        v_win = jnp.concatenate([v_prev, vc_ref[r0:r0 + L, :]], 0)
        pairs = [q_ref[r0:r0 + L, g * LANES:(g + 1) * LANES] for g in range(n_pairs)]
        zero = jnp.zeros_like(pairs[0])
        q_heads = jnp.concatenate([jnp.where(lane_lo, p, zero) for p in pairs]
                                  + [jnp.where(lane_lo, zero, p) for p in pairs], 0)
        s = _dot_nt(q_heads, k_win)
        first_key = jnp.where(pos0 > 0, 0, L)
        valid = band & (col >= first_key)
        s = jnp.where(valid, s, NEG_INF)
        m = jnp.maximum(jnp.max(s, -1, keepdims=True), sink)
        p = jnp.exp(s - m)
        denom = jnp.sum(p, -1, keepdims=True) + jnp.exp(sink - m)
        o = _dot((p / denom).astype(BF16), v_win)
        for g in range(n_pairs):
            o_lo = o[g * L:(g + 1) * L]
            o_hi = o[(g + n_pairs) * L:(g + n_pairs + 1) * L]
            o_ref[r0:r0 + L, g * LANES:(g + 1) * LANES] = jnp.where(lane_lo, o_lo, o_hi).astype(BF16)


def _swa(sq, sk, sv, sink_col, *, seq, blk=512):
    T = sq.shape[0]
    L = SWA_WINDOW
    per = blk // L
    cur = lambda w: pl.BlockSpec((blk, w), lambda i: (i, 0))
    prev = pl.BlockSpec((L, SWA_KV_W), lambda i: (jnp.maximum(i * per - 1, 0), 0))
    return pl.pallas_call(
        functools.partial(_swa_kernel, blk=blk, seq=seq),
        grid=(T // blk,),
        in_specs=[cur(SWA_W), cur(SWA_KV_W), prev, cur(SWA_KV_W), prev,
                  pl.BlockSpec(sink_col.shape, lambda i: (0, 0))],
        out_specs=cur(SWA_W),
        out_shape=jax.ShapeDtypeStruct((T, SWA_W), BF16),
        compiler_params=_params("parallel"),
        name="swa",
    )(sq, sk, sk, sv, sv, sink_col)


def _mla_kernel(q_ref, k_ref, vt_ref, o_ref, st0_ref, st1_ref, m_ref, acc_ref, *, tq, tk, heads):
    i = pl.program_id(2)
    key = lax.broadcasted_iota(jnp.int32, (tk, tq), 0)
    qry = lax.broadcasted_iota(jnp.int32, (tk, tq), 1)
    head_lanes = [slice(h * LANES, (h + 1) * LANES) for h in range(heads)]
    st_refs = (st0_ref, st1_ref)

    def scores(j, lanes):
        start = pl.multiple_of(j * tk, tk)
        return _dot_nt(k_ref[pl.ds(start, tk), lanes], q_ref[:, lanes])

    def step(j, cur, key_offset=None, prefetch=True):
        for h, lanes in enumerate(head_lanes):
            if prefetch:
                st_refs[1 - cur][h] = scores(j + 1, lanes)
            st = st_refs[cur][h]
            if key_offset is not None:
                st = jnp.where(key + key_offset <= qry, st, NEG_INF)
            m = m_ref[h]
            m_new = jnp.maximum(m, jnp.max(st, 0, keepdims=True))
            pt = jnp.exp2(st - m_new).astype(BF16)
            acc_ref[h] = jnp.exp2(m - m_new) * acc_ref[h] + _dot(vt_ref[j, lanes, :], pt)
            m_ref[h] = m_new

    m_ref[...] = jnp.full(m_ref.shape, NEG_INF, F32)
    acc_ref[...] = jnp.zeros(acc_ref.shape, F32)
    for h, lanes in enumerate(head_lanes):
        st0_ref[h] = scores(0, lanes)

    def pair(p, _):
        step(2 * p, 0)
        step(2 * p + 1, 1)
        return 0

    lax.fori_loop(0, i, pair, 0)
    step(2 * i, 0, key_offset=0)
    step(2 * i + 1, 1, key_offset=tk, prefetch=False)
    for h in range(heads):
        acc = acc_ref[h]
        o_ref[:, h * LANES:(h + 1) * LANES] = (acc / acc[MLA_V:MLA_V + 1, :]).T.astype(BF16)


def _mla(mq, mk, mvt, *, batch, seq, tk, heads=3):
    T = mq.shape[0]
    tq = 2 * tk
    nq = seq // tq
    qo = pl.BlockSpec((tq, heads * LANES), lambda b, h, i: (b * nq + i, h))
    k = pl.BlockSpec((seq, heads * LANES), lambda b, h, i: (b, h))
    vt = pl.BlockSpec((seq // tk, heads * LANES, tk), lambda b, h, i: (b, h, 0))
    return pl.pallas_call(
        functools.partial(_mla_kernel, tq=tq, tk=tk, heads=heads),
        grid=(batch, MLA_HEADS // heads, nq),
        in_specs=[qo, k, vt],
        out_specs=qo,
        out_shape=jax.ShapeDtypeStruct((T, MLA_PAD_W), BF16),
        scratch_shapes=[pltpu.VMEM((heads, tk, tq), F32), pltpu.VMEM((heads, tk, tq), F32),
                        pltpu.VMEM((heads, 1, tq), F32), pltpu.VMEM((heads, LANES, tq), F32)],
        compiler_params=_params("parallel", "parallel", "arbitrary"),
        name="mla",
    )(mq, mk, mvt)


def _group_partner(x, d, width, lane):
    return jnp.where((lane % width) + d < width,
                     pltpu.roll(x, LANES - d, 1), pltpu.roll(x, width - d, 1))


def _router_gates(logits, bias):
    lane = lax.broadcasted_iota(jnp.int32, logits.shape, 1)
    is_expert = lane < N_EXPERTS
    scores = jax.nn.sigmoid(logits)
    biased = jnp.where(is_expert, scores + bias, NEG_INF)
    G = EXPERTS_PER_GROUP
    member = lane % G
    rank = jnp.zeros(logits.shape, jnp.int32)
    for d in range(1, G):
        other = _group_partner(biased, d, G, lane)
        other_first = (member + d) % G < member
        rank = rank + ((other > biased) | ((other == biased) & other_first)).astype(jnp.int32)
    top2 = rank < 2
    grp_score = jnp.where(top2, biased, 0.0)
    part = grp_score
    for d in range(1, G):
        grp_score = grp_score + _group_partner(part, d, G, lane)
    beaten = jnp.zeros(logits.shape, jnp.bool_)
    grp = lane // G
    for dg in range(1, N_GROUPS):
        other = _group_partner(grp_score, dg * G, N_EXPERTS, lane)
        other_first = (grp + dg) % N_GROUPS < grp
        beaten = beaten | (other > grp_score) | ((other == grp_score) & other_first)
    sel = is_expert & top2 & jnp.logical_not(beaten)
    picked = jnp.where(sel, scores, 0.0)
    return picked / jnp.sum(picked, -1, keepdims=True)


def _outproj_kernel(ret_ref, swa_ref, mla_ref, x_ref, wr_ref, ws_ref, wm_ref, g_ref, b_ref,
                    rw_ref, rb_ref, x1_ref, x1b_ref, gates_ref, *, alpha):
    mix = _dot(ret_ref[...], wr_ref[...]) + _dot(swa_ref[...], ws_ref[...]) + _dot(mla_ref[...], wm_ref[...])
    x1 = _layer_norm(alpha * x_ref[...] + mix, g_ref[...], b_ref[...])
    x1_ref[...] = x1
    hi = x1.astype(BF16)
    x1b_ref[...] = hi
    lo = (x1 - hi.astype(F32)).astype(BF16)
    hi_part = _dot(hi, rw_ref[...])
    logits = hi_part[:, :LANES] + hi_part[:, LANES:] + _dot(lo, rw_ref[:, :LANES])
    gates_ref[...] = _router_gates(logits, rb_ref[...])


def _outproj(ret_o, swa_o, mla_o, x, wr, ws, wm, g, b, rw, rb, *, alpha, tm=512):
    T = x.shape[0]
    row = lambda w: pl.BlockSpec((tm, w), lambda i: (i, 0))
    full = lambda a: pl.BlockSpec(a.shape, lambda i: (0, 0))
    return pl.pallas_call(
        functools.partial(_outproj_kernel, alpha=alpha),
        grid=(T // tm,),
        in_specs=[row(RET_W), row(SWA_W), row(MLA_PAD_W), row(D_MODEL), full(wr), full(ws), full(wm),
                  full(g), full(b), full(rw), full(rb)],
        out_specs=[row(D_MODEL), row(D_MODEL), row(LANES)],
        out_shape=[jax.ShapeDtypeStruct((T, D_MODEL), F32), jax.ShapeDtypeStruct((T, D_MODEL), BF16),
                   jax.ShapeDtypeStruct((T, LANES), F32)],
        compiler_params=_params("parallel"),
        name="outproj",
    )(ret_o, swa_o, mla_o, x, wr, ws, wm, g, b, rw, rb)


def _moe_kernel(xb_ref, x_ref, gates_ref, wgu_ref, wd_ref, g_ref, b_ref, o_ref, acc_ref, *, alpha):
    e = pl.program_id(1)

    @pl.when(e == 0)
    def _():
        acc_ref[...] = jnp.zeros_like(acc_ref)

    gates = gates_ref[...]
    lane = lax.broadcasted_iota(jnp.int32, gates.shape, 1)
    gate = jnp.sum(jnp.where(lane == e, gates, 0.0), -1, keepdims=True)
    h = _dot(xb_ref[...], wgu_ref[0])
    up, lin = h[:, :D_EXPERT], h[:, D_EXPERT:]
    a = up * jax.nn.sigmoid(up) * lin * gate
    acc_ref[...] += _dot(a.astype(BF16), wd_ref[0])

    @pl.when(e == N_EXPERTS - 1)
    def _():
        o_ref[...] = _layer_norm(alpha * x_ref[...] + acc_ref[...], g_ref[...], b_ref[...])


def _moe(x1b, x1, gates, wgu, wd, g, b, *, alpha, tm=1024):
    T = x1.shape[0]
    row = lambda w: pl.BlockSpec((tm, w), lambda i, e: (i, 0))
    full = lambda a: pl.BlockSpec(a.shape, lambda i, e: (0, 0))
    return pl.pallas_call(
        functools.partial(_moe_kernel, alpha=alpha),
        grid=(T // tm, N_EXPERTS),
        in_specs=[row(D_MODEL), row(D_MODEL), row(LANES),
                  pl.BlockSpec((1, D_MODEL, 2 * D_EXPERT), lambda i, e: (e, 0, 0)),
                  pl.BlockSpec((1, D_EXPERT, D_MODEL), lambda i, e: (e, 0, 0)),
                  full(g), full(b)],
        out_specs=row(D_MODEL),
        out_shape=jax.ShapeDtypeStruct((T, D_MODEL), F32),
        scratch_shapes=[pltpu.VMEM((tm, D_MODEL), F32)],
        compiler_params=_params("parallel", "arbitrary"),
        name="moe",
    )(x1b, x1, gates, wgu, wd, g, b)


def _rope_tables(seq):
    pos = np.arange(seq, dtype=np.float64)[:, None]
    lane = np.arange(LANES)

    def table(dim, active, offset):
        j = (lane - offset) % dim
        inv = ROPE_THETA ** (-(2.0 * (j % (dim // 2))) / dim)
        ang = pos * inv[None, :]
        sign = np.where(j < dim // 2, -1.0, 1.0)
        cos = np.where(active[None, :], np.cos(ang), 1.0)
        sin = np.where(active[None, :], np.sin(ang) * sign[None, :], 0.0)
        return jnp.asarray(cos, F32), jnp.asarray(sin, F32)

    cr, sr = table(RET_DK, np.ones(LANES, bool), 0)
    cm, sm = table(MLA_ROPE, (lane >= MLA_NOPE) & (lane < MLA_NOPE + MLA_ROPE), MLA_NOPE)
    return cr, sr, cm, sm


def _layer_weights(l, w_in, mla_w_uq, mla_w_ukv, w_out):
    w = w_in[l]
    sizes = (256, 256, 256, 256, SWA_W, SWA_KV_W, SWA_KV_W, MLA_Q_RANK, MLA_KV_RANK, MLA_ROPE)
    offs = np.concatenate([[0], np.cumsum(sizes)])
    seg = [w[:, offs[i]:offs[i + 1]] for i in range(len(sizes))]
    sq = jnp.concatenate([seg[4][:, h * HEAD_DIM:(h + 1) * HEAD_DIM] for h in _SWA_HEAD_ORDER], 1)
    kr = jnp.pad(seg[9], ((0, 0), (MLA_NOPE, LANES - MLA_NOPE - MLA_ROPE)))
    w1 = jnp.concatenate(seg[:4] + [sq, kr, seg[5], seg[6], seg[8], seg[7]], 1).astype(BF16)

    def pad_heads(cols):
        return jnp.concatenate([jnp.pad(c, ((0, 0), (0, LANES - c.shape[1]))) for c in cols], 1)

    dq = MLA_NOPE + MLA_ROPE
    wuq = pad_heads([mla_w_uq[l][:, h * dq:(h + 1) * dq] for h in range(MLA_HEADS)]).astype(BF16)
    dkv = MLA_NOPE + MLA_V
    wuk = pad_heads([mla_w_ukv[l][:, h * dkv:h * dkv + MLA_NOPE] for h in range(MLA_HEADS)]).astype(BF16)
    wuvt = pad_heads([mla_w_ukv[l][:, h * dkv + MLA_NOPE:(h + 1) * dkv] for h in range(MLA_HEADS)]).T.astype(BF16)

    wo = w_out[l]
    wr = wo[:RET_W].astype(BF16)
    ws = jnp.concatenate([wo[RET_W + h * HEAD_DIM:RET_W + (h + 1) * HEAD_DIM] for h in _SWA_HEAD_ORDER],
                         0).astype(BF16)
    base = RET_W + SWA_W
    wm = jnp.concatenate([jnp.pad(wo[base + h * MLA_V:base + (h + 1) * MLA_V], ((0, LANES - MLA_V), (0, 0)))
                          for h in range(MLA_HEADS)], 0).astype(BF16)
    return w1, wuq, wuk, wuvt, wr, ws, wm


def kernel(x, w_in, ret_gn_w, swa_sinks, mla_q_norm_w, mla_kv_norm_w, mla_w_uq, mla_w_ukv, w_out,
           ln1_g, ln1_b, router_w, router_bias, exp_w_gate_up, exp_w_down, ln2_g, ln2_b):
    batch, seq, _ = x.shape
    depth = w_in.shape[0]
    alpha = (2 * depth) ** 0.25
    cr, sr, cm, sm = _rope_tables(seq)

    rw = jnp.pad(router_w, ((0, 0), (0, LANES - N_EXPERTS)))
    rwh = rw.astype(BF16)
    rw = jnp.concatenate([rwh, (rw - rwh.astype(F32)).astype(BF16)], 1)
    rb = jnp.pad(router_bias.astype(F32), (0, LANES - N_EXPERTS)).reshape(1, LANES)

    t = x.reshape(batch * seq, D_MODEL)
    for l in range(depth):
        w1, wuq, wuk, wuvt, wr, ws, wm = _layer_weights(l, w_in, mla_w_uq, mla_w_ukv, w_out)
        rq, rk, rv, rg, sq, sk, sv, mq, mk, mvt = _inproj(
            t, w1, cr, sr, cm, sm, mla_q_norm_w[l].reshape(1, -1), mla_kv_norm_w[l].reshape(1, -1),
            wuq, wuk, wuvt, seq=seq, tm=MLA_TILE)
        ret_o = _retention(rq, rk, rv, rg, ret_gn_w[l].reshape(1, -1), batch=batch, seq=seq)
        sink_col = jnp.repeat(swa_sinks[l].astype(F32), SWA_WINDOW).reshape(-1, 1)
        swa_o = _swa(sq, sk, sv, sink_col, seq=seq)
        mla_o = _mla(mq, mk, mvt, batch=batch, seq=seq, tk=MLA_TILE)
        x1, x1b, gates = _outproj(ret_o, swa_o, mla_o, t, wr, ws, wm,
                                  ln1_g[l].reshape(1, -1), ln1_b[l].reshape(1, -1), rw, rb, alpha=alpha)
        t = _moe(x1b, x1, gates, exp_w_gate_up[l].astype(BF16), exp_w_down[l].astype(BF16),
                 ln2_g[l].reshape(1, -1), ln2_b[l].reshape(1, -1), alpha=alpha)
    return t.reshape(batch, seq, D_MODEL)
```

```python
import functools
import math

import numpy as np
import jax
import jax.numpy as jnp
from jax import lax
from jax.experimental import pallas as pl
from jax.experimental.pallas import tpu as pltpu

D_MODEL = 1024
HEAD_DIM = 64
ROPE_THETA = 10000.0

RET_HEADS = 4
RET_DK = 64
RET_DV = 64
RET_W = RET_HEADS * RET_DV

SWA_Q_HEADS = 6
SWA_KV_HEADS = 2
SWA_WINDOW = 128
SWA_W = SWA_Q_HEADS * HEAD_DIM
SWA_KV_W = SWA_KV_HEADS * HEAD_DIM

MLA_HEADS = 6
MLA_Q_RANK = 384
MLA_KV_RANK = 256
MLA_NOPE = 64
MLA_ROPE = 32
MLA_V = 64

N_EXPERTS = 16
N_GROUPS = 4
EXPERTS_PER_GROUP = N_EXPERTS // N_GROUPS
D_EXPERT = 256

LN_EPS = 1e-5
RMS_EPS = 1e-6
NEG_INF = -1e30

LANES = 128
MLA_PAD_W = MLA_HEADS * LANES
MLA_TILE = 512
VMEM_LIMIT = 56 * 1024 * 1024

F32 = jnp.float32
BF16 = jnp.bfloat16

_C_RQ, _C_RK, _C_RV, _C_RG = 0, 256, 512, 768
_C_SQ, _C_KR, _C_SK, _C_SV = 1024, 1408, 1536, 1664
_C_CKV, _C_CQ, _C_END = 1792, 2048, 2432

_SWA_HEAD_ORDER = (0, 3, 1, 4, 2, 5)


def _params(*sem, flags=None):
    return pltpu.CompilerParams(dimension_semantics=sem, vmem_limit_bytes=VMEM_LIMIT, flags=flags)


def _dot(a, b):
    return jnp.dot(a, b, preferred_element_type=F32)


def _dot_nt(a, b):
    return lax.dot_general(a, b, (((1,), (1,)), ((), ())), preferred_element_type=F32)


def _dot_tn(a, b):
    return lax.dot_general(a, b, (((0,), (0,)), ((), ())), preferred_element_type=F32)


def _split_dot(x, w_bf16):
    hi = x.astype(BF16)
    lo = (x - hi.astype(F32)).astype(BF16)
    return _dot(hi, w_bf16) + _dot(lo, w_bf16)


def _rope(h, cos, sin_signed, half):
    fwd = pltpu.roll(h, LANES - half, 1)
    bwd = pltpu.roll(h, half, 1)
    lane = lax.broadcasted_iota(jnp.int32, h.shape, 1)
    rot = jnp.where((lane % (2 * half)) < half, fwd, bwd)
    return h * cos + rot * sin_signed


def _layer_norm(y, g, b):
    mu = jnp.mean(y, -1, keepdims=True)
    d = y - mu
    var = jnp.mean(d * d, -1, keepdims=True)
    return d * lax.rsqrt(var + LN_EPS) * g + b


def _rms_norm(x, g):
    return x * lax.rsqrt(jnp.mean(x * x, -1, keepdims=True) + RMS_EPS) * g


def _inproj_kernel(x_ref, w1_ref, cr_ref, sr_ref, cm_ref, sm_ref, qnw_ref, kvnw_ref,
                   wuq_ref, wuk_ref, wuvt_ref,
                   rq_ref, rk_ref, rv_ref, rg_ref, sq_ref, sk_ref, sv_ref, mq_ref, mk_ref, mvt_ref,
                   *, mla_q_scale):
    xb = x_ref[...].astype(BF16)

    def proj(a, b):
        return _dot(xb, w1_ref[:, a:b])

    cr, sr = cr_ref[...], sr_ref[...]
    rq, rk = proj(_C_RQ, _C_RK), proj(_C_RK, _C_RV)
    for g in range(RET_W // LANES):
        lanes = slice(g * LANES, (g + 1) * LANES)
        rq_ref[:, lanes] = _rope(rq[:, lanes], cr, sr, RET_DK // 2).astype(BF16)
        rk_ref[:, lanes] = (_rope(rk[:, lanes], cr, sr, RET_DK // 2) * (RET_DK ** -0.5)).astype(BF16)
    rv_ref[...] = proj(_C_RV, _C_RG).astype(BF16)
    rg_ref[...] = proj(_C_RG, _C_SQ).astype(BF16)

    sq_kr = proj(_C_SQ, _C_SK)
    sq_ref[...] = (sq_kr[:, :SWA_W] * (HEAD_DIM ** -0.5)).astype(BF16)
    sk_sv = proj(_C_SK, _C_CKV)
    sk_ref[...] = sk_sv[:, :SWA_KV_W].astype(BF16)
    sv_ref[...] = sk_sv[:, SWA_KV_W:].astype(BF16)

    cm, sm = cm_ref[...], sm_ref[...]
    cq = _rms_norm(proj(_C_CQ, _C_END), qnw_ref[...]).astype(BF16)
    q = _dot(cq, wuq_ref[...])
    ckv = _rms_norm(proj(_C_CKV, _C_CQ), kvnw_ref[...]).astype(BF16)
    kn = _dot(ckv, wuk_ref[...])
    kpe = _rope(sq_kr[:, SWA_W:], cm, sm, MLA_ROPE // 2)
    for h in range(MLA_HEADS):
        lo, hi = h * LANES, (h + 1) * LANES
        mq_ref[:, lo:hi] = (_rope(q[:, lo:hi], cm, sm, MLA_ROPE // 2) * mla_q_scale).astype(BF16)
        mk_ref[:, lo:hi] = (kn[:, lo:hi] + kpe).astype(BF16)
    vt = _dot_nt(wuvt_ref[...], ckv)
    feat = lax.broadcasted_iota(jnp.int32, vt.shape, 0) % LANES
    mvt_ref[0] = (vt + (feat == MLA_V).astype(F32)).astype(BF16)


def _inproj(x, w1, cr, sr, cm, sm, qnw, kvnw, wuq, wuk, wuvt, *, seq, tm):
    T = x.shape[0]
    nt = T // tm
    npos = seq // tm
    row = lambda w: pl.BlockSpec((tm, w), lambda i: (i, 0))
    pos = lambda: pl.BlockSpec((tm, LANES), lambda i: (i % npos, 0))
    full = lambda a: pl.BlockSpec(a.shape, lambda i: (0, 0))
    widths = (RET_W, RET_W, RET_W, RET_W, SWA_W, SWA_KV_W, SWA_KV_W, MLA_PAD_W, MLA_PAD_W)
    mla_q_scale = (MLA_NOPE + MLA_ROPE) ** -0.5 * math.log2(math.e)
    return pl.pallas_call(
        functools.partial(_inproj_kernel, mla_q_scale=mla_q_scale),
        grid=(nt,),
        in_specs=[row(D_MODEL), full(w1), pos(), pos(), pos(), pos(), full(qnw), full(kvnw),
                  full(wuq), full(wuk), full(wuvt)],
        out_specs=[row(w) for w in widths] + [pl.BlockSpec((1, MLA_PAD_W, tm), lambda i: (i, 0, 0))],
        out_shape=[jax.ShapeDtypeStruct((T, w), BF16) for w in widths]
        + [jax.ShapeDtypeStruct((nt, MLA_PAD_W, tm), BF16)],
        compiler_params=_params("parallel"),
        name="inproj",
    )(x, w1, cr, sr, cm, sm, qnw, kvnw, wuq, wuk, wuvt)


def _retention_kernel(q_ref, k_ref, v_ref, g_ref, gnw_ref, dec_ref, xi_ref, zeta_ref, dm_ref, bm_ref,
                      avg_ref, o_ref, state_ref, *, chunk, n_chunks):
    @pl.when(pl.program_id(1) == 0)
    def _():
        state_ref[...] = jnp.zeros_like(state_ref)

    lane_head = lax.broadcasted_iota(jnp.int32, (chunk, RET_W), 1) // RET_DV
    avg = avg_ref[...]
    for c in range(n_chunks):
        rows = slice(c * chunk, (c + 1) * chunk)
        q, k, v = q_ref[rows, :], k_ref[rows, :], v_ref[rows, :]
        zero = jnp.zeros_like(q)
        q_heads = jnp.concatenate([jnp.where(lane_head == h, q, zero) for h in range(RET_HEADS)], 0)
        scores = (_dot_nt(q_heads, k) * dec_ref[...]).astype(BF16)
        inner_heads = _dot(scores, v)
        o = _dot((q.astype(F32) * xi_ref[...]).astype(BF16), state_ref[...].astype(BF16))
        for h in range(RET_HEADS):
            o = o + jnp.where(lane_head == h, inner_heads[h * chunk:(h + 1) * chunk], 0.0)
        upd = _dot_tn((k.astype(F32) * zeta_ref[...]).astype(BF16), v)
        state_ref[...] = dm_ref[...] * state_ref[...] + bm_ref[...] * upd

        mu = _split_dot(o, avg)
        d = o - mu
        var = _split_dot(d * d, avg)
        on = d * lax.rsqrt(var + LN_EPS) * gnw_ref[...]
        gate = g_ref[rows, :].astype(F32)
        o_ref[rows, :] = (on * (gate * jax.nn.sigmoid(gate))).astype(BF16)


def _retention_tables(chunk):
    H = RET_HEADS
    gamma = 1.0 - 2.0 ** (-5.0 - np.arange(H, dtype=np.float64))
    log_g = np.log(gamma)
    idx = np.arange(chunk, dtype=np.float64)
    rel = idx[:, None] - idx[None, :]
    dec = np.where(rel[None] >= 0, np.exp(np.maximum(rel, 0.0)[None] * log_g[:, None, None]), 0.0)
    dec = dec.reshape(H * chunk, chunk)
    lane_head = np.arange(RET_W) // RET_DV
    xi = np.exp((idx[:, None] + 1.0) * log_g[lane_head][None, :])
    zeta = np.exp((chunk - 1.0 - idx[:, None]) * log_g[lane_head][None, :])
    same = lane_head[:, None] == lane_head[None, :]
    dm = np.where(same, np.exp(chunk * log_g)[lane_head][:, None], 0.0)
    bm = same.astype(np.float64)
    avg = bm / RET_DV
    f = lambda a: jnp.asarray(a, F32)
    return f(dec), f(xi), f(zeta), f(dm), f(bm), jnp.asarray(avg, BF16)


def _retention(rq, rk, rv, rg, gnw, *, batch, seq, blk=512, chunk=128):
    T = rq.shape[0]
    nb = seq // blk
    tables = _retention_tables(chunk)
    row = pl.BlockSpec((blk, RET_W), lambda b, i: (b * nb + i, 0))
    full = lambda a: pl.BlockSpec(a.shape, lambda b, i: (0, 0))
    return pl.pallas_call(
        functools.partial(_retention_kernel, chunk=chunk, n_chunks=blk // chunk),
        grid=(batch, nb),
        in_specs=[row, row, row, row, full(gnw)] + [full(t) for t in tables],
        out_specs=row,
        out_shape=jax.ShapeDtypeStruct((T, RET_W), BF16),
        scratch_shapes=[pltpu.VMEM((RET_W, RET_W), F32)],
        compiler_params=_params("parallel", "arbitrary"),
        name="retention",
    )(rq, rk, rv, rg, gnw, *tables)


def _swa_kernel(q_ref, kc_ref, kp_ref, vc_ref, vp_ref, sink_ref, o_ref, *, blk, seq):
    L = SWA_WINDOW
    n_pairs = SWA_Q_HEADS // 2
    lane_lo = lax.broadcasted_iota(jnp.int32, (L, LANES), 1) < HEAD_DIM
    row = lax.broadcasted_iota(jnp.int32, (SWA_Q_HEADS * L, 2 * L), 0) % L
    col = lax.broadcasted_iota(jnp.int32, (SWA_Q_HEADS * L, 2 * L), 1)
    band = (col > row) & (col <= row + L)
    sink = sink_ref[...]
    for sb in range(blk // L):
        r0 = sb * L
        pos0 = (pl.program_id(0) * blk + r0) % seq
        if sb == 0:
            k_prev, v_prev = kp_ref[...], vp_ref[...]
        else:
            k_prev, v_prev = kc_ref[r0 - L:r0, :], vc_ref[r0 - L:r0, :]
        k_win = jnp.concatenate([k_prev, kc_ref[r0:r0 + L, :]], 0)
        v_win = jnp.concatenate([v_prev, vc_ref[r0:r0 + L, :]], 0)
        pairs = [q_ref[r0:r0 + L, g * LANES:(g + 1) * LANES] for g in range(n_pairs)]
        zero = jnp.zeros_like(pairs[0])
        q_heads = jnp.concatenate([jnp.where(lane_lo, p, zero) for p in pairs]
                                  + [jnp.where(lane_lo, zero, p) for p in pairs], 0)
        s = _dot_nt(q_heads, k_win)
        first_key = jnp.where(pos0 > 0, 0, L)
        valid = band & (col >= first_key)
        s = jnp.where(valid, s, NEG_INF)
        m = jnp.maximum(jnp.max(s, -1, keepdims=True), sink)
        p = jnp.exp(s - m)
        denom = jnp.sum(p, -1, keepdims=True) + jnp.exp(sink - m)
        o = _dot((p / denom).astype(BF16), v_win)
        for g in range(n_pairs):
            o_lo = o[g * L:(g + 1) * L]
            o_hi = o[(g + n_pairs) * L:(g + n_pairs + 1) * L]
            o_ref[r0:r0 + L, g * LANES:(g + 1) * LANES] = jnp.where(lane_lo, o_lo, o_hi).astype(BF16)


def _swa(sq, sk, sv, sink_col, *, seq, blk=512):
    T = sq.shape[0]
    L = SWA_WINDOW
    per = blk // L
    cur = lambda w: pl.BlockSpec((blk, w), lambda i: (i, 0))
    prev = pl.BlockSpec((L, SWA_KV_W), lambda i: (jnp.maximum(i * per - 1, 0), 0))
    return pl.pallas_call(
        functools.partial(_swa_kernel, blk=blk, seq=seq),
        grid=(T // blk,),
        in_specs=[cur(SWA_W), cur(SWA_KV_W), prev, cur(SWA_KV_W), prev,
                  pl.BlockSpec(sink_col.shape, lambda i: (0, 0))],
        out_specs=cur(SWA_W),
        out_shape=jax.ShapeDtypeStruct((T, SWA_W), BF16),
        compiler_params=_params("parallel"),
        name="swa",
    )(sq, sk, sk, sv, sv, sink_col)


def _mla_kernel(q_ref, k_ref, vt_ref, o_ref, st0_ref, st1_ref, m_ref, acc_ref, *, tq, tk, heads):
    i = pl.program_id(2)
    key = lax.broadcasted_iota(jnp.int32, (tk, tq), 0)
    qry = lax.broadcasted_iota(jnp.int32, (tk, tq), 1)
    head_lanes = [slice(h * LANES, (h + 1) * LANES) for h in range(heads)]
    st_refs = (st0_ref, st1_ref)

    def scores(j, lanes):
        start = pl.multiple_of(j * tk, tk)
        return _dot_nt(k_ref[pl.ds(start, tk), lanes], q_ref[:, lanes])

    def step(j, cur, key_offset=None, prefetch=True):
        for h, lanes in enumerate(head_lanes):
            if prefetch:
                st_refs[1 - cur][h] = scores(j + 1, lanes)
            st = st_refs[cur][h]
            if key_offset is not None:
                st = jnp.where(key + key_offset <= qry, st, NEG_INF)
            m = m_ref[h]
            m_new = jnp.maximum(m, jnp.max(st, 0, keepdims=True))
            pt = jnp.exp2(st - m_new).astype(BF16)
            acc_ref[h] = jnp.exp2(m - m_new) * acc_ref[h] + _dot(vt_ref[j, lanes, :], pt)
            m_ref[h] = m_new

    m_ref[...] = jnp.full(m_ref.shape, NEG_INF, F32)
    acc_ref[...] = jnp.zeros(acc_ref.shape, F32)
    for h, lanes in enumerate(head_lanes):
        st0_ref[h] = scores(0, lanes)

    def pair(p, _):
        step(2 * p, 0)
        step(2 * p + 1, 1)
        return 0

    lax.fori_loop(0, i, pair, 0)
    step(2 * i, 0, key_offset=0)
    step(2 * i + 1, 1, key_offset=tk, prefetch=False)
    for h in range(heads):
        acc = acc_ref[h]
        o_ref[:, h * LANES:(h + 1) * LANES] = (acc / acc[MLA_V:MLA_V + 1, :]).T.astype(BF16)


def _mla(mq, mk, mvt, *, batch, seq, tk, heads=3):
    T = mq.shape[0]
    tq = 2 * tk
    nq = seq // tq
    qo = pl.BlockSpec((tq, heads * LANES), lambda b, h, i: (b * nq + i, h))
    k = pl.BlockSpec((seq, heads * LANES), lambda b, h, i: (b, h))
    vt = pl.BlockSpec((seq // tk, heads * LANES, tk), lambda b, h, i: (b, h, 0))
    return pl.pallas_call(
        functools.partial(_mla_kernel, tq=tq, tk=tk, heads=heads),
        grid=(batch, MLA_HEADS // heads, nq),
        in_specs=[qo, k, vt],
        out_specs=qo,
        out_shape=jax.ShapeDtypeStruct((T, MLA_PAD_W), BF16),
        scratch_shapes=[pltpu.VMEM((heads, tk, tq), F32), pltpu.VMEM((heads, tk, tq), F32),
                        pltpu.VMEM((heads, 1, tq), F32), pltpu.VMEM((heads, LANES, tq), F32)],
        compiler_params=_params("parallel", "parallel", "arbitrary"),
        name="mla",
    )(mq, mk, mvt)


def _group_partner(x, d, width, lane):
    return jnp.where((lane % width) + d < width,
                     pltpu.roll(x, LANES - d, 1), pltpu.roll(x, width - d, 1))


def _router_gates(logits, bias):
    lane = lax.broadcasted_iota(jnp.int32, logits.shape, 1)
    is_expert = lane < N_EXPERTS
    scores = jax.nn.sigmoid(logits)
    biased = jnp.where(is_expert, scores + bias, NEG_INF)
    G = EXPERTS_PER_GROUP
    member = lane % G
    rank = jnp.zeros(logits.shape, jnp.int32)
    for d in range(1, G):
        other = _group_partner(biased, d, G, lane)
        other_first = (member + d) % G < member
        rank = rank + ((other > biased) | ((other == biased) & other_first)).astype(jnp.int32)
    top2 = rank < 2
    grp_score = jnp.where(top2, biased, 0.0)
    part = grp_score
    for d in range(1, G):
        grp_score = grp_score + _group_partner(part, d, G, lane)
    beaten = jnp.zeros(logits.shape, jnp.bool_)
    grp = lane // G
    for dg in range(1, N_GROUPS):
        other = _group_partner(grp_score, dg * G, N_EXPERTS, lane)
        other_first = (grp + dg) % N_GROUPS < grp
        beaten = beaten | (other > grp_score) | ((other == grp_score) & other_first)
    sel = is_expert & top2 & jnp.logical_not(beaten)
    picked = jnp.where(sel, scores, 0.0)
    return picked / jnp.sum(picked, -1, keepdims=True)


def _outproj_kernel(ret_ref, swa_ref, mla_ref, x_ref, wr_ref, ws_ref, wm_ref, g_ref, b_ref,
                    rw_ref, rb_ref, x1_ref, x1b_ref, gates_ref, *, alpha):
    mix = _dot(ret_ref[...], wr_ref[...]) + _dot(swa_ref[...], ws_ref[...]) + _dot(mla_ref[...], wm_ref[...])
    x1 = _layer_norm(alpha * x_ref[...] + mix, g_ref[...], b_ref[...])
    x1_ref[...] = x1
    hi = x1.astype(BF16)
    x1b_ref[...] = hi
    lo = (x1 - hi.astype(F32)).astype(BF16)
    hi_part = _dot(hi, rw_ref[...])
    logits = hi_part[:, :LANES] + hi_part[:, LANES:] + _dot(lo, rw_ref[:, :LANES])
    gates_ref[...] = _router_gates(logits, rb_ref[...])


def _outproj(ret_o, swa_o, mla_o, x, wr, ws, wm, g, b, rw, rb, *, alpha, tm=512):
    T = x.shape[0]
    row = lambda w: pl.BlockSpec((tm, w), lambda i: (i, 0))
    full = lambda a: pl.BlockSpec(a.shape, lambda i: (0, 0))
    return pl.pallas_call(
        functools.partial(_outproj_kernel, alpha=alpha),
        grid=(T // tm,),
        in_specs=[row(RET_W), row(SWA_W), row(MLA_PAD_W), row(D_MODEL), full(wr), full(ws), full(wm),
                  full(g), full(b), full(rw), full(rb)],
        out_specs=[row(D_MODEL), row(D_MODEL), row(LANES)],
        out_shape=[jax.ShapeDtypeStruct((T, D_MODEL), F32), jax.ShapeDtypeStruct((T, D_MODEL), BF16),
                   jax.ShapeDtypeStruct((T, LANES), F32)],
        compiler_params=_params("parallel"),
        name="outproj",
    )(ret_o, swa_o, mla_o, x, wr, ws, wm, g, b, rw, rb)


MOE_CHUNK = 128
MOE_SEG_ALIGN = 16


def _moe_rows(tm):
    return -(-(2 * tm + N_EXPERTS * (MOE_SEG_ALIGN - 1) + MOE_CHUNK) // LANES) * LANES


def _moe_kernel(xb_ref, x_ref, gates_ref, wgu_ref, wd_ref, g_ref, b_ref, o_ref,
                xs_ref, ys_ref, start_ref, chunks_ref, *, alpha):
    tm = xb_ref.shape[0]
    R = xs_ref.shape[0]
    gates = gates_ref[...]
    big = float(4 * R)

    def strict_lower(n):
        return lax.broadcasted_iota(jnp.int32, (n, n), 1) < lax.broadcasted_iota(jnp.int32, (n, n), 0)

    def strict_upper(n):
        return lax.broadcasted_iota(jnp.int32, (n, n), 0) < lax.broadcasted_iota(jnp.int32, (n, n), 1)

    def seg_pad(count):
        return jnp.floor((count + (MOE_SEG_ALIGN - 1.0)) * (1.0 / MOE_SEG_ALIGN)) * MOE_SEG_ALIGN

    one = lambda mask: jnp.where(mask, 1.0, 0.0).astype(BF16)

    sel = (gates > 0.0) & (lax.broadcasted_iota(jnp.int32, gates.shape, 1) < N_EXPERTS)
    rank = _dot(one(strict_lower(tm)), one(sel))
    count = jnp.sum(jnp.where(sel, 1.0, 0.0), 0, keepdims=True)
    pad = jnp.broadcast_to(seg_pad(count), (8, LANES)).astype(BF16)
    offset = _dot(pad, one(strict_upper(LANES)))[0:1]
    slot = offset + rank
    slot_a = jnp.min(jnp.where(sel, slot, big), -1, keepdims=True)
    slot_b = jnp.max(jnp.where(sel, slot, -1.0), -1, keepdims=True)
    gate_a = jnp.sum(jnp.where(sel & (slot == slot_a), gates, 0.0), -1, keepdims=True)
    gate_b = jnp.sum(jnp.where(sel & (slot == slot_b), gates, 0.0), -1, keepdims=True)

    gates_t = gates.T
    sel_t = (gates_t > 0.0) & (lax.broadcasted_iota(jnp.int32, gates_t.shape, 0) < N_EXPERTS)
    rank_t = _dot(one(sel_t), one(strict_upper(tm)))
    count_t = jnp.sum(jnp.where(sel_t, 1.0, 0.0), -1, keepdims=True)
    pad_t = jnp.broadcast_to(seg_pad(count_t), (LANES, LANES)).astype(BF16)
    slot_t = _dot(one(strict_lower(LANES)), pad_t)[:, 0:1] + rank_t
    slot_a_t = jnp.min(jnp.where(sel_t, slot_t, big), 0, keepdims=True)
    slot_b_t = jnp.max(jnp.where(sel_t, slot_t, -1.0), 0, keepdims=True)

    row_id = lax.broadcasted_iota(jnp.int32, (R, tm), 0).astype(F32)
    xs_ref[...] = _dot(one((row_id == slot_a_t) | (row_id == slot_b_t)), xb_ref[...]).astype(BF16)

    @pl.when(pl.program_id(0) == 0)
    def _():
        ys_ref[...] = jnp.zeros_like(ys_ref)

    offset_i = offset.astype(jnp.int32)
    chunks_i = jnp.floor((count + (MOE_CHUNK - 1.0)) * (1.0 / MOE_CHUNK)).astype(jnp.int32)
    for e in range(N_EXPERTS):
        start_ref[e] = offset_i[0, e]
        chunks_ref[e] = chunks_i[0, e]

    def chunk_rows(e, c):
        return pl.ds(pl.multiple_of(start_ref[e] + c * MOE_CHUNK, MOE_SEG_ALIGN), MOE_CHUNK)

    def activation(e, rows):
        h = _dot(xs_ref[rows, :], wgu_ref[e])
        up, lin = h[:, :D_EXPERT], h[:, D_EXPERT:]
        return (up * jax.nn.sigmoid(up) * lin).astype(BF16)

    def project(e, rows, a):
        ys_ref[rows, :] = _dot(a, wd_ref[e]).astype(BF16)

    def later_chunks(e, _):
        def body(c, _):
            rows = chunk_rows(e, c)
            project(e, rows, activation(e, rows))
            return 0
        return lax.fori_loop(1, chunks_ref[e], body, 0)

    lax.fori_loop(0, N_EXPERTS, later_chunks, 0)

    a = activation(0, chunk_rows(0, 0))
    for e in range(N_EXPERTS):
        a_next = activation(e + 1, chunk_rows(e + 1, 0)) if e + 1 < N_EXPERTS else None
        project(e, chunk_rows(e, 0), a)
        a = a_next

    col_id = lax.broadcasted_iota(jnp.int32, (tm, R), 1).astype(F32)
    unsort = jnp.where(col_id == slot_a, gate_a, jnp.where(col_id == slot_b, gate_b, 0.0)).astype(BF16)
    o_ref[...] = _layer_norm(alpha * x_ref[...] + _dot(unsort, ys_ref[...]), g_ref[...], b_ref[...])


def _moe(x1b, x1, gates, wgu, wd, g, b, *, alpha, tm=512):
    T = x1.shape[0]
    R = _moe_rows(tm)
    row = lambda w: pl.BlockSpec((tm, w), lambda i: (i, 0))
    full = lambda a: pl.BlockSpec(a.shape, lambda i: (0,) * a.ndim)
    resident = lambda a: pl.BlockSpec(a.shape, lambda i: (0,) * a.ndim, pipeline_mode=pl.Buffered(1))
    return pl.pallas_call(
        functools.partial(_moe_kernel, alpha=alpha),
        grid=(T // tm,),
        in_specs=[row(D_MODEL), row(D_MODEL), row(LANES), resident(wgu), resident(wd), full(g), full(b)],
        out_specs=row(D_MODEL),
        out_shape=jax.ShapeDtypeStruct((T, D_MODEL), F32),
        scratch_shapes=[pltpu.VMEM((R, D_MODEL), BF16), pltpu.VMEM((R, D_MODEL), BF16),
                        pltpu.SMEM((N_EXPERTS,), jnp.int32), pltpu.SMEM((N_EXPERTS,), jnp.int32)],
        compiler_params=_params("arbitrary"),
        name="moe",
    )(x1b, x1, gates, wgu, wd, g, b)


def _rope_tables(seq):
    pos = np.arange(seq, dtype=np.float64)[:, None]
    lane = np.arange(LANES)

    def table(dim, active, offset):
        j = (lane - offset) % dim
        inv = ROPE_THETA ** (-(2.0 * (j % (dim // 2))) / dim)
        ang = pos * inv[None, :]
        sign = np.where(j < dim // 2, -1.0, 1.0)
        cos = np.where(active[None, :], np.cos(ang), 1.0)
        sin = np.where(active[None, :], np.sin(ang) * sign[None, :], 0.0)
        return jnp.asarray(cos, F32), jnp.asarray(sin, F32)

    cr, sr = table(RET_DK, np.ones(LANES, bool), 0)
    cm, sm = table(MLA_ROPE, (lane >= MLA_NOPE) & (lane < MLA_NOPE + MLA_ROPE), MLA_NOPE)
    return cr, sr, cm, sm


def _layer_weights(l, w_in, mla_w_uq, mla_w_ukv, w_out):
    w = w_in[l]
    sizes = (256, 256, 256, 256, SWA_W, SWA_KV_W, SWA_KV_W, MLA_Q_RANK, MLA_KV_RANK, MLA_ROPE)
    offs = np.concatenate([[0], np.cumsum(sizes)])
    seg = [w[:, offs[i]:offs[i + 1]] for i in range(len(sizes))]
    sq = jnp.concatenate([seg[4][:, h * HEAD_DIM:(h + 1) * HEAD_DIM] for h in _SWA_HEAD_ORDER], 1)
    kr = jnp.pad(seg[9], ((0, 0), (MLA_NOPE, LANES - MLA_NOPE - MLA_ROPE)))
    w1 = jnp.concatenate(seg[:4] + [sq, kr, seg[5], seg[6], seg[8], seg[7]], 1).astype(BF16)

    def pad_heads(cols):
        return jnp.concatenate([jnp.pad(c, ((0, 0), (0, LANES - c.shape[1]))) for c in cols], 1)

    dq = MLA_NOPE + MLA_ROPE
    wuq = pad_heads([mla_w_uq[l][:, h * dq:(h + 1) * dq] for h in range(MLA_HEADS)]).astype(BF16)
    dkv = MLA_NOPE + MLA_V
    wuk = pad_heads([mla_w_ukv[l][:, h * dkv:h * dkv + MLA_NOPE] for h in range(MLA_HEADS)]).astype(BF16)
    wuvt = pad_heads([mla_w_ukv[l][:, h * dkv + MLA_NOPE:(h + 1) * dkv] for h in range(MLA_HEADS)]).T.astype(BF16)

    wo = w_out[l]
    wr = wo[:RET_W].astype(BF16)
    ws = jnp.concatenate([wo[RET_W + h * HEAD_DIM:RET_W + (h + 1) * HEAD_DIM] for h in _SWA_HEAD_ORDER],
                         0).astype(BF16)
    base = RET_W + SWA_W
    wm = jnp.concatenate([jnp.pad(wo[base + h * MLA_V:base + (h + 1) * MLA_V], ((0, LANES - MLA_V), (0, 0)))
                          for h in range(MLA_HEADS)], 0).astype(BF16)
    return w1, wuq, wuk, wuvt, wr, ws, wm


def kernel(x, w_in, ret_gn_w, swa_sinks, mla_q_norm_w, mla_kv_norm_w, mla_w_uq, mla_w_ukv, w_out,
           ln1_g, ln1_b, router_w, router_bias, exp_w_gate_up, exp_w_down, ln2_g, ln2_b):
    batch, seq, _ = x.shape
    depth = w_in.shape[0]
    alpha = (2 * depth) ** 0.25
    cr, sr, cm, sm = _rope_tables(seq)

    rw = jnp.pad(router_w, ((0, 0), (0, LANES - N_EXPERTS)))
    rwh = rw.astype(BF16)
    rw = jnp.concatenate([rwh, (rw - rwh.astype(F32)).astype(BF16)], 1)
    rb = jnp.pad(router_bias.astype(F32), (0, LANES - N_EXPERTS)).reshape(1, LANES)

    t = x.reshape(batch * seq, D_MODEL)
    for l in range(depth):
        w1, wuq, wuk, wuvt, wr, ws, wm = _layer_weights(l, w_in, mla_w_uq, mla_w_ukv, w_out)
        rq, rk, rv, rg, sq, sk, sv, mq, mk, mvt = _inproj(
            t, w1, cr, sr, cm, sm, mla_q_norm_w[l].reshape(1, -1), mla_kv_norm_w[l].reshape(1, -1),
            wuq, wuk, wuvt, seq=seq, tm=MLA_TILE)
        ret_o = _retention(rq, rk, rv, rg, ret_gn_w[l].reshape(1, -1), batch=batch, seq=seq)
        sink_col = jnp.repeat(swa_sinks[l].astype(F32), SWA_WINDOW).reshape(-1, 1)
        swa_o = _swa(sq, sk, sv, sink_col, seq=seq)
        mla_o = _mla(mq, mk, mvt, batch=batch, seq=seq, tk=MLA_TILE)
        x1, x1b, gates = _outproj(ret_o, swa_o, mla_o, t, wr, ws, wm,
                                  ln1_g[l].reshape(1, -1), ln1_b[l].reshape(1, -1), rw, rb, alpha=alpha)
        t = _moe(x1b, x1, gates, exp_w_gate_up[l].astype(BF16), exp_w_down[l].astype(BF16),
                 ln2_g[l].reshape(1, -1), ln2_b[l].reshape(1, -1), alpha=alpha)
    return t.reshape(batch, seq, D_MODEL)
```

```python
import functools
import math

import numpy as np
import jax
import jax.numpy as jnp
from jax import lax
from jax.experimental import pallas as pl
from jax.experimental.pallas import tpu as pltpu

D_MODEL = 1024
HEAD_DIM = 64
ROPE_THETA = 10000.0

RET_HEADS = 4
RET_DK = 64
RET_DV = 64
RET_W = RET_HEADS * RET_DV

SWA_Q_HEADS = 6
SWA_KV_HEADS = 2
SWA_WINDOW = 128
SWA_W = SWA_Q_HEADS * HEAD_DIM
SWA_KV_W = SWA_KV_HEADS * HEAD_DIM

MLA_HEADS = 6
MLA_Q_RANK = 384
MLA_KV_RANK = 256
MLA_NOPE = 64
MLA_ROPE = 32
MLA_V = 64

N_EXPERTS = 16
N_GROUPS = 4
EXPERTS_PER_GROUP = N_EXPERTS // N_GROUPS
D_EXPERT = 256

LN_EPS = 1e-5
RMS_EPS = 1e-6
NEG_INF = -1e30

LANES = 128
MLA_PAD_W = MLA_HEADS * LANES
MLA_TILE = 512
VMEM_LIMIT = 56 * 1024 * 1024

F32 = jnp.float32
BF16 = jnp.bfloat16

_C_RQ, _C_RK, _C_RV, _C_RG = 0, 256, 512, 768
_C_SQ, _C_KR, _C_SK, _C_SV = 1024, 1408, 1536, 1664
_C_CKV, _C_CQ, _C_END = 1792, 2048, 2432

_SWA_HEAD_ORDER = (0, 3, 1, 4, 2, 5)


def _params(*sem, flags=None):
    return pltpu.CompilerParams(dimension_semantics=sem, vmem_limit_bytes=VMEM_LIMIT, flags=flags)


def _dot(a, b):
    return jnp.dot(a, b, preferred_element_type=F32)


def _dot_nt(a, b):
    return lax.dot_general(a, b, (((1,), (1,)), ((), ())), preferred_element_type=F32)


def _dot_tn(a, b):
    return lax.dot_general(a, b, (((0,), (0,)), ((), ())), preferred_element_type=F32)


def _split_dot(x, w_bf16):
    hi = x.astype(BF16)
    lo = (x - hi.astype(F32)).astype(BF16)
    return _dot(hi, w_bf16) + _dot(lo, w_bf16)


def _rope(h, cos, sin_signed, half):
    fwd = pltpu.roll(h, LANES - half, 1)
    bwd = pltpu.roll(h, half, 1)
    lane = lax.broadcasted_iota(jnp.int32, h.shape, 1)
    rot = jnp.where((lane % (2 * half)) < half, fwd, bwd)
    return h * cos + rot * sin_signed


def _layer_norm(y, g, b):
    mu = jnp.mean(y, -1, keepdims=True)
    d = y - mu
    var = jnp.mean(d * d, -1, keepdims=True)
    return d * lax.rsqrt(var + LN_EPS) * g + b


def _rms_norm(x, g):
    return x * lax.rsqrt(jnp.mean(x * x, -1, keepdims=True) + RMS_EPS) * g


def _inproj_kernel(x_ref, w1_ref, cr_ref, sr_ref, cm_ref, sm_ref, qnw_ref, kvnw_ref,
                   wuq_ref, wuk_ref, wuvt_ref,
                   rq_ref, rk_ref, rv_ref, rg_ref, sq_ref, sk_ref, mq_ref, mk_ref, svt_ref, mvt_ref,
                   *, mla_q_scale):
    xb = x_ref[...].astype(BF16)

    def proj(a, b):
        return _dot(xb, w1_ref[:, a:b])

    cq = proj(_C_CQ, _C_END)
    ckv = proj(_C_CKV, _C_CQ)
    rq, rk = proj(_C_RQ, _C_RK), proj(_C_RK, _C_RV)
    rv, rg = proj(_C_RV, _C_RG), proj(_C_RG, _C_SQ)
    sq_kr = proj(_C_SQ, _C_SK)
    sk_sv = proj(_C_SK, _C_CKV)
    cq = _rms_norm(cq, qnw_ref[...]).astype(BF16)
    ckv = _rms_norm(ckv, kvnw_ref[...]).astype(BF16)
    q = _dot(cq, wuq_ref[...])
    kn = _dot(ckv, wuk_ref[...])
    vt = _dot_nt(wuvt_ref[...], ckv)

    cr, sr = cr_ref[...], sr_ref[...]
    for g in range(RET_W // LANES):
        lanes = slice(g * LANES, (g + 1) * LANES)
        rq_ref[:, lanes] = _rope(rq[:, lanes], cr, sr, RET_DK // 2).astype(BF16)
        rk_ref[:, lanes] = (_rope(rk[:, lanes], cr, sr, RET_DK // 2) * (RET_DK ** -0.5)).astype(BF16)
    rv_ref[...] = rv.astype(BF16)
    rg_ref[...] = rg.astype(BF16)
    sq_ref[...] = (sq_kr[:, :SWA_W] * (HEAD_DIM ** -0.5)).astype(BF16)
    sk_ref[...] = sk_sv[:, :SWA_KV_W].astype(BF16)
    svt_ref[0] = sk_sv[:, SWA_KV_W:].T.astype(BF16)

    cm, sm = cm_ref[...], sm_ref[...]
    kpe = _rope(sq_kr[:, SWA_W:], cm, sm, MLA_ROPE // 2)
    for h in range(MLA_HEADS):
        lo, hi = h * LANES, (h + 1) * LANES
        mq_ref[:, lo:hi] = (_rope(q[:, lo:hi], cm, sm, MLA_ROPE // 2) * mla_q_scale).astype(BF16)
        mk_ref[:, lo:hi] = (kn[:, lo:hi] + kpe).astype(BF16)
    feat = lax.broadcasted_iota(jnp.int32, vt.shape, 0) % LANES
    mvt_ref[0] = (vt + (feat == MLA_V).astype(F32)).astype(BF16)


def _inproj(x, w1, cr, sr, cm, sm, qnw, kvnw, wuq, wuk, wuvt, *, seq, tm):
    T = x.shape[0]
    nt = T // tm
    npos = seq // tm
    row = lambda w: pl.BlockSpec((tm, w), lambda i: (i, 0))
    pos = lambda: pl.BlockSpec((tm, LANES), lambda i: (i % npos, 0))
    full = lambda a: pl.BlockSpec(a.shape, lambda i: (0, 0))
    widths = (RET_W, RET_W, RET_W, RET_W, SWA_W, SWA_KV_W, MLA_PAD_W, MLA_PAD_W)
    widths_t = (SWA_KV_W, MLA_PAD_W)
    mla_q_scale = (MLA_NOPE + MLA_ROPE) ** -0.5 * math.log2(math.e)
    return pl.pallas_call(
        functools.partial(_inproj_kernel, mla_q_scale=mla_q_scale),
        grid=(nt,),
        in_specs=[row(D_MODEL), full(w1), pos(), pos(), pos(), pos(), full(qnw), full(kvnw),
                  full(wuq), full(wuk), full(wuvt)],
        out_specs=[row(w) for w in widths] + [pl.BlockSpec((1, w, tm), lambda i: (i, 0, 0)) for w in widths_t],
        out_shape=[jax.ShapeDtypeStruct((T, w), BF16) for w in widths]
        + [jax.ShapeDtypeStruct((nt, w, tm), BF16) for w in widths_t],
        compiler_params=_params("parallel"),
        name="inproj",
    )(x, w1, cr, sr, cm, sm, qnw, kvnw, wuq, wuk, wuvt)


def _retention_kernel(q_ref, k_ref, v_ref, g_ref, gnw_ref, dec_ref, xi_ref, zeta_ref, dm_ref, bm_ref,
                      avg_ref, o_ref, state_ref, *, chunk, n_chunks):
    @pl.when(pl.program_id(1) == 0)
    def _():
        state_ref[...] = jnp.zeros_like(state_ref)

    lane_head = lax.broadcasted_iota(jnp.int32, (chunk, RET_W), 1) // RET_DV
    avg = avg_ref[...]
    chunks = [slice(c * chunk, (c + 1) * chunk) for c in range(n_chunks)]

    def raw_scores(rows):
        q = q_ref[rows, :]
        zero = jnp.zeros_like(q)
        q_heads = jnp.concatenate([jnp.where(lane_head == h, q, zero) for h in range(RET_HEADS)], 0)
        return _dot_nt(q_heads, k_ref[rows, :])

    scores = [raw_scores(rows) for rows in chunks]
    updates = [_dot_tn((k_ref[rows, :].astype(F32) * zeta_ref[...]).astype(BF16), v_ref[rows, :])
               for rows in chunks]
    inner = [_dot((s * dec_ref[...]).astype(BF16), v_ref[rows, :]) for s, rows in zip(scores, chunks)]

    state = state_ref[...]
    outs = []
    for rows, upd, inner_heads in zip(chunks, updates, inner):
        o = _dot((q_ref[rows, :].astype(F32) * xi_ref[...]).astype(BF16), state.astype(BF16))
        for h in range(RET_HEADS):
            o = o + jnp.where(lane_head == h, inner_heads[h * chunk:(h + 1) * chunk], 0.0)
        outs.append(o)
        state = dm_ref[...] * state + bm_ref[...] * upd
    state_ref[...] = state

    mus = [_split_dot(o, avg) for o in outs]
    devs = [o - mu for o, mu in zip(outs, mus)]
    variances = [_split_dot(d * d, avg) for d in devs]
    for rows, d, var in zip(chunks, devs, variances):
        on = d * lax.rsqrt(var + LN_EPS) * gnw_ref[...]
        gate = g_ref[rows, :].astype(F32)
        o_ref[rows, :] = (on * (gate * jax.nn.sigmoid(gate))).astype(BF16)


def _retention_tables(chunk):
    H = RET_HEADS
    gamma = 1.0 - 2.0 ** (-5.0 - np.arange(H, dtype=np.float64))
    log_g = np.log(gamma)
    idx = np.arange(chunk, dtype=np.float64)
    rel = idx[:, None] - idx[None, :]
    dec = np.where(rel[None] >= 0, np.exp(np.maximum(rel, 0.0)[None] * log_g[:, None, None]), 0.0)
    dec = dec.reshape(H * chunk, chunk)
    lane_head = np.arange(RET_W) // RET_DV
    xi = np.exp((idx[:, None] + 1.0) * log_g[lane_head][None, :])
    zeta = np.exp((chunk - 1.0 - idx[:, None]) * log_g[lane_head][None, :])
    same = lane_head[:, None] == lane_head[None, :]
    dm = np.where(same, np.exp(chunk * log_g)[lane_head][:, None], 0.0)
    bm = same.astype(np.float64)
    avg = bm / RET_DV
    f = lambda a: jnp.asarray(a, F32)
    return f(dec), f(xi), f(zeta), f(dm), f(bm), jnp.asarray(avg, BF16)


def _retention(rq, rk, rv, rg, gnw, *, batch, seq, blk=512, chunk=128):
    T = rq.shape[0]
    nb = seq // blk
    tables = _retention_tables(chunk)
    row = pl.BlockSpec((blk, RET_W), lambda b, i: (b * nb + i, 0))
    full = lambda a: pl.BlockSpec(a.shape, lambda b, i: (0, 0))
    return pl.pallas_call(
        functools.partial(_retention_kernel, chunk=chunk, n_chunks=blk // chunk),
        grid=(batch, nb),
        in_specs=[row, row, row, row, full(gnw)] + [full(t) for t in tables],
        out_specs=row,
        out_shape=jax.ShapeDtypeStruct((T, RET_W), BF16),
        scratch_shapes=[pltpu.VMEM((RET_W, RET_W), F32)],
        compiler_params=_params("parallel", "arbitrary"),
        name="retention",
    )(rq, rk, rv, rg, gnw, *tables)


def _swa_kernel(q_ref, kc_ref, kp_ref, vtc_ref, vtp_ref, sink_ref, o_ref, *, blk, seq):
    L = SWA_WINDOW
    n_pairs = SWA_Q_HEADS // 2
    n_sub = blk // L
    lane_lo = lax.broadcasted_iota(jnp.int32, (L, LANES), 1) < HEAD_DIM
    feat_lo = lax.broadcasted_iota(jnp.int32, (LANES, L), 0) < HEAD_DIM
    key = lax.broadcasted_iota(jnp.int32, (2 * L, SWA_Q_HEADS * L), 0)
    qry = lax.broadcasted_iota(jnp.int32, (2 * L, SWA_Q_HEADS * L), 1) % L
    band = (key > qry) & (key <= qry + L)
    sink = sink_ref[...]

    def scores(sb):
        r0 = sb * L
        k_prev = kp_ref[...] if sb == 0 else kc_ref[r0 - L:r0, :]
        k_win = jnp.concatenate([k_prev, kc_ref[r0:r0 + L, :]], 0)
        pairs = [q_ref[r0:r0 + L, g * LANES:(g + 1) * LANES] for g in range(n_pairs)]
        zero = jnp.zeros_like(pairs[0])
        q_heads = jnp.concatenate([jnp.where(lane_lo, p, zero) for p in pairs]
                                  + [jnp.where(lane_lo, zero, p) for p in pairs], 0)
        return _dot_nt(k_win, q_heads)

    def attend(sb, st):
        r0 = sb * L
        pos0 = (pl.program_id(0) * blk + r0) % seq
        first_key = jnp.where(pos0 > 0, 0, L)
        st = jnp.where(band & (key >= first_key), st, NEG_INF)
        m = jnp.maximum(jnp.max(st, 0, keepdims=True), sink)
        p = jnp.exp(st - m)
        denom = jnp.sum(p, 0, keepdims=True) + jnp.exp(sink - m)
        if sb == 0:
            vt_win = jnp.concatenate([vtp_ref[0], vtc_ref[0, :, :L]], 1)
        else:
            vt_win = vtc_ref[0, :, r0 - L:r0 + L]
        ot = _dot(vt_win, (p / denom).astype(BF16))
        for g in range(n_pairs):
            o_lo = ot[:, g * L:(g + 1) * L]
            o_hi = ot[:, (g + n_pairs) * L:(g + n_pairs + 1) * L]
            o_ref[r0:r0 + L, g * LANES:(g + 1) * LANES] = jnp.where(feat_lo, o_lo, o_hi).T.astype(BF16)

    st = scores(0)
    for sb in range(n_sub):
        st_next = scores(sb + 1) if sb + 1 < n_sub else None
        attend(sb, st)
        st = st_next


def _swa(sq, sk, svt, sink_row, *, seq, blk):
    T = sq.shape[0]
    L = SWA_WINDOW
    per = blk // L
    cur = lambda w: pl.BlockSpec((blk, w), lambda i: (i, 0))
    k_prev = pl.BlockSpec((L, SWA_KV_W), lambda i: (jnp.maximum(i * per - 1, 0), 0))
    vt_cur = pl.BlockSpec((1, SWA_KV_W, blk), lambda i: (i, 0, 0))
    vt_prev = pl.BlockSpec((1, SWA_KV_W, L), lambda i: (jnp.maximum(i - 1, 0), 0, per - 1))
    return pl.pallas_call(
        functools.partial(_swa_kernel, blk=blk, seq=seq),
        grid=(T // blk,),
        in_specs=[cur(SWA_W), cur(SWA_KV_W), k_prev, vt_cur, vt_prev,
                  pl.BlockSpec(sink_row.shape, lambda i: (0, 0))],
        out_specs=cur(SWA_W),
        out_shape=jax.ShapeDtypeStruct((T, SWA_W), BF16),
        compiler_params=_params("parallel"),
        name="swa",
    )(sq, sk, sk, svt, svt, sink_row)


def _mla_kernel(q_ref, k_ref, vt_ref, o_ref, st0_ref, st1_ref, m_ref, acc_ref, *, tq, tk, heads):
    i = pl.program_id(2)
    key = lax.broadcasted_iota(jnp.int32, (tk, tq), 0)
    qry = lax.broadcasted_iota(jnp.int32, (tk, tq), 1)
    head_lanes = [slice(h * LANES, (h + 1) * LANES) for h in range(heads)]
    st_refs = (st0_ref, st1_ref)

    def scores(j, lanes):
        start = pl.multiple_of(j * tk, tk)
        return _dot_nt(k_ref[pl.ds(start, tk), lanes], q_ref[:, lanes])

    def step(j, cur, key_offset=None, prefetch=True):
        for h, lanes in enumerate(head_lanes):
            if prefetch:
                st_refs[1 - cur][h] = scores(j + 1, lanes)
            st = st_refs[cur][h]
            if key_offset is not None:
                st = jnp.where(key + key_offset <= qry, st, NEG_INF)
            m = m_ref[h]
            m_new = jnp.maximum(m, jnp.max(st, 0, keepdims=True))
            pt = jnp.exp2(st - m_new).astype(BF16)
            acc_ref[h] = jnp.exp2(m - m_new) * acc_ref[h] + _dot(vt_ref[j, lanes, :], pt)
            m_ref[h] = m_new

    m_ref[...] = jnp.full(m_ref.shape, NEG_INF, F32)
    acc_ref[...] = jnp.zeros(acc_ref.shape, F32)
    for h, lanes in enumerate(head_lanes):
        st0_ref[h] = scores(0, lanes)

    def pair(p, _):
        step(2 * p, 0)
        step(2 * p + 1, 1)
        return 0

    lax.fori_loop(0, i, pair, 0)
    step(2 * i, 0, key_offset=0)
    step(2 * i + 1, 1, key_offset=tk, prefetch=False)
    for h in range(heads):
        acc = acc_ref[h]
        o_ref[:, h * LANES:(h + 1) * LANES] = (acc / acc[MLA_V:MLA_V + 1, :]).T.astype(BF16)


def _mla(mq, mk, mvt, *, batch, seq, tk, heads=3):
    T = mq.shape[0]
    tq = 2 * tk
    nq = seq // tq
    qo = pl.BlockSpec((tq, heads * LANES), lambda b, h, i: (b * nq + i, h))
    k = pl.BlockSpec((seq, heads * LANES), lambda b, h, i: (b, h))
    vt = pl.BlockSpec((seq // tk, heads * LANES, tk), lambda b, h, i: (b, h, 0))
    return pl.pallas_call(
        functools.partial(_mla_kernel, tq=tq, tk=tk, heads=heads),
        grid=(batch, MLA_HEADS // heads, nq),
        in_specs=[qo, k, vt],
        out_specs=qo,
        out_shape=jax.ShapeDtypeStruct((T, MLA_PAD_W), BF16),
        scratch_shapes=[pltpu.VMEM((heads, tk, tq), F32), pltpu.VMEM((heads, tk, tq), F32),
                        pltpu.VMEM((heads, 1, tq), F32), pltpu.VMEM((heads, LANES, tq), F32)],
        compiler_params=_params("parallel", "parallel", "arbitrary"),
        name="mla",
    )(mq, mk, mvt)


def _group_partner(x, d, width, lane):
    return jnp.where((lane % width) + d < width,
                     pltpu.roll(x, LANES - d, 1), pltpu.roll(x, width - d, 1))


def _router_gates(logits, bias):
    lane = lax.broadcasted_iota(jnp.int32, logits.shape, 1)
    is_expert = lane < N_EXPERTS
    scores = jax.nn.sigmoid(logits)
    biased = jnp.where(is_expert, scores + bias, NEG_INF)
    G = EXPERTS_PER_GROUP
    member = lane % G
    rank = jnp.zeros(logits.shape, jnp.int32)
    for d in range(1, G):
        other = _group_partner(biased, d, G, lane)
        other_first = (member + d) % G < member
        rank = rank + ((other > biased) | ((other == biased) & other_first)).astype(jnp.int32)
    top2 = rank < 2
    grp_score = jnp.where(top2, biased, 0.0)
    part = grp_score
    for d in range(1, G):
        grp_score = grp_score + _group_partner(part, d, G, lane)
    beaten = jnp.zeros(logits.shape, jnp.bool_)
    grp = lane // G
    for dg in range(1, N_GROUPS):
        other = _group_partner(grp_score, dg * G, N_EXPERTS, lane)
        other_first = (grp + dg) % N_GROUPS < grp
        beaten = beaten | (other > grp_score) | ((other == grp_score) & other_first)
    sel = is_expert & top2 & jnp.logical_not(beaten)
    picked = jnp.where(sel, scores, 0.0)
    return picked / jnp.sum(picked, -1, keepdims=True)


def _outproj_kernel(ret_ref, swa_ref, mla_ref, x_ref, wr_ref, ws_ref, wm_ref, g_ref, b_ref,
                    rw_ref, rb_ref, x1_ref, x1b_ref, gates_ref, *, alpha, n_sub):
    tm = x_ref.shape[0]
    subs = [slice(s * tm // n_sub, (s + 1) * tm // n_sub) for s in range(n_sub)]

    def mix(rows):
        return (_dot(ret_ref[rows, :], wr_ref[...]) + _dot(swa_ref[rows, :], ws_ref[...])
                + _dot(mla_ref[rows, :], wm_ref[...]))

    def finish(rows, mixed):
        x1 = _layer_norm(alpha * x_ref[rows, :] + mixed, g_ref[...], b_ref[...])
        x1_ref[rows, :] = x1
        hi = x1.astype(BF16)
        x1b_ref[rows, :] = hi
        lo = (x1 - hi.astype(F32)).astype(BF16)
        hi_part = _dot(hi, rw_ref[...])
        logits = hi_part[:, :LANES] + hi_part[:, LANES:] + _dot(lo, rw_ref[:, :LANES])
        gates_ref[rows, :] = _router_gates(logits, rb_ref[...])

    mixed = mix(subs[0])
    for s in range(n_sub):
        mixed_next = mix(subs[s + 1]) if s + 1 < n_sub else None
        finish(subs[s], mixed)
        mixed = mixed_next


def _outproj(ret_o, swa_o, mla_o, x, wr, ws, wm, g, b, rw, rb, *, alpha, tm=512, n_sub=4):
    T = x.shape[0]
    row = lambda w: pl.BlockSpec((tm, w), lambda i: (i, 0))
    full = lambda a: pl.BlockSpec(a.shape, lambda i: (0, 0))
    return pl.pallas_call(
        functools.partial(_outproj_kernel, alpha=alpha, n_sub=n_sub),
        grid=(T // tm,),
        in_specs=[row(RET_W), row(SWA_W), row(MLA_PAD_W), row(D_MODEL), full(wr), full(ws), full(wm),
                  full(g), full(b), full(rw), full(rb)],
        out_specs=[row(D_MODEL), row(D_MODEL), row(LANES)],
        out_shape=[jax.ShapeDtypeStruct((T, D_MODEL), F32), jax.ShapeDtypeStruct((T, D_MODEL), BF16),
                   jax.ShapeDtypeStruct((T, LANES), F32)],
        compiler_params=_params("parallel"),
        name="outproj",
    )(ret_o, swa_o, mla_o, x, wr, ws, wm, g, b, rw, rb)


MOE_CHUNK = 128
MOE_SEG_ALIGN = 16


def _moe_rows(tm):
    return -(-(2 * tm + N_EXPERTS * (MOE_SEG_ALIGN - 1) + MOE_CHUNK) // LANES) * LANES


def _moe_kernel(xb_ref, x_ref, gates_ref, wgu_ref, wd_ref, g_ref, b_ref, o_ref,
                xs_ref, ys_ref, start_ref, chunks_ref, *, alpha):
    tm = xb_ref.shape[0]
    R = xs_ref.shape[0]
    gates = gates_ref[...]
    big = float(4 * R)

    def strict_lower(n):
        return lax.broadcasted_iota(jnp.int32, (n, n), 1) < lax.broadcasted_iota(jnp.int32, (n, n), 0)

    def strict_upper(n):
        return lax.broadcasted_iota(jnp.int32, (n, n), 0) < lax.broadcasted_iota(jnp.int32, (n, n), 1)

    def seg_pad(count):
        return jnp.floor((count + (MOE_SEG_ALIGN - 1.0)) * (1.0 / MOE_SEG_ALIGN)) * MOE_SEG_ALIGN

    one = lambda mask: jnp.where(mask, 1.0, 0.0).astype(BF16)

    sel = (gates > 0.0) & (lax.broadcasted_iota(jnp.int32, gates.shape, 1) < N_EXPERTS)
    rank = _dot(one(strict_lower(tm)), one(sel))
    count = jnp.sum(jnp.where(sel, 1.0, 0.0), 0, keepdims=True)
    pad = jnp.broadcast_to(seg_pad(count), (8, LANES)).astype(BF16)
    offset = _dot(pad, one(strict_upper(LANES)))[0:1]
    slot = offset + rank
    slot_a = jnp.min(jnp.where(sel, slot, big), -1, keepdims=True)
    slot_b = jnp.max(jnp.where(sel, slot, -1.0), -1, keepdims=True)
    gate_a = jnp.sum(jnp.where(sel & (slot == slot_a), gates, 0.0), -1, keepdims=True)
    gate_b = jnp.sum(jnp.where(sel & (slot == slot_b), gates, 0.0), -1, keepdims=True)

    gates_t = gates.T
    sel_t = (gates_t > 0.0) & (lax.broadcasted_iota(jnp.int32, gates_t.shape, 0) < N_EXPERTS)
    rank_t = _dot(one(sel_t), one(strict_upper(tm)))
    count_t = jnp.sum(jnp.where(sel_t, 1.0, 0.0), -1, keepdims=True)
    pad_t = jnp.broadcast_to(seg_pad(count_t), (LANES, LANES)).astype(BF16)
    slot_t = _dot(one(strict_lower(LANES)), pad_t)[:, 0:1] + rank_t
    slot_a_t = jnp.min(jnp.where(sel_t, slot_t, big), 0, keepdims=True)
    slot_b_t = jnp.max(jnp.where(sel_t, slot_t, -1.0), 0, keepdims=True)

    row_id = lax.broadcasted_iota(jnp.int32, (R, tm), 0).astype(F32)
    xs_ref[...] = _dot(one((row_id == slot_a_t) | (row_id == slot_b_t)), xb_ref[...]).astype(BF16)

    @pl.when(pl.program_id(0) == 0)
    def _():
        ys_ref[...] = jnp.zeros_like(ys_ref)

    offset_i = offset.astype(jnp.int32)
    chunks_i = jnp.floor((count + (MOE_CHUNK - 1.0)) * (1.0 / MOE_CHUNK)).astype(jnp.int32)
    for e in range(N_EXPERTS):
        start_ref[e] = offset_i[0, e]
        chunks_ref[e] = chunks_i[0, e]

    def chunk_rows(e, c):
        return pl.ds(pl.multiple_of(start_ref[e] + c * MOE_CHUNK, MOE_SEG_ALIGN), MOE_CHUNK)

    def activation(e, rows):
        h = _dot(xs_ref[rows, :], wgu_ref[e])
        up, lin = h[:, :D_EXPERT], h[:, D_EXPERT:]
        return (up * jax.nn.sigmoid(up) * lin).astype(BF16)

    def project(e, rows, a):
        ys_ref[rows, :] = _dot(a, wd_ref[e]).astype(BF16)

    def later_chunks(e, _):
        def body(c, _):
            rows = chunk_rows(e, c)
            project(e, rows, activation(e, rows))
            return 0
        return lax.fori_loop(1, chunks_ref[e], body, 0)

    lax.fori_loop(0, N_EXPERTS, later_chunks, 0)

    a = activation(0, chunk_rows(0, 0))
    for e in range(N_EXPERTS):
        a_next = activation(e + 1, chunk_rows(e + 1, 0)) if e + 1 < N_EXPERTS else None
        project(e, chunk_rows(e, 0), a)
        a = a_next

    col_id = lax.broadcasted_iota(jnp.int32, (tm, R), 1).astype(F32)
    unsort = jnp.where(col_id == slot_a, gate_a, jnp.where(col_id == slot_b, gate_b, 0.0)).astype(BF16)
    o_ref[...] = _layer_norm(alpha * x_ref[...] + _dot(unsort, ys_ref[...]), g_ref[...], b_ref[...])


def _moe(x1b, x1, gates, wgu, wd, g, b, *, alpha, tm=512):
    T = x1.shape[0]
    R = _moe_rows(tm)
    row = lambda w: pl.BlockSpec((tm, w), lambda i: (i, 0))
    full = lambda a: pl.BlockSpec(a.shape, lambda i: (0,) * a.ndim)
    resident = lambda a: pl.BlockSpec(a.shape, lambda i: (0,) * a.ndim, pipeline_mode=pl.Buffered(1))
    return pl.pallas_call(
        functools.partial(_moe_kernel, alpha=alpha),
        grid=(T // tm,),
        in_specs=[row(D_MODEL), row(D_MODEL), row(LANES), resident(wgu), resident(wd), full(g), full(b)],
        out_specs=row(D_MODEL),
        out_shape=jax.ShapeDtypeStruct((T, D_MODEL), F32),
        scratch_shapes=[pltpu.VMEM((R, D_MODEL), BF16), pltpu.VMEM((R, D_MODEL), BF16),
                        pltpu.SMEM((N_EXPERTS,), jnp.int32), pltpu.SMEM((N_EXPERTS,), jnp.int32)],
        compiler_params=_params("arbitrary"),
        name="moe",
    )(x1b, x1, gates, wgu, wd, g, b)


def _rope_tables(seq):
    pos = np.arange(seq, dtype=np.float64)[:, None]
    lane = np.arange(LANES)

    def table(dim, active, offset):
        j = (lane - offset) % dim
        inv = ROPE_THETA ** (-(2.0 * (j % (dim // 2))) / dim)
        ang = pos * inv[None, :]
        sign = np.where(j < dim // 2, -1.0, 1.0)
        cos = np.where(active[None, :], np.cos(ang), 1.0)
        sin = np.where(active[None, :], np.sin(ang) * sign[None, :], 0.0)
        return jnp.asarray(cos, F32), jnp.asarray(sin, F32)

    cr, sr = table(RET_DK, np.ones(LANES, bool), 0)
    cm, sm = table(MLA_ROPE, (lane >= MLA_NOPE) & (lane < MLA_NOPE + MLA_ROPE), MLA_NOPE)
    return cr, sr, cm, sm


def _layer_weights(l, w_in, mla_w_uq, mla_w_ukv, w_out):
    w = w_in[l]
    sizes = (256, 256, 256, 256, SWA_W, SWA_KV_W, SWA_KV_W, MLA_Q_RANK, MLA_KV_RANK, MLA_ROPE)
    offs = np.concatenate([[0], np.cumsum(sizes)])
    seg = [w[:, offs[i]:offs[i + 1]] for i in range(len(sizes))]
    sq = jnp.concatenate([seg[4][:, h * HEAD_DIM:(h + 1) * HEAD_DIM] for h in _SWA_HEAD_ORDER], 1)
    kr = jnp.pad(seg[9], ((0, 0), (MLA_NOPE, LANES - MLA_NOPE - MLA_ROPE)))
    w1 = jnp.concatenate(seg[:4] + [sq, kr, seg[5], seg[6], seg[8], seg[7]], 1).astype(BF16)

    def pad_heads(cols):
        return jnp.concatenate([jnp.pad(c, ((0, 0), (0, LANES - c.shape[1]))) for c in cols], 1)

    dq = MLA_NOPE + MLA_ROPE
    wuq = pad_heads([mla_w_uq[l][:, h * dq:(h + 1) * dq] for h in range(MLA_HEADS)]).astype(BF16)
    dkv = MLA_NOPE + MLA_V
    wuk = pad_heads([mla_w_ukv[l][:, h * dkv:h * dkv + MLA_NOPE] for h in range(MLA_HEADS)]).astype(BF16)
    wuvt = pad_heads([mla_w_ukv[l][:, h * dkv + MLA_NOPE:(h + 1) * dkv] for h in range(MLA_HEADS)]).T.astype(BF16)

    wo = w_out[l]
    wr = wo[:RET_W].astype(BF16)
    ws = jnp.concatenate([wo[RET_W + h * HEAD_DIM:RET_W + (h + 1) * HEAD_DIM] for h in _SWA_HEAD_ORDER],
                         0).astype(BF16)
    base = RET_W + SWA_W
    wm = jnp.concatenate([jnp.pad(wo[base + h * MLA_V:base + (h + 1) * MLA_V], ((0, LANES - MLA_V), (0, 0)))
                          for h in range(MLA_HEADS)], 0).astype(BF16)
    return w1, wuq, wuk, wuvt, wr, ws, wm


def kernel(x, w_in, ret_gn_w, swa_sinks, mla_q_norm_w, mla_kv_norm_w, mla_w_uq, mla_w_ukv, w_out,
           ln1_g, ln1_b, router_w, router_bias, exp_w_gate_up, exp_w_down, ln2_g, ln2_b):
    batch, seq, _ = x.shape
    depth = w_in.shape[0]
    alpha = (2 * depth) ** 0.25
    cr, sr, cm, sm = _rope_tables(seq)

    rw = jnp.pad(router_w, ((0, 0), (0, LANES - N_EXPERTS)))
    rwh = rw.astype(BF16)
    rw = jnp.concatenate([rwh, (rw - rwh.astype(F32)).astype(BF16)], 1)
    rb = jnp.pad(router_bias.astype(F32), (0, LANES - N_EXPERTS)).reshape(1, LANES)

    t = x.reshape(batch * seq, D_MODEL)
    for l in range(depth):
        w1, wuq, wuk, wuvt, wr, ws, wm = _layer_weights(l, w_in, mla_w_uq, mla_w_ukv, w_out)
        rq, rk, rv, rg, sq, sk, mq, mk, svt, mvt = _inproj(
            t, w1, cr, sr, cm, sm, mla_q_norm_w[l].reshape(1, -1), mla_kv_norm_w[l].reshape(1, -1),
            wuq, wuk, wuvt, seq=seq, tm=MLA_TILE)
        ret_o = _retention(rq, rk, rv, rg, ret_gn_w[l].reshape(1, -1), batch=batch, seq=seq)
        sink_row = jnp.repeat(swa_sinks[l].astype(F32), SWA_WINDOW).reshape(1, -1)
        swa_o = _swa(sq, sk, svt, sink_row, seq=seq, blk=MLA_TILE)
        mla_o = _mla(mq, mk, mvt, batch=batch, seq=seq, tk=MLA_TILE)
        x1, x1b, gates = _outproj(ret_o, swa_o, mla_o, t, wr, ws, wm,
                                  ln1_g[l].reshape(1, -1), ln1_b[l].reshape(1, -1), rw, rb, alpha=alpha)
        t = _moe(x1b, x1, gates, exp_w_gate_up[l].astype(BF16), exp_w_down[l].astype(BF16),
                 ln2_g[l].reshape(1, -1), ln2_b[l].reshape(1, -1), alpha=alpha)
    return t.reshape(batch, seq, D_MODEL)
```

```python
import functools
import math

import numpy as np
import jax
import jax.numpy as jnp
from jax import lax
from jax.experimental import pallas as pl
from jax.experimental.pallas import tpu as pltpu

D_MODEL = 1024
HEAD_DIM = 64
ROPE_THETA = 10000.0

RET_HEADS = 4
RET_DK = 64
RET_DV = 64
RET_W = RET_HEADS * RET_DV

SWA_Q_HEADS = 6
SWA_KV_HEADS = 2
SWA_WINDOW = 128
SWA_W = SWA_Q_HEADS * HEAD_DIM
SWA_KV_W = SWA_KV_HEADS * HEAD_DIM

MLA_HEADS = 6
MLA_Q_RANK = 384
MLA_KV_RANK = 256
MLA_NOPE = 64
MLA_ROPE = 32
MLA_V = 64

N_EXPERTS = 16
N_GROUPS = 4
EXPERTS_PER_GROUP = N_EXPERTS // N_GROUPS
D_EXPERT = 256

LN_EPS = 1e-5
RMS_EPS = 1e-6
NEG_INF = -1e30

LANES = 128
MLA_PAD_W = MLA_HEADS * LANES
MLA_TILE = 512
VMEM_LIMIT = 56 * 1024 * 1024

F32 = jnp.float32
BF16 = jnp.bfloat16

_C_RQ, _C_RK, _C_RV, _C_RG = 0, 256, 512, 768
_C_SQ, _C_KR, _C_SK, _C_SV = 1024, 1408, 1536, 1664
_C_CKV, _C_CQ, _C_END = 1792, 2048, 2432

_SWA_HEAD_ORDER = (0, 3, 1, 4, 2, 5)


def _params(*sem, flags=None):
    return pltpu.CompilerParams(dimension_semantics=sem, vmem_limit_bytes=VMEM_LIMIT, flags=flags)


def _layer_spec(a, layer, **kwargs):
    return pl.BlockSpec((None,) + a.shape[1:], lambda *_: (layer,) + (0,) * (a.ndim - 1), **kwargs)


def _dot(a, b):
    return jnp.dot(a, b, preferred_element_type=F32)


def _dot_nt(a, b):
    return lax.dot_general(a, b, (((1,), (1,)), ((), ())), preferred_element_type=F32)


def _dot_tn(a, b):
    return lax.dot_general(a, b, (((0,), (0,)), ((), ())), preferred_element_type=F32)


def _split_dot(x, w_bf16):
    hi = x.astype(BF16)
    lo = (x - hi.astype(F32)).astype(BF16)
    return _dot(hi, w_bf16) + _dot(lo, w_bf16)


def _rope(h, cos, sin_signed, half):
    fwd = pltpu.roll(h, LANES - half, 1)
    bwd = pltpu.roll(h, half, 1)
    lane = lax.broadcasted_iota(jnp.int32, h.shape, 1)
    rot = jnp.where((lane % (2 * half)) < half, fwd, bwd)
    return h * cos + rot * sin_signed


def _layer_norm(y, g, b):
    mu = jnp.mean(y, -1, keepdims=True)
    d = y - mu
    var = jnp.mean(d * d, -1, keepdims=True)
    return d * lax.rsqrt(var + LN_EPS) * g + b


def _rms_norm(x, g):
    return x * lax.rsqrt(jnp.mean(x * x, -1, keepdims=True) + RMS_EPS) * g


def _inproj_kernel(x_ref, w1_ref, cr_ref, sr_ref, cm_ref, sm_ref, qnw_ref, kvnw_ref,
                   wuq_ref, wuk_ref, wuvt_ref,
                   rq_ref, rk_ref, rv_ref, rg_ref, sq_ref, sk_ref, mq_ref, mk_ref, svt_ref, mvt_ref,
                   *, mla_q_scale):
    xb = x_ref[...].astype(BF16)

    def proj(a, b):
        return _dot(xb, w1_ref[:, a:b])

    cq = proj(_C_CQ, _C_END)
    ckv = proj(_C_CKV, _C_CQ)
    rq, rk = proj(_C_RQ, _C_RK), proj(_C_RK, _C_RV)
    rv, rg = proj(_C_RV, _C_RG), proj(_C_RG, _C_SQ)
    sq_kr = proj(_C_SQ, _C_SK)
    sk_sv = proj(_C_SK, _C_CKV)
    cq = _rms_norm(cq, qnw_ref[...]).astype(BF16)
    ckv = _rms_norm(ckv, kvnw_ref[...]).astype(BF16)
    q = _dot(cq, wuq_ref[...])
    kn = _dot(ckv, wuk_ref[...])
    vt = _dot_nt(wuvt_ref[...], ckv)

    cr, sr = cr_ref[...], sr_ref[...]
    for g in range(RET_W // LANES):
        lanes = slice(g * LANES, (g + 1) * LANES)
        rq_ref[:, lanes] = _rope(rq[:, lanes], cr, sr, RET_DK // 2).astype(BF16)
        rk_ref[:, lanes] = (_rope(rk[:, lanes], cr, sr, RET_DK // 2) * (RET_DK ** -0.5)).astype(BF16)
    rv_ref[...] = rv.astype(BF16)
    rg_ref[...] = rg.astype(BF16)
    sq_ref[...] = (sq_kr[:, :SWA_W] * (HEAD_DIM ** -0.5)).astype(BF16)
    sk_ref[...] = sk_sv[:, :SWA_KV_W].astype(BF16)
    svt_ref[0] = sk_sv[:, SWA_KV_W:].T.astype(BF16)

    cm, sm = cm_ref[...], sm_ref[...]
    kpe = _rope(sq_kr[:, SWA_W:], cm, sm, MLA_ROPE // 2)
    for h in range(MLA_HEADS):
        lo, hi = h * LANES, (h + 1) * LANES
        mq_ref[:, lo:hi] = (_rope(q[:, lo:hi], cm, sm, MLA_ROPE // 2) * mla_q_scale).astype(BF16)
        mk_ref[:, lo:hi] = (kn[:, lo:hi] + kpe).astype(BF16)
    feat = lax.broadcasted_iota(jnp.int32, vt.shape, 0) % LANES
    mvt_ref[0] = (vt + (feat == MLA_V).astype(F32)).astype(BF16)


def _inproj(x, w1, cr, sr, cm, sm, qnw, kvnw, wuq, wuk, wuvt, *, layer, seq, tm):
    T = x.shape[0]
    nt = T // tm
    npos = seq // tm
    row = lambda w: pl.BlockSpec((tm, w), lambda i: (i, 0))
    pos = lambda: pl.BlockSpec((tm, LANES), lambda i: (i % npos, 0))
    full = lambda a: _layer_spec(a, layer)
    widths = (RET_W, RET_W, RET_W, RET_W, SWA_W, SWA_KV_W, MLA_PAD_W, MLA_PAD_W)
    widths_t = (SWA_KV_W, MLA_PAD_W)
    mla_q_scale = (MLA_NOPE + MLA_ROPE) ** -0.5 * math.log2(math.e)
    return pl.pallas_call(
        functools.partial(_inproj_kernel, mla_q_scale=mla_q_scale),
        grid=(nt,),
        in_specs=[row(D_MODEL), full(w1), pos(), pos(), pos(), pos(), full(qnw), full(kvnw),
                  full(wuq), full(wuk), full(wuvt)],
        out_specs=[row(w) for w in widths] + [pl.BlockSpec((1, w, tm), lambda i: (i, 0, 0)) for w in widths_t],
        out_shape=[jax.ShapeDtypeStruct((T, w), BF16) for w in widths]
        + [jax.ShapeDtypeStruct((nt, w, tm), BF16) for w in widths_t],
        compiler_params=_params("parallel"),
        name="inproj",
    )(x, w1, cr, sr, cm, sm, qnw, kvnw, wuq, wuk, wuvt)


def _retention_kernel(q_ref, k_ref, v_ref, g_ref, gnw_ref, dec_ref, xi_ref, zeta_ref, dm_ref, bm_ref,
                      avg_ref, o_ref, state_ref, *, chunk, n_chunks):
    @pl.when(pl.program_id(1) == 0)
    def _():
        state_ref[...] = jnp.zeros_like(state_ref)

    lane_head = lax.broadcasted_iota(jnp.int32, (chunk, RET_W), 1) // RET_DV
    avg = avg_ref[...]
    chunks = [slice(c * chunk, (c + 1) * chunk) for c in range(n_chunks)]

    def raw_scores(rows):
        q = q_ref[rows, :]
        zero = jnp.zeros_like(q)
        q_heads = jnp.concatenate([jnp.where(lane_head == h, q, zero) for h in range(RET_HEADS)], 0)
        return _dot_nt(q_heads, k_ref[rows, :])

    scores = [raw_scores(rows) for rows in chunks]
    updates = [_dot_tn((k_ref[rows, :].astype(F32) * zeta_ref[...]).astype(BF16), v_ref[rows, :])
               for rows in chunks]
    inner = [_dot((s * dec_ref[...]).astype(BF16), v_ref[rows, :]) for s, rows in zip(scores, chunks)]

    state = state_ref[...]
    outs = []
    for rows, upd, inner_heads in zip(chunks, updates, inner):
        o = _dot((q_ref[rows, :].astype(F32) * xi_ref[...]).astype(BF16), state.astype(BF16))
        for h in range(RET_HEADS):
            o = o + jnp.where(lane_head == h, inner_heads[h * chunk:(h + 1) * chunk], 0.0)
        outs.append(o)
        state = dm_ref[...] * state + bm_ref[...] * upd
    state_ref[...] = state

    mus = [_split_dot(o, avg) for o in outs]
    devs = [o - mu for o, mu in zip(outs, mus)]
    variances = [_split_dot(d * d, avg) for d in devs]
    for rows, d, var in zip(chunks, devs, variances):
        on = d * lax.rsqrt(var + LN_EPS) * gnw_ref[...]
        gate = g_ref[rows, :].astype(F32)
        o_ref[rows, :] = (on * (gate * jax.nn.sigmoid(gate))).astype(BF16)


def _retention_tables(chunk):
    H = RET_HEADS
    gamma = 1.0 - 2.0 ** (-5.0 - np.arange(H, dtype=np.float64))
    log_g = np.log(gamma)
    idx = np.arange(chunk, dtype=np.float64)
    rel = idx[:, None] - idx[None, :]
    dec = np.where(rel[None] >= 0, np.exp(np.maximum(rel, 0.0)[None] * log_g[:, None, None]), 0.0)
    dec = dec.reshape(H * chunk, chunk)
    lane_head = np.arange(RET_W) // RET_DV
    xi = np.exp((idx[:, None] + 1.0) * log_g[lane_head][None, :])
    zeta = np.exp((chunk - 1.0 - idx[:, None]) * log_g[lane_head][None, :])
    same = lane_head[:, None] == lane_head[None, :]
    dm = np.where(same, np.exp(chunk * log_g)[lane_head][:, None], 0.0)
    bm = same.astype(np.float64)
    avg = bm / RET_DV
    f = lambda a: jnp.asarray(a, F32)
    return f(dec), f(xi), f(zeta), f(dm), f(bm), jnp.asarray(avg, BF16)


def _retention(rq, rk, rv, rg, gnw, *, layer, batch, seq, blk=512, chunk=128):
    T = rq.shape[0]
    nb = seq // blk
    tables = _retention_tables(chunk)
    row = pl.BlockSpec((blk, RET_W), lambda b, i: (b * nb + i, 0))
    full = lambda a: pl.BlockSpec(a.shape, lambda b, i: (0, 0))
    return pl.pallas_call(
        functools.partial(_retention_kernel, chunk=chunk, n_chunks=blk // chunk),
        grid=(batch, nb),
        in_specs=[row, row, row, row, _layer_spec(gnw, layer)] + [full(t) for t in tables],
        out_specs=row,
        out_shape=jax.ShapeDtypeStruct((T, RET_W), BF16),
        scratch_shapes=[pltpu.VMEM((RET_W, RET_W), F32)],
        compiler_params=_params("parallel", "arbitrary"),
        name="retention",
    )(rq, rk, rv, rg, gnw, *tables)


def _swa_kernel(q_ref, kc_ref, kp_ref, vtc_ref, vtp_ref, sink_ref, o_ref, *, blk, seq):
    L = SWA_WINDOW
    n_pairs = SWA_Q_HEADS // 2
    n_sub = blk // L
    lane_lo = lax.broadcasted_iota(jnp.int32, (L, LANES), 1) < HEAD_DIM
    feat_lo = lax.broadcasted_iota(jnp.int32, (LANES, L), 0) < HEAD_DIM
    key = lax.broadcasted_iota(jnp.int32, (2 * L, SWA_Q_HEADS * L), 0)
    qry = lax.broadcasted_iota(jnp.int32, (2 * L, SWA_Q_HEADS * L), 1) % L
    band = (key > qry) & (key <= qry + L)
    sink = sink_ref[...]

    def scores(sb):
        r0 = sb * L
        k_prev = kp_ref[...] if sb == 0 else kc_ref[r0 - L:r0, :]
        k_win = jnp.concatenate([k_prev, kc_ref[r0:r0 + L, :]], 0)
        pairs = [q_ref[r0:r0 + L, g * LANES:(g + 1) * LANES] for g in range(n_pairs)]
        zero = jnp.zeros_like(pairs[0])
        q_heads = jnp.concatenate([jnp.where(lane_lo, p, zero) for p in pairs]
                                  + [jnp.where(lane_lo, zero, p) for p in pairs], 0)
        return _dot_nt(k_win, q_heads)

    def attend(sb, st):
        r0 = sb * L
        pos0 = (pl.program_id(0) * blk + r0) % seq
        first_key = jnp.where(pos0 > 0, 0, L)
        st = jnp.where(band & (key >= first_key), st, NEG_INF)
        m = jnp.maximum(jnp.max(st, 0, keepdims=True), sink)
        p = jnp.exp(st - m)
        denom = jnp.sum(p, 0, keepdims=True) + jnp.exp(sink - m)
        if sb == 0:
            vt_win = jnp.concatenate([vtp_ref[0], vtc_ref[0, :, :L]], 1)
        else:
            vt_win = vtc_ref[0, :, r0 - L:r0 + L]
        ot = _dot(vt_win, (p / denom).astype(BF16))
        for g in range(n_pairs):
            o_lo = ot[:, g * L:(g + 1) * L]
            o_hi = ot[:, (g + n_pairs) * L:(g + n_pairs + 1) * L]
            o_ref[r0:r0 + L, g * LANES:(g + 1) * LANES] = jnp.where(feat_lo, o_lo, o_hi).T.astype(BF16)

    st = scores(0)
    for sb in range(n_sub):
        st_next = scores(sb + 1) if sb + 1 < n_sub else None
        attend(sb, st)
        st = st_next


def _swa(sq, sk, svt, sink_row, *, layer, seq, blk):
    T = sq.shape[0]
    L = SWA_WINDOW
    per = blk // L
    cur = lambda w: pl.BlockSpec((blk, w), lambda i: (i, 0))
    k_prev = pl.BlockSpec((L, SWA_KV_W), lambda i: (jnp.maximum(i * per - 1, 0), 0))
    vt_cur = pl.BlockSpec((1, SWA_KV_W, blk), lambda i: (i, 0, 0))
    vt_prev = pl.BlockSpec((1, SWA_KV_W, L), lambda i: (jnp.maximum(i - 1, 0), 0, per - 1))
    return pl.pallas_call(
        functools.partial(_swa_kernel, blk=blk, seq=seq),
        grid=(T // blk,),
        in_specs=[cur(SWA_W), cur(SWA_KV_W), k_prev, vt_cur, vt_prev, _layer_spec(sink_row, layer)],
        out_specs=cur(SWA_W),
        out_shape=jax.ShapeDtypeStruct((T, SWA_W), BF16),
        compiler_params=_params("parallel"),
        name="swa",
    )(sq, sk, sk, svt, svt, sink_row)


def _mla_kernel(q_ref, k_ref, vt_ref, o_ref, st0_ref, st1_ref, m_ref, acc_ref, *, tq, tk, heads):
    i = pl.program_id(2)
    head_lanes = [slice(h * LANES, (h + 1) * LANES) for h in range(heads)]
    st_refs = (st0_ref, st1_ref)
    all_q = slice(0, tq)
    late_q = slice(tk, tq)

    def scores(j, lanes, qs):
        start = pl.multiple_of(j * tk, tk)
        return _dot_nt(k_ref[pl.ds(start, tk), lanes], q_ref[qs, lanes])

    def step(j, cur, qs=all_q, diagonal=False, next_qs=all_q):
        for h, lanes in enumerate(head_lanes):
            if next_qs is not None:
                st_refs[1 - cur][h, :, next_qs] = scores(j + 1, lanes, next_qs)
            st = st_refs[cur][h, :, qs]
            if diagonal:
                key = lax.broadcasted_iota(jnp.int32, st.shape, 0)
                qry = lax.broadcasted_iota(jnp.int32, st.shape, 1)
                st = jnp.where(key <= qry, st, NEG_INF)
            m = m_ref[h][:, qs]
            m_new = jnp.maximum(m, jnp.max(st, 0, keepdims=True))
            pt = jnp.exp2(st - m_new).astype(BF16)
            acc_ref[h, :, qs] = jnp.exp2(m - m_new) * acc_ref[h, :, qs] + _dot(vt_ref[j, lanes, :], pt)
            if qs is all_q:
                m_ref[h] = m_new

    m_ref[...] = jnp.full(m_ref.shape, NEG_INF, F32)
    acc_ref[...] = jnp.zeros(acc_ref.shape, F32)
    for h, lanes in enumerate(head_lanes):
        st0_ref[h] = scores(0, lanes, all_q)

    def pair(p, _):
        step(2 * p, 0)
        step(2 * p + 1, 1)
        return 0

    lax.fori_loop(0, i, pair, 0)
    step(2 * i, 0, diagonal=True, next_qs=late_q)
    step(2 * i + 1, 1, qs=late_q, diagonal=True, next_qs=None)
    for h in range(heads):
        acc = acc_ref[h]
        o_ref[:, h * LANES:(h + 1) * LANES] = (acc / acc[MLA_V:MLA_V + 1, :]).T.astype(BF16)


def _mla(mq, mk, mvt, *, batch, seq, tk, heads=3):
    T = mq.shape[0]
    tq = 2 * tk
    nq = seq // tq
    qo = pl.BlockSpec((tq, heads * LANES), lambda b, h, i: (b * nq + i, h))
    k = pl.BlockSpec((seq, heads * LANES), lambda b, h, i: (b, h))
    vt = pl.BlockSpec((seq // tk, heads * LANES, tk), lambda b, h, i: (b, h, 0))
    return pl.pallas_call(
        functools.partial(_mla_kernel, tq=tq, tk=tk, heads=heads),
        grid=(batch, MLA_HEADS // heads, nq),
        in_specs=[qo, k, vt],
        out_specs=qo,
        out_shape=jax.ShapeDtypeStruct((T, MLA_PAD_W), BF16),
        scratch_shapes=[pltpu.VMEM((heads, tk, tq), F32), pltpu.VMEM((heads, tk, tq), F32),
                        pltpu.VMEM((heads, 1, tq), F32), pltpu.VMEM((heads, LANES, tq), F32)],
        compiler_params=_params("parallel", "parallel", "arbitrary"),
        name="mla",
    )(mq, mk, mvt)


def _group_partner(x, d, width, lane):
    return jnp.where((lane % width) + d < width,
                     pltpu.roll(x, LANES - d, 1), pltpu.roll(x, width - d, 1))


def _router_gates(logits, bias):
    lane = lax.broadcasted_iota(jnp.int32, logits.shape, 1)
    is_expert = lane < N_EXPERTS
    scores = jax.nn.sigmoid(logits)
    biased = jnp.where(is_expert, scores + bias, NEG_INF)
    G = EXPERTS_PER_GROUP
    member = lane % G
    rank = jnp.zeros(logits.shape, jnp.int32)
    for d in range(1, G):
        other = _group_partner(biased, d, G, lane)
        other_first = (member + d) % G < member
        rank = rank + ((other > biased) | ((other == biased) & other_first)).astype(jnp.int32)
    top2 = rank < 2
    grp_score = jnp.where(top2, biased, 0.0)
    part = grp_score
    for d in range(1, G):
        grp_score = grp_score + _group_partner(part, d, G, lane)
    beaten = jnp.zeros(logits.shape, jnp.bool_)
    grp = lane // G
    for dg in range(1, N_GROUPS):
        other = _group_partner(grp_score, dg * G, N_EXPERTS, lane)
        other_first = (grp + dg) % N_GROUPS < grp
        beaten = beaten | (other > grp_score) | ((other == grp_score) & other_first)
    sel = is_expert & top2 & jnp.logical_not(beaten)
    picked = jnp.where(sel, scores, 0.0)
    return picked / jnp.sum(picked, -1, keepdims=True)


def _outproj_kernel(ret_ref, swa_ref, mla_ref, x_ref, wr_ref, ws_ref, wm_ref, g_ref, b_ref,
                    rw_ref, rb_ref, x1_ref, x1b_ref, gates_ref, *, alpha, n_sub):
    tm = x_ref.shape[0]
    subs = [slice(s * tm // n_sub, (s + 1) * tm // n_sub) for s in range(n_sub)]

    def mix(rows):
        return (_dot(ret_ref[rows, :], wr_ref[...]) + _dot(swa_ref[rows, :], ws_ref[...])
                + _dot(mla_ref[rows, :], wm_ref[...]))

    def finish(rows, mixed):
        x1 = _layer_norm(alpha * x_ref[rows, :] + mixed, g_ref[...], b_ref[...])
        x1_ref[rows, :] = x1
        hi = x1.astype(BF16)
        x1b_ref[rows, :] = hi
        lo = (x1 - hi.astype(F32)).astype(BF16)
        hi_part = _dot(hi, rw_ref[...])
        logits = hi_part[:, :LANES] + hi_part[:, LANES:] + _dot(lo, rw_ref[:, :LANES])
        gates_ref[rows, :] = _router_gates(logits, rb_ref[...])

    mixed = mix(subs[0])
    for s in range(n_sub):
        mixed_next = mix(subs[s + 1]) if s + 1 < n_sub else None
        finish(subs[s], mixed)
        mixed = mixed_next


def _outproj(ret_o, swa_o, mla_o, x, wr, ws, wm, g, b, rw, rb, *, layer, alpha, tm=512, n_sub=4):
    T = x.shape[0]
    row = lambda w: pl.BlockSpec((tm, w), lambda i: (i, 0))
    full = lambda a: pl.BlockSpec(a.shape, lambda i: (0, 0))
    per_layer = lambda a: _layer_spec(a, layer)
    return pl.pallas_call(
        functools.partial(_outproj_kernel, alpha=alpha, n_sub=n_sub),
        grid=(T // tm,),
        in_specs=[row(RET_W), row(SWA_W), row(MLA_PAD_W), row(D_MODEL), per_layer(wr), per_layer(ws),
                  per_layer(wm), per_layer(g), per_layer(b), full(rw), full(rb)],
        out_specs=[row(D_MODEL), row(D_MODEL), row(LANES)],
        out_shape=[jax.ShapeDtypeStruct((T, D_MODEL), F32), jax.ShapeDtypeStruct((T, D_MODEL), BF16),
                   jax.ShapeDtypeStruct((T, LANES), F32)],
        compiler_params=_params("parallel"),
        name="outproj",
    )(ret_o, swa_o, mla_o, x, wr, ws, wm, g, b, rw, rb)


MOE_CHUNK = 128
MOE_SEG_ALIGN = 16


def _moe_rows(tm):
    return -(-(2 * tm + N_EXPERTS * (MOE_SEG_ALIGN - 1) + MOE_CHUNK) // LANES) * LANES


def _moe_kernel(xb_ref, x_ref, gates_ref, wgu_ref, wd_ref, g_ref, b_ref, o_ref,
                xs_ref, ys_ref, start_ref, chunks_ref, *, alpha):
    tm = xb_ref.shape[0]
    R = xs_ref.shape[0]
    gates = gates_ref[...]
    big = float(4 * R)

    def strict_lower(n):
        return lax.broadcasted_iota(jnp.int32, (n, n), 1) < lax.broadcasted_iota(jnp.int32, (n, n), 0)

    def strict_upper(n):
        return lax.broadcasted_iota(jnp.int32, (n, n), 0) < lax.broadcasted_iota(jnp.int32, (n, n), 1)

    def seg_pad(count):
        return jnp.floor((count + (MOE_SEG_ALIGN - 1.0)) * (1.0 / MOE_SEG_ALIGN)) * MOE_SEG_ALIGN

    one = lambda mask: jnp.where(mask, 1.0, 0.0).astype(BF16)

    sel = (gates > 0.0) & (lax.broadcasted_iota(jnp.int32, gates.shape, 1) < N_EXPERTS)
    rank = _dot(one(strict_lower(tm)), one(sel))
    count = jnp.sum(jnp.where(sel, 1.0, 0.0), 0, keepdims=True)
    pad = jnp.broadcast_to(seg_pad(count), (8, LANES)).astype(BF16)
    offset = _dot(pad, one(strict_upper(LANES)))[0:1]
    slot = offset + rank
    slot_a = jnp.min(jnp.where(sel, slot, big), -1, keepdims=True)
    slot_b = jnp.max(jnp.where(sel, slot, -1.0), -1, keepdims=True)
    gate_a = jnp.sum(jnp.where(sel & (slot == slot_a), gates, 0.0), -1, keepdims=True)
    gate_b = jnp.sum(jnp.where(sel & (slot == slot_b), gates, 0.0), -1, keepdims=True)

    gates_t = gates.T
    sel_t = (gates_t > 0.0) & (lax.broadcasted_iota(jnp.int32, gates_t.shape, 0) < N_EXPERTS)
    rank_t = _dot(one(sel_t), one(strict_upper(tm)))
    count_t = jnp.sum(jnp.where(sel_t, 1.0, 0.0), -1, keepdims=True)
    pad_t = jnp.broadcast_to(seg_pad(count_t), (LANES, LANES)).astype(BF16)
    slot_t = _dot(one(strict_lower(LANES)), pad_t)[:, 0:1] + rank_t
    slot_a_t = jnp.min(jnp.where(sel_t, slot_t, big), 0, keepdims=True)
    slot_b_t = jnp.max(jnp.where(sel_t, slot_t, -1.0), 0, keepdims=True)

    row_id = lax.broadcasted_iota(jnp.int32, (R, tm), 0).astype(F32)
    xs_ref[...] = _dot(one((row_id == slot_a_t) | (row_id == slot_b_t)), xb_ref[...]).astype(BF16)

    @pl.when(pl.program_id(0) == 0)
    def _():
        ys_ref[...] = jnp.zeros_like(ys_ref)

    offset_i = offset.astype(jnp.int32)
    chunks_i = jnp.floor((count + (MOE_CHUNK - 1.0)) * (1.0 / MOE_CHUNK)).astype(jnp.int32)
    for e in range(N_EXPERTS):
        start_ref[e] = offset_i[0, e]
        chunks_ref[e] = chunks_i[0, e]

    def chunk_rows(e, c):
        return pl.ds(pl.multiple_of(start_ref[e] + c * MOE_CHUNK, MOE_SEG_ALIGN), MOE_CHUNK)

    def activation(e, rows):
        h = _dot(xs_ref[rows, :], wgu_ref[e])
        up, lin = h[:, :D_EXPERT], h[:, D_EXPERT:]
        return (up * jax.nn.sigmoid(up) * lin).astype(BF16)

    def project(e, rows, a):
        ys_ref[rows, :] = _dot(a, wd_ref[e]).astype(BF16)

    def later_chunks(e, _):
        def body(c, _):
            rows = chunk_rows(e, c)
            project(e, rows, activation(e, rows))
            return 0
        return lax.fori_loop(1, chunks_ref[e], body, 0)

    lax.fori_loop(0, N_EXPERTS, later_chunks, 0)

    a = activation(0, chunk_rows(0, 0))
    for e in range(N_EXPERTS):
        a_next = activation(e + 1, chunk_rows(e + 1, 0)) if e + 1 < N_EXPERTS else None
        project(e, chunk_rows(e, 0), a)
        a = a_next

    col_id = lax.broadcasted_iota(jnp.int32, (tm, R), 1).astype(F32)
    unsort = jnp.where(col_id == slot_a, gate_a, jnp.where(col_id == slot_b, gate_b, 0.0)).astype(BF16)
    o_ref[...] = _layer_norm(alpha * x_ref[...] + _dot(unsort, ys_ref[...]), g_ref[...], b_ref[...])


def _moe(x1b, x1, gates, wgu, wd, g, b, *, layer, alpha, tm=512):
    T = x1.shape[0]
    R = _moe_rows(tm)
    row = lambda w: pl.BlockSpec((tm, w), lambda i: (i, 0))
    full = lambda a: _layer_spec(a, layer)
    resident = lambda a: _layer_spec(a, layer, pipeline_mode=pl.Buffered(1))
    return pl.pallas_call(
        functools.partial(_moe_kernel, alpha=alpha),
        grid=(T // tm,),
        in_specs=[row(D_MODEL), row(D_MODEL), row(LANES), resident(wgu), resident(wd), full(g), full(b)],
        out_specs=row(D_MODEL),
        out_shape=jax.ShapeDtypeStruct((T, D_MODEL), F32),
        scratch_shapes=[pltpu.VMEM((R, D_MODEL), BF16), pltpu.VMEM((R, D_MODEL), BF16),
                        pltpu.SMEM((N_EXPERTS,), jnp.int32), pltpu.SMEM((N_EXPERTS,), jnp.int32)],
        compiler_params=_params("arbitrary"),
        name="moe",
    )(x1b, x1, gates, wgu, wd, g, b)


def _rope_tables(seq):
    pos = np.arange(seq, dtype=np.float64)[:, None]
    lane = np.arange(LANES)

    def table(dim, active, offset):
        j = (lane - offset) % dim
        inv = ROPE_THETA ** (-(2.0 * (j % (dim // 2))) / dim)
        ang = pos * inv[None, :]
        sign = np.where(j < dim // 2, -1.0, 1.0)
        cos = np.where(active[None, :], np.cos(ang), 1.0)
        sin = np.where(active[None, :], np.sin(ang) * sign[None, :], 0.0)
        return jnp.asarray(cos, F32), jnp.asarray(sin, F32)

    cr, sr = table(RET_DK, np.ones(LANES, bool), 0)
    cm, sm = table(MLA_ROPE, (lane >= MLA_NOPE) & (lane < MLA_NOPE + MLA_ROPE), MLA_NOPE)
    return cr, sr, cm, sm


def _stacked_weights(w_in, mla_w_uq, mla_w_ukv, w_out):
    depth = w_in.shape[0]
    sizes = (256, 256, 256, 256, SWA_W, SWA_KV_W, SWA_KV_W, MLA_Q_RANK, MLA_KV_RANK, MLA_ROPE)
    o = [int(v) for v in np.concatenate([[0], np.cumsum(sizes)])]
    order = np.array(_SWA_HEAD_ORDER)
    pad_lanes = lambda a: jnp.pad(a, [(0, 0)] * (a.ndim - 1) + [(0, LANES - a.shape[-1])])

    sq = w_in[:, :, o[4]:o[5]].reshape(depth, D_MODEL, SWA_Q_HEADS, HEAD_DIM)[:, :, order]
    kr = jnp.pad(w_in[:, :, o[9]:o[10]], ((0, 0), (0, 0), (MLA_NOPE, LANES - MLA_NOPE - MLA_ROPE)))
    w1 = jnp.concatenate([w_in[:, :, :o[4]], sq.reshape(depth, D_MODEL, SWA_W), kr, w_in[:, :, o[5]:o[7]],
                          w_in[:, :, o[8]:o[9]], w_in[:, :, o[7]:o[8]]], 2).astype(BF16)

    dq = MLA_NOPE + MLA_ROPE
    wuq = pad_lanes(mla_w_uq.reshape(depth, MLA_Q_RANK, MLA_HEADS, dq)).reshape(depth, MLA_Q_RANK, MLA_PAD_W)
    ukv = mla_w_ukv.reshape(depth, MLA_KV_RANK, MLA_HEADS, MLA_NOPE + MLA_V)
    wuk = pad_lanes(ukv[..., :MLA_NOPE]).reshape(depth, MLA_KV_RANK, MLA_PAD_W)
    wuvt = jnp.swapaxes(pad_lanes(ukv[..., MLA_NOPE:]).reshape(depth, MLA_KV_RANK, MLA_PAD_W), 1, 2)

    wr = w_out[:, :RET_W]
    ws = w_out[:, RET_W:RET_W + SWA_W].reshape(depth, SWA_Q_HEADS, HEAD_DIM, D_MODEL)[:, order]
    wm = w_out[:, RET_W + SWA_W:].reshape(depth, MLA_HEADS, MLA_V, D_MODEL)
    wm = jnp.pad(wm, ((0, 0), (0, 0), (0, LANES - MLA_V), (0, 0)))
    to_bf16 = lambda a: a.astype(BF16)
    return (w1, to_bf16(wuq), to_bf16(wuk), to_bf16(wuvt), to_bf16(wr),
            to_bf16(ws.reshape(depth, SWA_W, D_MODEL)), to_bf16(wm.reshape(depth, MLA_PAD_W, D_MODEL)))


def kernel(x, w_in, ret_gn_w, swa_sinks, mla_q_norm_w, mla_kv_norm_w, mla_w_uq, mla_w_ukv, w_out,
           ln1_g, ln1_b, router_w, router_bias, exp_w_gate_up, exp_w_down, ln2_g, ln2_b):
    batch, seq, _ = x.shape
    depth = w_in.shape[0]
    alpha = (2 * depth) ** 0.25
    cr, sr, cm, sm = _rope_tables(seq)

    rw = jnp.pad(router_w, ((0, 0), (0, LANES - N_EXPERTS)))
    rwh = rw.astype(BF16)
    rw = jnp.concatenate([rwh, (rw - rwh.astype(F32)).astype(BF16)], 1)
    rb = jnp.pad(router_bias.astype(F32), (0, LANES - N_EXPERTS)).reshape(1, LANES)

    w1, wuq, wuk, wuvt, wr, ws, wm = _stacked_weights(w_in, mla_w_uq, mla_w_ukv, w_out)
    wgu, wd = exp_w_gate_up.astype(BF16), exp_w_down.astype(BF16)
    rows = lambda a: a.astype(F32).reshape(depth, 1, -1)
    qnw, kvnw, gnw = rows(mla_q_norm_w), rows(mla_kv_norm_w), rows(ret_gn_w)
    g1, b1, g2, b2 = rows(ln1_g), rows(ln1_b), rows(ln2_g), rows(ln2_b)
    sink_row = rows(jnp.repeat(swa_sinks, SWA_WINDOW, axis=1))

    t = x.reshape(batch * seq, D_MODEL)
    for l in range(depth):
        rq, rk, rv, rg, sq, sk, mq, mk, svt, mvt = _inproj(
            t, w1, cr, sr, cm, sm, qnw, kvnw, wuq, wuk, wuvt, layer=l, seq=seq, tm=MLA_TILE)
        ret_o = _retention(rq, rk, rv, rg, gnw, layer=l, batch=batch, seq=seq)
        swa_o = _swa(sq, sk, svt, sink_row, layer=l, seq=seq, blk=MLA_TILE)
        mla_o = _mla(mq, mk, mvt, batch=batch, seq=seq, tk=MLA_TILE)
        x1, x1b, gates = _outproj(ret_o, swa_o, mla_o, t, wr, ws, wm, g1, b1, rw, rb, layer=l, alpha=alpha)
        t = _moe(x1b, x1, gates, wgu, wd, g2, b2, layer=l, alpha=alpha)
    return t.reshape(batch, seq, D_MODEL)
```

```python
import functools
import math

import numpy as np
import jax
import jax.numpy as jnp
from jax import lax
from jax.experimental import pallas as pl
from jax.experimental.pallas import tpu as pltpu

D_MODEL = 1024
HEAD_DIM = 64
ROPE_THETA = 10000.0

RET_HEADS = 4
RET_DK = 64
RET_DV = 64
RET_W = RET_HEADS * RET_DV

SWA_Q_HEADS = 6
SWA_KV_HEADS = 2
SWA_WINDOW = 128
SWA_W = SWA_Q_HEADS * HEAD_DIM
SWA_KV_W = SWA_KV_HEADS * HEAD_DIM

MLA_HEADS = 6
MLA_Q_RANK = 384
MLA_KV_RANK = 256
MLA_NOPE = 64
MLA_ROPE = 32
MLA_V = 64

N_EXPERTS = 16
N_GROUPS = 4
EXPERTS_PER_GROUP = N_EXPERTS // N_GROUPS
D_EXPERT = 256

LN_EPS = 1e-5
RMS_EPS = 1e-6
NEG_INF = -1e30

LANES = 128
MLA_PAD_W = MLA_HEADS * LANES
MLA_TILE = 512
VMEM_LIMIT = 56 * 1024 * 1024

F32 = jnp.float32
BF16 = jnp.bfloat16

_C_RQ, _C_RK, _C_RV, _C_RG = 0, 256, 512, 768
_C_SQ, _C_KR, _C_SK, _C_SV = 1024, 1408, 1536, 1664
_C_CKV, _C_CQ, _C_END = 1792, 2048, 2432

_SWA_HEAD_ORDER = (0, 3, 1, 4, 2, 5)


def _params(*sem, flags=None):
    return pltpu.CompilerParams(dimension_semantics=sem, vmem_limit_bytes=VMEM_LIMIT, flags=flags)


def _layer_spec(a, layer, **kwargs):
    return pl.BlockSpec((None,) + a.shape[1:], lambda *_: (layer,) + (0,) * (a.ndim - 1), **kwargs)


def _dot(a, b):
    return jnp.dot(a, b, preferred_element_type=F32)


def _dot_nt(a, b):
    return lax.dot_general(a, b, (((1,), (1,)), ((), ())), preferred_element_type=F32)


def _dot_tn(a, b):
    return lax.dot_general(a, b, (((0,), (0,)), ((), ())), preferred_element_type=F32)


def _split_dot(x, w_bf16):
    hi = x.astype(BF16)
    lo = (x - hi.astype(F32)).astype(BF16)
    return _dot(hi, w_bf16) + _dot(lo, w_bf16)


def _rope(h, cos, sin_signed, half):
    fwd = pltpu.roll(h, LANES - half, 1)
    bwd = pltpu.roll(h, half, 1)
    lane = lax.broadcasted_iota(jnp.int32, h.shape, 1)
    rot = jnp.where((lane % (2 * half)) < half, fwd, bwd)
    return h * cos + rot * sin_signed


def _layer_norm(y, g, b):
    mu = jnp.mean(y, -1, keepdims=True)
    d = y - mu
    var = jnp.mean(d * d, -1, keepdims=True)
    return d * lax.rsqrt(var + LN_EPS) * g + b


def _rms_norm(x, g):
    return x * lax.rsqrt(jnp.mean(x * x, -1, keepdims=True) + RMS_EPS) * g


def _inproj_kernel(x_ref, w1_ref, cr_ref, sr_ref, cm_ref, sm_ref, qnw_ref, kvnw_ref,
                   wuq_ref, wuk_ref, wuvt_ref,
                   rq_ref, rk_ref, rv_ref, rg_ref, sq_ref, sk_ref, mq_ref, mk_ref, svt_ref, mvt_ref,
                   *, mla_q_scale):
    xb = x_ref[...].astype(BF16)

    def proj(a, b):
        return _dot(xb, w1_ref[:, a:b])

    cq = proj(_C_CQ, _C_END)
    ckv = proj(_C_CKV, _C_CQ)
    rq, rk = proj(_C_RQ, _C_RK), proj(_C_RK, _C_RV)
    rv, rg = proj(_C_RV, _C_RG), proj(_C_RG, _C_SQ)
    sq_kr = proj(_C_SQ, _C_SK)
    sk_sv = proj(_C_SK, _C_CKV)
    cq = _rms_norm(cq, qnw_ref[...]).astype(BF16)
    ckv = _rms_norm(ckv, kvnw_ref[...]).astype(BF16)
    q = _dot(cq, wuq_ref[...])
    kn = _dot(ckv, wuk_ref[...])
    vt = _dot_nt(wuvt_ref[...], ckv)

    cr, sr = cr_ref[...], sr_ref[...]
    for g in range(RET_W // LANES):
        lanes = slice(g * LANES, (g + 1) * LANES)
        rq_ref[:, lanes] = _rope(rq[:, lanes], cr, sr, RET_DK // 2).astype(BF16)
        rk_ref[:, lanes] = (_rope(rk[:, lanes], cr, sr, RET_DK // 2) * (RET_DK ** -0.5)).astype(BF16)
    rv_ref[...] = rv.astype(BF16)
    rg_ref[...] = rg.astype(BF16)
    sq_ref[...] = (sq_kr[:, :SWA_W] * (HEAD_DIM ** -0.5)).astype(BF16)
    sk_ref[...] = sk_sv[:, :SWA_KV_W].astype(BF16)
    svt_ref[0] = sk_sv[:, SWA_KV_W:].T.astype(BF16)

    cm, sm = cm_ref[...], sm_ref[...]
    kpe = _rope(sq_kr[:, SWA_W:], cm, sm, MLA_ROPE // 2)
    for h in range(MLA_HEADS):
        lo, hi = h * LANES, (h + 1) * LANES
        mq_ref[:, lo:hi] = (_rope(q[:, lo:hi], cm, sm, MLA_ROPE // 2) * mla_q_scale).astype(BF16)
        mk_ref[:, lo:hi] = (kn[:, lo:hi] + kpe).astype(BF16)
    feat = lax.broadcasted_iota(jnp.int32, vt.shape, 0) % LANES
    mvt_ref[0] = (vt + (feat == MLA_V).astype(F32)).astype(BF16)


def _inproj(x, w1, cr, sr, cm, sm, qnw, kvnw, wuq, wuk, wuvt, *, layer, seq, tm):
    T = x.shape[0]
    nt = T // tm
    npos = seq // tm
    row = lambda w: pl.BlockSpec((tm, w), lambda i: (i, 0))
    pos = lambda: pl.BlockSpec((tm, LANES), lambda i: (i % npos, 0))
    full = lambda a: _layer_spec(a, layer)
    widths = (RET_W, RET_W, RET_W, RET_W, SWA_W, SWA_KV_W, MLA_PAD_W, MLA_PAD_W)
    widths_t = (SWA_KV_W, MLA_PAD_W)
    mla_q_scale = (MLA_NOPE + MLA_ROPE) ** -0.5 * math.log2(math.e)
    return pl.pallas_call(
        functools.partial(_inproj_kernel, mla_q_scale=mla_q_scale),
        grid=(nt,),
        in_specs=[row(D_MODEL), full(w1), pos(), pos(), pos(), pos(), full(qnw), full(kvnw),
                  full(wuq), full(wuk), full(wuvt)],
        out_specs=[row(w) for w in widths] + [pl.BlockSpec((1, w, tm), lambda i: (i, 0, 0)) for w in widths_t],
        out_shape=[jax.ShapeDtypeStruct((T, w), BF16) for w in widths]
        + [jax.ShapeDtypeStruct((nt, w, tm), BF16) for w in widths_t],
        compiler_params=_params("parallel"),
        name="inproj",
    )(x, w1, cr, sr, cm, sm, qnw, kvnw, wuq, wuk, wuvt)


def _retention_kernel(q_ref, k_ref, v_ref, g_ref, gnw_ref, dec_ref, xi_ref, zeta_ref, dm_ref, bm_ref,
                      avg_ref, o_ref, state_ref, *, chunk, n_chunks):
    @pl.when(pl.program_id(1) == 0)
    def _():
        state_ref[...] = jnp.zeros_like(state_ref)

    lane_head = lax.broadcasted_iota(jnp.int32, (chunk, RET_W), 1) // RET_DV
    avg = avg_ref[...]
    chunks = [slice(c * chunk, (c + 1) * chunk) for c in range(n_chunks)]

    def raw_scores(rows):
        q = q_ref[rows, :]
        zero = jnp.zeros_like(q)
        q_heads = jnp.concatenate([jnp.where(lane_head == h, q, zero) for h in range(RET_HEADS)], 0)
        return _dot_nt(q_heads, k_ref[rows, :])

    scores = [raw_scores(rows) for rows in chunks]
    updates = [_dot_tn((k_ref[rows, :].astype(F32) * zeta_ref[...]).astype(BF16), v_ref[rows, :])
               for rows in chunks]
    inner = [_dot((s * dec_ref[...]).astype(BF16), v_ref[rows, :]) for s, rows in zip(scores, chunks)]

    state = state_ref[...]
    outs = []
    for rows, upd, inner_heads in zip(chunks, updates, inner):
        o = _dot((q_ref[rows, :].astype(F32) * xi_ref[...]).astype(BF16), state.astype(BF16))
        for h in range(RET_HEADS):
            o = o + jnp.where(lane_head == h, inner_heads[h * chunk:(h + 1) * chunk], 0.0)
        outs.append(o)
        state = dm_ref[...] * state + bm_ref[...] * upd
    state_ref[...] = state

    mus = [_split_dot(o, avg) for o in outs]
    devs = [o - mu for o, mu in zip(outs, mus)]
    variances = [_split_dot(d * d, avg) for d in devs]
    for rows, d, var in zip(chunks, devs, variances):
        on = d * lax.rsqrt(var + LN_EPS) * gnw_ref[...]
        gate = g_ref[rows, :].astype(F32)
        o_ref[rows, :] = (on * (gate * jax.nn.sigmoid(gate))).astype(BF16)


def _retention_tables(chunk):
    H = RET_HEADS
    gamma = 1.0 - 2.0 ** (-5.0 - np.arange(H, dtype=np.float64))
    log_g = np.log(gamma)
    idx = np.arange(chunk, dtype=np.float64)
    rel = idx[:, None] - idx[None, :]
    dec = np.where(rel[None] >= 0, np.exp(np.maximum(rel, 0.0)[None] * log_g[:, None, None]), 0.0)
    dec = dec.reshape(H * chunk, chunk)
    lane_head = np.arange(RET_W) // RET_DV
    xi = np.exp((idx[:, None] + 1.0) * log_g[lane_head][None, :])
    zeta = np.exp((chunk - 1.0 - idx[:, None]) * log_g[lane_head][None, :])
    same = lane_head[:, None] == lane_head[None, :]
    dm = np.where(same, np.exp(chunk * log_g)[lane_head][:, None], 0.0)
    bm = same.astype(np.float64)
    avg = bm / RET_DV
    f = lambda a: jnp.asarray(a, F32)
    return f(dec), f(xi), f(zeta), f(dm), f(bm), jnp.asarray(avg, BF16)


def _retention(rq, rk, rv, rg, gnw, *, layer, batch, seq, blk=512, chunk=128):
    T = rq.shape[0]
    nb = seq // blk
    tables = _retention_tables(chunk)
    row = pl.BlockSpec((blk, RET_W), lambda b, i: (b * nb + i, 0))
    full = lambda a: pl.BlockSpec(a.shape, lambda b, i: (0, 0))
    return pl.pallas_call(
        functools.partial(_retention_kernel, chunk=chunk, n_chunks=blk // chunk),
        grid=(batch, nb),
        in_specs=[row, row, row, row, _layer_spec(gnw, layer)] + [full(t) for t in tables],
        out_specs=row,
        out_shape=jax.ShapeDtypeStruct((T, RET_W), BF16),
        scratch_shapes=[pltpu.VMEM((RET_W, RET_W), F32)],
        compiler_params=_params("parallel", "arbitrary"),
        name="retention",
    )(rq, rk, rv, rg, gnw, *tables)


def _swa_kernel(q_ref, kc_ref, kp_ref, vtc_ref, vtp_ref, sink_ref, o_ref, *, blk, seq):
    L = SWA_WINDOW
    n_pairs = SWA_Q_HEADS // 2
    n_sub = blk // L
    lane_lo = lax.broadcasted_iota(jnp.int32, (L, LANES), 1) < HEAD_DIM
    feat_lo = lax.broadcasted_iota(jnp.int32, (LANES, L), 0) < HEAD_DIM
    key = lax.broadcasted_iota(jnp.int32, (2 * L, SWA_Q_HEADS * L), 0)
    qry = lax.broadcasted_iota(jnp.int32, (2 * L, SWA_Q_HEADS * L), 1) % L
    band = (key > qry) & (key <= qry + L)
    sink = sink_ref[...]

    def scores(sb):
        r0 = sb * L
        k_prev = kp_ref[...] if sb == 0 else kc_ref[r0 - L:r0, :]
        k_win = jnp.concatenate([k_prev, kc_ref[r0:r0 + L, :]], 0)
        pairs = [q_ref[r0:r0 + L, g * LANES:(g + 1) * LANES] for g in range(n_pairs)]
        zero = jnp.zeros_like(pairs[0])
        q_heads = jnp.concatenate([jnp.where(lane_lo, p, zero) for p in pairs]
                                  + [jnp.where(lane_lo, zero, p) for p in pairs], 0)
        return _dot_nt(k_win, q_heads)

    def attend(sb, st):
        r0 = sb * L
        pos0 = (pl.program_id(0) * blk + r0) % seq
        first_key = jnp.where(pos0 > 0, 0, L)
        st = jnp.where(band & (key >= first_key), st, NEG_INF)
        m = jnp.maximum(jnp.max(st, 0, keepdims=True), sink)
        p = jnp.exp(st - m)
        denom = jnp.sum(p, 0, keepdims=True) + jnp.exp(sink - m)
        if sb == 0:
            vt_win = jnp.concatenate([vtp_ref[0], vtc_ref[0, :, :L]], 1)
        else:
            vt_win = vtc_ref[0, :, r0 - L:r0 + L]
        ot = _dot(vt_win, (p / denom).astype(BF16))
        for g in range(n_pairs):
            o_lo = ot[:, g * L:(g + 1) * L]
            o_hi = ot[:, (g + n_pairs) * L:(g + n_pairs + 1) * L]
            o_ref[r0:r0 + L, g * LANES:(g + 1) * LANES] = jnp.where(feat_lo, o_lo, o_hi).T.astype(BF16)

    st = scores(0)
    for sb in range(n_sub):
        st_next = scores(sb + 1) if sb + 1 < n_sub else None
        attend(sb, st)
        st = st_next


def _swa(sq, sk, svt, sink_row, *, layer, seq, blk):
    T = sq.shape[0]
    L = SWA_WINDOW
    per = blk // L
    cur = lambda w: pl.BlockSpec((blk, w), lambda i: (i, 0))
    k_prev = pl.BlockSpec((L, SWA_KV_W), lambda i: (jnp.maximum(i * per - 1, 0), 0))
    vt_cur = pl.BlockSpec((1, SWA_KV_W, blk), lambda i: (i, 0, 0))
    vt_prev = pl.BlockSpec((1, SWA_KV_W, L), lambda i: (jnp.maximum(i - 1, 0), 0, per - 1))
    return pl.pallas_call(
        functools.partial(_swa_kernel, blk=blk, seq=seq),
        grid=(T // blk,),
        in_specs=[cur(SWA_W), cur(SWA_KV_W), k_prev, vt_cur, vt_prev, _layer_spec(sink_row, layer)],
        out_specs=cur(SWA_W),
        out_shape=jax.ShapeDtypeStruct((T, SWA_W), BF16),
        compiler_params=_params("parallel"),
        name="swa",
    )(sq, sk, sk, svt, svt, sink_row)


def _mla_kernel(q_ref, k_ref, vt_ref, o_ref, st0_ref, st1_ref, m_ref, acc_ref, *, tq, tk, heads):
    i = pl.program_id(2)
    head_lanes = [slice(h * LANES, (h + 1) * LANES) for h in range(heads)]
    st_refs = (st0_ref, st1_ref)
    all_q = slice(0, tq)
    late_q = slice(tk, tq)

    def scores(j, lanes, qs):
        start = pl.multiple_of(j * tk, tk)
        return _dot_nt(k_ref[pl.ds(start, tk), lanes], q_ref[qs, lanes])

    def step(j, cur, qs=all_q, diagonal=False, next_qs=all_q):
        for h, lanes in enumerate(head_lanes):
            if next_qs is not None:
                st_refs[1 - cur][h, :, next_qs] = scores(j + 1, lanes, next_qs)
            st = st_refs[cur][h, :, qs]
            if diagonal:
                key = lax.broadcasted_iota(jnp.int32, st.shape, 0)
                qry = lax.broadcasted_iota(jnp.int32, st.shape, 1)
                st = jnp.where(key <= qry, st, NEG_INF)
            m = m_ref[h][:, qs]
            m_new = jnp.maximum(m, jnp.max(st, 0, keepdims=True))
            pt = jnp.exp2(st - m_new).astype(BF16)
            acc_ref[h, :, qs] = jnp.exp2(m - m_new) * acc_ref[h, :, qs] + _dot(vt_ref[j, lanes, :], pt)
            if qs is all_q:
                m_ref[h] = m_new

    m_ref[...] = jnp.full(m_ref.shape, NEG_INF, F32)
    acc_ref[...] = jnp.zeros(acc_ref.shape, F32)
    for h, lanes in enumerate(head_lanes):
        st0_ref[h] = scores(0, lanes, all_q)

    def pair(p, _):
        step(2 * p, 0)
        step(2 * p + 1, 1)
        return 0

    lax.fori_loop(0, i, pair, 0)
    step(2 * i, 0, diagonal=True, next_qs=late_q)
    step(2 * i + 1, 1, qs=late_q, diagonal=True, next_qs=None)
    for h in range(heads):
        acc = acc_ref[h]
        o_ref[:, h * LANES:(h + 1) * LANES] = (acc / acc[MLA_V:MLA_V + 1, :]).T.astype(BF16)


def _mla(mq, mk, mvt, *, batch, seq, tk, heads=3):
    T = mq.shape[0]
    tq = 2 * tk
    nq = seq // tq
    qo = pl.BlockSpec((tq, heads * LANES), lambda b, h, i: (b * nq + i, h))
    k = pl.BlockSpec((seq, heads * LANES), lambda b, h, i: (b, h))
    vt = pl.BlockSpec((seq // tk, heads * LANES, tk), lambda b, h, i: (b, h, 0))
    return pl.pallas_call(
        functools.partial(_mla_kernel, tq=tq, tk=tk, heads=heads),
        grid=(batch, MLA_HEADS // heads, nq),
        in_specs=[qo, k, vt],
        out_specs=qo,
        out_shape=jax.ShapeDtypeStruct((T, MLA_PAD_W), BF16),
        scratch_shapes=[pltpu.VMEM((heads, tk, tq), F32), pltpu.VMEM((heads, tk, tq), F32),
                        pltpu.VMEM((heads, 1, tq), F32), pltpu.VMEM((heads, LANES, tq), F32)],
        compiler_params=_params("parallel", "parallel", "arbitrary"),
        name="mla",
    )(mq, mk, mvt)


def _outproj_kernel(ret_ref, swa_ref, mla_ref, x_ref, wr_ref, ws_ref, wm_ref, g_ref, b_ref,
                    rw_ref, x1_ref, x1b_ref, logits_ref, *, alpha, n_sub):
    tm = x_ref.shape[0]
    subs = [slice(s * tm // n_sub, (s + 1) * tm // n_sub) for s in range(n_sub)]

    def mix(rows):
        return (_dot(ret_ref[rows, :], wr_ref[...]) + _dot(swa_ref[rows, :], ws_ref[...])
                + _dot(mla_ref[rows, :], wm_ref[...]))

    def finish(rows, mixed):
        x1 = _layer_norm(alpha * x_ref[rows, :] + mixed, g_ref[...], b_ref[...])
        x1_ref[rows, :] = x1
        hi = x1.astype(BF16)
        x1b_ref[rows, :] = hi
        lo = (x1 - hi.astype(F32)).astype(BF16)
        hi_part = _dot(hi, rw_ref[...])
        logits_ref[rows, :] = hi_part[:, :LANES] + hi_part[:, LANES:] + _dot(lo, rw_ref[:, :LANES])

    mixed = mix(subs[0])
    for s in range(n_sub):
        mixed_next = mix(subs[s + 1]) if s + 1 < n_sub else None
        finish(subs[s], mixed)
        mixed = mixed_next


def _outproj(ret_o, swa_o, mla_o, x, wr, ws, wm, g, b, rw, *, layer, alpha, tm=512, n_sub=4):
    T = x.shape[0]
    row = lambda w: pl.BlockSpec((tm, w), lambda i: (i, 0))
    full = lambda a: pl.BlockSpec(a.shape, lambda i: (0, 0))
    per_layer = lambda a: _layer_spec(a, layer)
    return pl.pallas_call(
        functools.partial(_outproj_kernel, alpha=alpha, n_sub=n_sub),
        grid=(T // tm,),
        in_specs=[row(RET_W), row(SWA_W), row(MLA_PAD_W), row(D_MODEL), per_layer(wr), per_layer(ws),
                  per_layer(wm), per_layer(g), per_layer(b), full(rw)],
        out_specs=[row(D_MODEL), row(D_MODEL), row(LANES)],
        out_shape=[jax.ShapeDtypeStruct((T, D_MODEL), F32), jax.ShapeDtypeStruct((T, D_MODEL), BF16),
                   jax.ShapeDtypeStruct((T, LANES), F32)],
        compiler_params=_params("parallel"),
        name="outproj",
    )(ret_o, swa_o, mla_o, x, wr, ws, wm, g, b, rw)


MOE_CHUNK = 128
MOE_SEG_ALIGN = 16


def _moe_rows(tm):
    return -(-(2 * tm + N_EXPERTS * (MOE_SEG_ALIGN - 1) + MOE_CHUNK) // LANES) * LANES


def _ring_partner(x, d, width):
    n = x.shape[0]
    ahead = pltpu.roll(x, n - d, 0)
    if width == n:
        return ahead
    row = lax.broadcasted_iota(jnp.int32, x.shape, 0)
    return jnp.where((row % width) + d < width, ahead, pltpu.roll(x, width - d, 0))


def _router_gates_t(logits_t, bias_col):
    row = lax.broadcasted_iota(jnp.int32, logits_t.shape, 0)
    scores = jax.nn.sigmoid(logits_t)
    biased = scores + bias_col
    G = EXPERTS_PER_GROUP
    member = row % G
    rank = jnp.zeros(logits_t.shape, jnp.int32)
    for d in range(1, G):
        other = _ring_partner(biased, d, G)
        other_first = (member + d) % G < member
        rank = rank + ((other > biased) | ((other == biased) & other_first)).astype(jnp.int32)
    top2 = rank < 2
    part = jnp.where(top2, biased, 0.0)
    grp_score = part
    for d in range(1, G):
        grp_score = grp_score + _ring_partner(part, d, G)
    beaten = jnp.zeros(logits_t.shape, jnp.bool_)
    grp = row // G
    for dg in range(1, N_GROUPS):
        other = _ring_partner(grp_score, dg * G, N_EXPERTS)
        other_first = (grp + dg) % N_GROUPS < grp
        beaten = beaten | (other > grp_score) | ((other == grp_score) & other_first)
    picked = jnp.where(top2 & jnp.logical_not(beaten), scores, 0.0)
    return picked / jnp.sum(picked, 0, keepdims=True)


def _moe_kernel(xb_ref, x_ref, logits_ref, rb_ref, earlier_ref, wgu_ref, wd_ref, g_ref, b_ref, o_ref,
                xs_ref, ys_ref, start_ref, chunks_ref, *, alpha):
    tm = xb_ref.shape[0]
    R = xs_ref.shape[0]
    big = float(4 * R)
    experts_t = _router_gates_t(logits_ref[...].T[:N_EXPERTS], rb_ref[...])
    gates_t = jnp.concatenate([experts_t, jnp.zeros((LANES - N_EXPERTS, tm), F32)], 0)
    gates = gates_t.T

    def strict_upper(n):
        return lax.broadcasted_iota(jnp.int32, (n, n), 0) < lax.broadcasted_iota(jnp.int32, (n, n), 1)

    def seg_pad(count):
        return jnp.floor((count + (MOE_SEG_ALIGN - 1.0)) * (1.0 / MOE_SEG_ALIGN)) * MOE_SEG_ALIGN

    one = lambda mask: jnp.where(mask, 1.0, 0.0).astype(BF16)

    sel = (gates > 0.0) & (lax.broadcasted_iota(jnp.int32, gates.shape, 1) < N_EXPERTS)
    earlier = earlier_ref[...]
    rank = _dot(earlier, one(sel))
    count = jnp.sum(jnp.where(sel, 1.0, 0.0), 0, keepdims=True)
    pad = jnp.broadcast_to(seg_pad(count), (8, LANES)).astype(BF16)
    offset = _dot(pad, one(strict_upper(LANES)))[0:1]
    slot = offset + rank
    slot_a = jnp.min(jnp.where(sel, slot, big), -1, keepdims=True)
    slot_b = jnp.max(jnp.where(sel, slot, -1.0), -1, keepdims=True)
    gate_a = jnp.sum(jnp.where(sel & (slot == slot_a), gates, 0.0), -1, keepdims=True)
    gate_b = jnp.sum(jnp.where(sel & (slot == slot_b), gates, 0.0), -1, keepdims=True)

    sel_t = experts_t > 0.0
    rank_t = _dot_nt(one(sel_t), earlier)
    on_diag = (lax.broadcasted_iota(jnp.int32, (N_EXPERTS, LANES), 0)
               == lax.broadcasted_iota(jnp.int32, (N_EXPERTS, LANES), 1))
    offset_col = jnp.sum(jnp.where(on_diag, offset, 0.0), -1, keepdims=True)
    slot_t = offset_col + rank_t
    slot_a_t = jnp.min(jnp.where(sel_t, slot_t, big), 0, keepdims=True)
    slot_b_t = jnp.max(jnp.where(sel_t, slot_t, -1.0), 0, keepdims=True)

    row_id = lax.broadcasted_iota(jnp.int32, (R, tm), 0)
    sort = one((row_id == slot_a_t.astype(jnp.int32)) | (row_id == slot_b_t.astype(jnp.int32)))
    xs_ref[...] = _dot(sort, xb_ref[...]).astype(BF16)

    @pl.when(pl.program_id(0) == 0)
    def _():
        ys_ref[...] = jnp.zeros_like(ys_ref)

    offset_i = offset.astype(jnp.int32)
    chunks_i = jnp.floor((count + (MOE_CHUNK - 1.0)) * (1.0 / MOE_CHUNK)).astype(jnp.int32)
    for e in range(N_EXPERTS):
        start_ref[e] = offset_i[0, e]
        chunks_ref[e] = chunks_i[0, e]

    def chunk_rows(e, c):
        return pl.ds(pl.multiple_of(start_ref[e] + c * MOE_CHUNK, MOE_SEG_ALIGN), MOE_CHUNK)

    def activation(e, rows):
        h = _dot(xs_ref[rows, :], wgu_ref[e])
        up, lin = h[:, :D_EXPERT], h[:, D_EXPERT:]
        return (up * jax.nn.sigmoid(up) * lin).astype(BF16)

    def project(e, rows, a):
        ys_ref[rows, :] = _dot(a, wd_ref[e]).astype(BF16)

    def later_chunks(e, _):
        def body(c, _):
            rows = chunk_rows(e, c)
            project(e, rows, activation(e, rows))
            return 0
        return lax.fori_loop(1, chunks_ref[e], body, 0)

    lax.fori_loop(0, N_EXPERTS, later_chunks, 0)

    a = activation(0, chunk_rows(0, 0))
    for e in range(N_EXPERTS):
        a_next = activation(e + 1, chunk_rows(e + 1, 0)) if e + 1 < N_EXPERTS else None
        project(e, chunk_rows(e, 0), a)
        a = a_next

    col_id = lax.broadcasted_iota(jnp.int32, (tm, R), 1)
    unsort = jnp.where(col_id == slot_a.astype(jnp.int32), gate_a,
                       jnp.where(col_id == slot_b.astype(jnp.int32), gate_b, 0.0)).astype(BF16)
    o_ref[...] = _layer_norm(alpha * x_ref[...] + _dot(unsort, ys_ref[...]), g_ref[...], b_ref[...])


def _moe(x1b, x1, logits, rb, wgu, wd, g, b, *, layer, alpha, tm=512):
    T = x1.shape[0]
    R = _moe_rows(tm)
    earlier = jnp.asarray(np.tril(np.ones((tm, tm), np.float32), -1), BF16)
    row = lambda w: pl.BlockSpec((tm, w), lambda i: (i, 0))
    full = lambda a: _layer_spec(a, layer)
    resident = lambda a: _layer_spec(a, layer, pipeline_mode=pl.Buffered(1))
    return pl.pallas_call(
        functools.partial(_moe_kernel, alpha=alpha),
        grid=(T // tm,),
        in_specs=[row(D_MODEL), row(D_MODEL), row(LANES), pl.BlockSpec(rb.shape, lambda i: (0, 0)),
                  pl.BlockSpec(earlier.shape, lambda i: (0, 0)), resident(wgu), resident(wd), full(g), full(b)],
        out_specs=row(D_MODEL),
        out_shape=jax.ShapeDtypeStruct((T, D_MODEL), F32),
        scratch_shapes=[pltpu.VMEM((R, D_MODEL), BF16), pltpu.VMEM((R, D_MODEL), BF16),
                        pltpu.SMEM((N_EXPERTS,), jnp.int32), pltpu.SMEM((N_EXPERTS,), jnp.int32)],
        compiler_params=_params("arbitrary"),
        name="moe",
    )(x1b, x1, logits, rb, earlier, wgu, wd, g, b)


def _rope_tables(seq):
    pos = np.arange(seq, dtype=np.float64)[:, None]
    lane = np.arange(LANES)

    def table(dim, active, offset):
        j = (lane - offset) % dim
        inv = ROPE_THETA ** (-(2.0 * (j % (dim // 2))) / dim)
        ang = pos * inv[None, :]
        sign = np.where(j < dim // 2, -1.0, 1.0)
        cos = np.where(active[None, :], np.cos(ang), 1.0)
        sin = np.where(active[None, :], np.sin(ang) * sign[None, :], 0.0)
        return jnp.asarray(cos, F32), jnp.asarray(sin, F32)

    cr, sr = table(RET_DK, np.ones(LANES, bool), 0)
    cm, sm = table(MLA_ROPE, (lane >= MLA_NOPE) & (lane < MLA_NOPE + MLA_ROPE), MLA_NOPE)
    return cr, sr, cm, sm


def _stacked_weights(w_in, mla_w_uq, mla_w_ukv, w_out):
    depth = w_in.shape[0]
    sizes = (256, 256, 256, 256, SWA_W, SWA_KV_W, SWA_KV_W, MLA_Q_RANK, MLA_KV_RANK, MLA_ROPE)
    o = [int(v) for v in np.concatenate([[0], np.cumsum(sizes)])]
    order = np.array(_SWA_HEAD_ORDER)
    pad_lanes = lambda a: jnp.pad(a, [(0, 0)] * (a.ndim - 1) + [(0, LANES - a.shape[-1])])

    sq = w_in[:, :, o[4]:o[5]].reshape(depth, D_MODEL, SWA_Q_HEADS, HEAD_DIM)[:, :, order]
    kr = jnp.pad(w_in[:, :, o[9]:o[10]], ((0, 0), (0, 0), (MLA_NOPE, LANES - MLA_NOPE - MLA_ROPE)))
    w1 = jnp.concatenate([w_in[:, :, :o[4]], sq.reshape(depth, D_MODEL, SWA_W), kr, w_in[:, :, o[5]:o[7]],
                          w_in[:, :, o[8]:o[9]], w_in[:, :, o[7]:o[8]]], 2).astype(BF16)

    dq = MLA_NOPE + MLA_ROPE
    wuq = pad_lanes(mla_w_uq.reshape(depth, MLA_Q_RANK, MLA_HEADS, dq)).reshape(depth, MLA_Q_RANK, MLA_PAD_W)
    ukv = mla_w_ukv.reshape(depth, MLA_KV_RANK, MLA_HEADS, MLA_NOPE + MLA_V)
    wuk = pad_lanes(ukv[..., :MLA_NOPE]).reshape(depth, MLA_KV_RANK, MLA_PAD_W)
    wuvt = jnp.swapaxes(pad_lanes(ukv[..., MLA_NOPE:]).reshape(depth, MLA_KV_RANK, MLA_PAD_W), 1, 2)

    wr = w_out[:, :RET_W]
    ws = w_out[:, RET_W:RET_W + SWA_W].reshape(depth, SWA_Q_HEADS, HEAD_DIM, D_MODEL)[:, order]
    wm = w_out[:, RET_W + SWA_W:].reshape(depth, MLA_HEADS, MLA_V, D_MODEL)
    wm = jnp.pad(wm, ((0, 0), (0, 0), (0, LANES - MLA_V), (0, 0)))
    to_bf16 = lambda a: a.astype(BF16)
    return (w1, to_bf16(wuq), to_bf16(wuk), to_bf16(wuvt), to_bf16(wr),
            to_bf16(ws.reshape(depth, SWA_W, D_MODEL)), to_bf16(wm.reshape(depth, MLA_PAD_W, D_MODEL)))


def kernel(x, w_in, ret_gn_w, swa_sinks, mla_q_norm_w, mla_kv_norm_w, mla_w_uq, mla_w_ukv, w_out,
           ln1_g, ln1_b, router_w, router_bias, exp_w_gate_up, exp_w_down, ln2_g, ln2_b):
    batch, seq, _ = x.shape
    depth = w_in.shape[0]
    alpha = (2 * depth) ** 0.25
    cr, sr, cm, sm = _rope_tables(seq)

    rw = jnp.pad(router_w, ((0, 0), (0, LANES - N_EXPERTS)))
    rwh = rw.astype(BF16)
    rw = jnp.concatenate([rwh, (rw - rwh.astype(F32)).astype(BF16)], 1)
    rb = router_bias.astype(F32).reshape(N_EXPERTS, 1)

    w1, wuq, wuk, wuvt, wr, ws, wm = _stacked_weights(w_in, mla_w_uq, mla_w_ukv, w_out)
    wgu, wd = exp_w_gate_up.astype(BF16), exp_w_down.astype(BF16)
    rows = lambda a: a.astype(F32).reshape(depth, 1, -1)
    qnw, kvnw, gnw = rows(mla_q_norm_w), rows(mla_kv_norm_w), rows(ret_gn_w)
    g1, b1, g2, b2 = rows(ln1_g), rows(ln1_b), rows(ln2_g), rows(ln2_b)
    sink_row = rows(jnp.repeat(swa_sinks, SWA_WINDOW, axis=1))

    t = x.reshape(batch * seq, D_MODEL)
    for l in range(depth):
        rq, rk, rv, rg, sq, sk, mq, mk, svt, mvt = _inproj(
            t, w1, cr, sr, cm, sm, qnw, kvnw, wuq, wuk, wuvt, layer=l, seq=seq, tm=MLA_TILE)
        ret_o = _retention(rq, rk, rv, rg, gnw, layer=l, batch=batch, seq=seq)
        swa_o = _swa(sq, sk, svt, sink_row, layer=l, seq=seq, blk=MLA_TILE)
        mla_o = _mla(mq, mk, mvt, batch=batch, seq=seq, tk=MLA_TILE)
        x1, x1b, logits = _outproj(ret_o, swa_o, mla_o, t, wr, ws, wm, g1, b1, rw, layer=l, alpha=alpha)
        t = _moe(x1b, x1, logits, rb, wgu, wd, g2, b2, layer=l, alpha=alpha)
    return t.reshape(batch, seq, D_MODEL)
```

```python
import functools
import math

import numpy as np
import jax
import jax.numpy as jnp
from jax import lax
from jax.experimental import pallas as pl
from jax.experimental.pallas import tpu as pltpu

D_MODEL = 1024
HEAD_DIM = 64
ROPE_THETA = 10000.0

RET_HEADS = 4
RET_DK = 64
RET_DV = 64
RET_W = RET_HEADS * RET_DV

SWA_Q_HEADS = 6
SWA_KV_HEADS = 2
SWA_WINDOW = 128
SWA_W = SWA_Q_HEADS * HEAD_DIM
SWA_KV_W = SWA_KV_HEADS * HEAD_DIM

MLA_HEADS = 6
MLA_Q_RANK = 384
MLA_KV_RANK = 256
MLA_NOPE = 64
MLA_ROPE = 32
MLA_V = 64

N_EXPERTS = 16
N_GROUPS = 4
EXPERTS_PER_GROUP = N_EXPERTS // N_GROUPS
D_EXPERT = 256

LN_EPS = 1e-5
RMS_EPS = 1e-6
NEG_INF = -1e30

LANES = 128
MLA_PAD_W = MLA_HEADS * LANES
MLA_TILE = 512
VMEM_LIMIT = 56 * 1024 * 1024

F32 = jnp.float32
BF16 = jnp.bfloat16

_C_RQ, _C_RK, _C_RV, _C_RG = 0, 256, 512, 768
_C_SQ, _C_KR, _C_SK, _C_SV = 1024, 1408, 1536, 1664
_C_CKV, _C_CQ, _C_END = 1792, 2048, 2432

_SWA_HEAD_ORDER = (0, 3, 1, 4, 2, 5)


def _params(*sem, flags=None):
    return pltpu.CompilerParams(dimension_semantics=sem, vmem_limit_bytes=VMEM_LIMIT, flags=flags)


def _layer_spec(a, layer, **kwargs):
    return pl.BlockSpec((None,) + a.shape[1:], lambda *_: (layer,) + (0,) * (a.ndim - 1), **kwargs)


def _dot(a, b):
    return jnp.dot(a, b, preferred_element_type=F32)


def _dot_nt(a, b):
    return lax.dot_general(a, b, (((1,), (1,)), ((), ())), preferred_element_type=F32)


def _dot_tn(a, b):
    return lax.dot_general(a, b, (((0,), (0,)), ((), ())), preferred_element_type=F32)


def _split_dot(x, w_bf16):
    hi = x.astype(BF16)
    lo = (x - hi.astype(F32)).astype(BF16)
    return _dot(hi, w_bf16) + _dot(lo, w_bf16)


def _rope(h, cos, sin_signed, half):
    fwd = pltpu.roll(h, LANES - half, 1)
    bwd = pltpu.roll(h, half, 1)
    lane = lax.broadcasted_iota(jnp.int32, h.shape, 1)
    rot = jnp.where((lane % (2 * half)) < half, fwd, bwd)
    return h * cos + rot * sin_signed


def _layer_norm(y, g, b):
    mu = jnp.mean(y, -1, keepdims=True)
    d = y - mu
    var = jnp.mean(d * d, -1, keepdims=True)
    return d * lax.rsqrt(var + LN_EPS) * g + b


def _rms_norm(x, g):
    return x * lax.rsqrt(jnp.mean(x * x, -1, keepdims=True) + RMS_EPS) * g


def _inproj_kernel(x_ref, w1_ref, cr_ref, sr_ref, cm_ref, sm_ref, qnw_ref, kvnw_ref,
                   wuq_ref, wuk_ref, wuvt_ref,
                   rq_ref, rk_ref, rv_ref, rg_ref, sq_ref, sk_ref, mq_ref, mk_ref, svt_ref, mvt_ref,
                   *, mla_q_scale):
    xb = x_ref[...].astype(BF16)

    def proj(a, b):
        return _dot(xb, w1_ref[:, a:b])

    cq = proj(_C_CQ, _C_END)
    ckv = proj(_C_CKV, _C_CQ)
    rq, rk = proj(_C_RQ, _C_RK), proj(_C_RK, _C_RV)
    rv, rg = proj(_C_RV, _C_RG), proj(_C_RG, _C_SQ)
    sq_kr = proj(_C_SQ, _C_SK)
    sk_sv = proj(_C_SK, _C_CKV)
    cq = _rms_norm(cq, qnw_ref[...]).astype(BF16)
    ckv = _rms_norm(ckv, kvnw_ref[...]).astype(BF16)
    q = _dot(cq, wuq_ref[...])
    kn = _dot(ckv, wuk_ref[...])
    vt = _dot_nt(wuvt_ref[...], ckv)

    cr, sr = cr_ref[...], sr_ref[...]
    for g in range(RET_W // LANES):
        lanes = slice(g * LANES, (g + 1) * LANES)
        rq_ref[:, lanes] = _rope(rq[:, lanes], cr, sr, RET_DK // 2).astype(BF16)
        rk_ref[:, lanes] = (_rope(rk[:, lanes], cr, sr, RET_DK // 2) * (RET_DK ** -0.5)).astype(BF16)
    rv_ref[...] = rv.astype(BF16)
    rg_ref[...] = rg.astype(BF16)
    sq_ref[...] = (sq_kr[:, :SWA_W] * (HEAD_DIM ** -0.5 * math.log2(math.e))).astype(BF16)
    sk_ref[...] = sk_sv[:, :SWA_KV_W].astype(BF16)
    svt_ref[0] = sk_sv[:, SWA_KV_W:].T.astype(BF16)

    cm, sm = cm_ref[...], sm_ref[...]
    kpe = _rope(sq_kr[:, SWA_W:], cm, sm, MLA_ROPE // 2)
    for h in range(MLA_HEADS):
        lo, hi = h * LANES, (h + 1) * LANES
        mq_ref[:, lo:hi] = (_rope(q[:, lo:hi], cm, sm, MLA_ROPE // 2) * mla_q_scale).astype(BF16)
        mk_ref[:, lo:hi] = (kn[:, lo:hi] + kpe).astype(BF16)
    feat = lax.broadcasted_iota(jnp.int32, vt.shape, 0) % LANES
    mvt_ref[0] = (vt + (feat == MLA_V).astype(F32)).astype(BF16)


def _inproj(x, w1, cr, sr, cm, sm, qnw, kvnw, wuq, wuk, wuvt, *, layer, seq, tm):
    T = x.shape[0]
    nt = T // tm
    npos = seq // tm
    row = lambda w: pl.BlockSpec((tm, w), lambda i: (i, 0))
    pos = lambda: pl.BlockSpec((tm, LANES), lambda i: (i % npos, 0))
    full = lambda a: _layer_spec(a, layer)
    widths = (RET_W, RET_W, RET_W, RET_W, SWA_W, SWA_KV_W, MLA_PAD_W, MLA_PAD_W)
    widths_t = (SWA_KV_W, MLA_PAD_W)
    mla_q_scale = (MLA_NOPE + MLA_ROPE) ** -0.5 * math.log2(math.e)
    return pl.pallas_call(
        functools.partial(_inproj_kernel, mla_q_scale=mla_q_scale),
        grid=(nt,),
        in_specs=[row(D_MODEL), full(w1), pos(), pos(), pos(), pos(), full(qnw), full(kvnw),
                  full(wuq), full(wuk), full(wuvt)],
        out_specs=[row(w) for w in widths] + [pl.BlockSpec((1, w, tm), lambda i: (i, 0, 0)) for w in widths_t],
        out_shape=[jax.ShapeDtypeStruct((T, w), BF16) for w in widths]
        + [jax.ShapeDtypeStruct((nt, w, tm), BF16) for w in widths_t],
        compiler_params=_params("parallel"),
        name="inproj",
    )(x, w1, cr, sr, cm, sm, qnw, kvnw, wuq, wuk, wuvt)


def _retention_kernel(q_ref, k_ref, v_ref, g_ref, gnw_ref, dec_ref, xi_ref, zeta_ref, dm_ref, bm_ref,
                      avg_ref, o_ref, state_ref, *, chunk, n_chunks):
    @pl.when(pl.program_id(1) == 0)
    def _():
        state_ref[...] = jnp.zeros_like(state_ref)

    lane_head = lax.broadcasted_iota(jnp.int32, (chunk, RET_W), 1) // RET_DV
    avg = avg_ref[...]
    chunks = [slice(c * chunk, (c + 1) * chunk) for c in range(n_chunks)]

    def raw_scores(rows):
        q = q_ref[rows, :]
        zero = jnp.zeros_like(q)
        q_heads = jnp.concatenate([jnp.where(lane_head == h, q, zero) for h in range(RET_HEADS)], 0)
        return _dot_nt(q_heads, k_ref[rows, :])

    scores = [raw_scores(rows) for rows in chunks]
    updates = [_dot_tn((k_ref[rows, :].astype(F32) * zeta_ref[...]).astype(BF16), v_ref[rows, :])
               for rows in chunks]
    inner = [_dot((s * dec_ref[...]).astype(BF16), v_ref[rows, :]) for s, rows in zip(scores, chunks)]

    state = state_ref[...]
    outs = []
    for rows, upd, inner_heads in zip(chunks, updates, inner):
        o = _dot((q_ref[rows, :].astype(F32) * xi_ref[...]).astype(BF16), state.astype(BF16))
        for h in range(RET_HEADS):
            o = o + jnp.where(lane_head == h, inner_heads[h * chunk:(h + 1) * chunk], 0.0)
        outs.append(o)
        state = dm_ref[...] * state + bm_ref[...] * upd
    state_ref[...] = state

    mus = [_split_dot(o, avg) for o in outs]
    devs = [o - mu for o, mu in zip(outs, mus)]
    variances = [_split_dot(d * d, avg) for d in devs]
    for rows, d, var in zip(chunks, devs, variances):
        on = d * lax.rsqrt(var + LN_EPS) * gnw_ref[...]
        gate = g_ref[rows, :].astype(F32)
        o_ref[rows, :] = (on * (gate * jax.nn.sigmoid(gate))).astype(BF16)


def _retention_tables(chunk):
    H = RET_HEADS
    gamma = 1.0 - 2.0 ** (-5.0 - np.arange(H, dtype=np.float64))
    log_g = np.log(gamma)
    idx = np.arange(chunk, dtype=np.float64)
    rel = idx[:, None] - idx[None, :]
    dec = np.where(rel[None] >= 0, np.exp(np.maximum(rel, 0.0)[None] * log_g[:, None, None]), 0.0)
    dec = dec.reshape(H * chunk, chunk)
    lane_head = np.arange(RET_W) // RET_DV
    xi = np.exp((idx[:, None] + 1.0) * log_g[lane_head][None, :])
    zeta = np.exp((chunk - 1.0 - idx[:, None]) * log_g[lane_head][None, :])
    same = lane_head[:, None] == lane_head[None, :]
    dm = np.where(same, np.exp(chunk * log_g)[lane_head][:, None], 0.0)
    bm = same.astype(np.float64)
    avg = bm / RET_DV
    f = lambda a: jnp.asarray(a, F32)
    return f(dec), f(xi), f(zeta), f(dm), f(bm), jnp.asarray(avg, BF16)


def _retention(rq, rk, rv, rg, gnw, *, layer, batch, seq, blk=512, chunk=128):
    T = rq.shape[0]
    nb = seq // blk
    tables = _retention_tables(chunk)
    row = pl.BlockSpec((blk, RET_W), lambda b, i: (b * nb + i, 0))
    full = lambda a: pl.BlockSpec(a.shape, lambda b, i: (0, 0))
    return pl.pallas_call(
        functools.partial(_retention_kernel, chunk=chunk, n_chunks=blk // chunk),
        grid=(batch, nb),
        in_specs=[row, row, row, row, _layer_spec(gnw, layer)] + [full(t) for t in tables],
        out_specs=row,
        out_shape=jax.ShapeDtypeStruct((T, RET_W), BF16),
        scratch_shapes=[pltpu.VMEM((RET_W, RET_W), F32)],
        compiler_params=_params("parallel", "arbitrary"),
        name="retention",
    )(rq, rk, rv, rg, gnw, *tables)


def _swa_kernel(q_ref, kc_ref, kp_ref, vtc_ref, vtp_ref, sink_ref, o_ref, *, blk, seq):
    L = SWA_WINDOW
    n_pairs = SWA_Q_HEADS // 2
    n_sub = blk // L
    lane_lo = lax.broadcasted_iota(jnp.int32, (L, LANES), 1) < HEAD_DIM
    feat_lo = lax.broadcasted_iota(jnp.int32, (LANES, L), 0) < HEAD_DIM
    key = lax.broadcasted_iota(jnp.int32, (2 * L, SWA_Q_HEADS * L), 0)
    qry = lax.broadcasted_iota(jnp.int32, (2 * L, SWA_Q_HEADS * L), 1) % L
    band = (key > qry) & (key <= qry + L)
    sink = sink_ref[...] * math.log2(math.e)

    def scores(sb):
        r0 = sb * L
        k_prev = kp_ref[...] if sb == 0 else kc_ref[r0 - L:r0, :]
        k_win = jnp.concatenate([k_prev, kc_ref[r0:r0 + L, :]], 0)
        pairs = [q_ref[r0:r0 + L, g * LANES:(g + 1) * LANES] for g in range(n_pairs)]
        zero = jnp.zeros_like(pairs[0])
        q_heads = jnp.concatenate([jnp.where(lane_lo, p, zero) for p in pairs]
                                  + [jnp.where(lane_lo, zero, p) for p in pairs], 0)
        return _dot_nt(k_win, q_heads)

    def attend(sb, st):
        r0 = sb * L
        valid = band
        if sb == 0:
            pos0 = (pl.program_id(0) * blk) % seq
            valid = band & (key >= jnp.where(pos0 > 0, 0, L))
        st = jnp.where(valid, st, NEG_INF)
        m = jnp.maximum(jnp.max(st, 0, keepdims=True), sink)
        p = jnp.exp2(st - m)
        denom = jnp.sum(p, 0, keepdims=True) + jnp.exp2(sink - m)
        if sb == 0:
            vt_win = jnp.concatenate([vtp_ref[0], vtc_ref[0, :, :L]], 1)
        else:
            vt_win = vtc_ref[0, :, r0 - L:r0 + L]
        ot = _dot(vt_win, p.astype(BF16)) * (1.0 / denom)
        for g in range(n_pairs):
            o_lo = ot[:, g * L:(g + 1) * L]
            o_hi = ot[:, (g + n_pairs) * L:(g + n_pairs + 1) * L]
            o_ref[r0:r0 + L, g * LANES:(g + 1) * LANES] = jnp.where(feat_lo, o_lo, o_hi).T.astype(BF16)

    st = scores(0)
    for sb in range(n_sub):
        st_next = scores(sb + 1) if sb + 1 < n_sub else None
        attend(sb, st)
        st = st_next


def _swa(sq, sk, svt, sink_row, *, layer, seq, blk):
    T = sq.shape[0]
    L = SWA_WINDOW
    per = blk // L
    cur = lambda w: pl.BlockSpec((blk, w), lambda i: (i, 0))
    k_prev = pl.BlockSpec((L, SWA_KV_W), lambda i: (jnp.maximum(i * per - 1, 0), 0))
    vt_cur = pl.BlockSpec((1, SWA_KV_W, blk), lambda i: (i, 0, 0))
    vt_prev = pl.BlockSpec((1, SWA_KV_W, L), lambda i: (jnp.maximum(i - 1, 0), 0, per - 1))
    return pl.pallas_call(
        functools.partial(_swa_kernel, blk=blk, seq=seq),
        grid=(T // blk,),
        in_specs=[cur(SWA_W), cur(SWA_KV_W), k_prev, vt_cur, vt_prev, _layer_spec(sink_row, layer)],
        out_specs=cur(SWA_W),
        out_shape=jax.ShapeDtypeStruct((T, SWA_W), BF16),
        compiler_params=_params("parallel"),
        name="swa",
    )(sq, sk, sk, svt, svt, sink_row)


def _mla_kernel(q_ref, k_ref, vt_ref, o_ref, st0_ref, st1_ref, m_ref, acc_ref, *, tq, tk, heads):
    i = pl.program_id(2)
    head_lanes = [slice(h * LANES, (h + 1) * LANES) for h in range(heads)]
    st_refs = (st0_ref, st1_ref)
    all_q = slice(0, tq)
    late_q = slice(tk, tq)

    def scores(j, lanes, qs):
        start = pl.multiple_of(j * tk, tk)
        return _dot_nt(k_ref[pl.ds(start, tk), lanes], q_ref[qs, lanes])

    def step(j, cur, qs=all_q, diagonal=False, next_qs=all_q):
        for h, lanes in enumerate(head_lanes):
            if next_qs is not None:
                st_refs[1 - cur][h, :, next_qs] = scores(j + 1, lanes, next_qs)
            st = st_refs[cur][h, :, qs]
            if diagonal:
                key = lax.broadcasted_iota(jnp.int32, st.shape, 0)
                qry = lax.broadcasted_iota(jnp.int32, st.shape, 1)
                st = jnp.where(key <= qry, st, NEG_INF)
            m = m_ref[h][:, qs]
            m_new = jnp.maximum(m, jnp.max(st, 0, keepdims=True))
            pt = jnp.exp2(st - m_new).astype(BF16)
            acc_ref[h, :, qs] = jnp.exp2(m - m_new) * acc_ref[h, :, qs] + _dot(vt_ref[j, lanes, :], pt)
            if qs is all_q:
                m_ref[h] = m_new

    m_ref[...] = jnp.full(m_ref.shape, NEG_INF, F32)
    acc_ref[...] = jnp.zeros(acc_ref.shape, F32)
    for h, lanes in enumerate(head_lanes):
        st0_ref[h] = scores(0, lanes, all_q)

    def pair(p, _):
        step(2 * p, 0)
        step(2 * p + 1, 1)
        return 0

    lax.fori_loop(0, i, pair, 0)
    step(2 * i, 0, diagonal=True, next_qs=late_q)
    step(2 * i + 1, 1, qs=late_q, diagonal=True, next_qs=None)
    for h in range(heads):
        acc = acc_ref[h]
        o_ref[:, h * LANES:(h + 1) * LANES] = (acc / acc[MLA_V:MLA_V + 1, :]).T.astype(BF16)


def _mla(mq, mk, mvt, *, batch, seq, tk, heads=3):
    T = mq.shape[0]
    tq = 2 * tk
    nq = seq // tq
    qo = pl.BlockSpec((tq, heads * LANES), lambda b, h, i: (b * nq + i, h))
    k = pl.BlockSpec((seq, heads * LANES), lambda b, h, i: (b, h))
    vt = pl.BlockSpec((seq // tk, heads * LANES, tk), lambda b, h, i: (b, h, 0))
    return pl.pallas_call(
        functools.partial(_mla_kernel, tq=tq, tk=tk, heads=heads),
        grid=(batch, MLA_HEADS // heads, nq),
        in_specs=[qo, k, vt],
        out_specs=qo,
        out_shape=jax.ShapeDtypeStruct((T, MLA_PAD_W), BF16),
        scratch_shapes=[pltpu.VMEM((heads, tk, tq), F32), pltpu.VMEM((heads, tk, tq), F32),
                        pltpu.VMEM((heads, 1, tq), F32), pltpu.VMEM((heads, LANES, tq), F32)],
        compiler_params=_params("parallel", "parallel", "arbitrary"),
        name="mla",
    )(mq, mk, mvt)


def _outproj_kernel(ret_ref, swa_ref, mla_ref, x_ref, wr_ref, ws_ref, wm_ref, g_ref, b_ref,
                    rw_ref, x1_ref, x1b_ref, logits_ref, *, alpha, n_sub):
    tm = x_ref.shape[0]
    subs = [slice(s * tm // n_sub, (s + 1) * tm // n_sub) for s in range(n_sub)]

    def mix(rows):
        return (_dot(ret_ref[rows, :], wr_ref[...]) + _dot(swa_ref[rows, :], ws_ref[...])
                + _dot(mla_ref[rows, :], wm_ref[...]))

    def finish(rows, mixed):
        x1 = _layer_norm(alpha * x_ref[rows, :] + mixed, g_ref[...], b_ref[...])
        x1_ref[rows, :] = x1
        hi = x1.astype(BF16)
        x1b_ref[rows, :] = hi
        lo = (x1 - hi.astype(F32)).astype(BF16)
        hi_part = _dot(hi, rw_ref[...])
        logits_ref[rows, :] = hi_part[:, :LANES] + hi_part[:, LANES:] + _dot(lo, rw_ref[:, :LANES])

    mixed = mix(subs[0])
    for s in range(n_sub):
        mixed_next = mix(subs[s + 1]) if s + 1 < n_sub else None
        finish(subs[s], mixed)
        mixed = mixed_next


def _outproj(ret_o, swa_o, mla_o, x, wr, ws, wm, g, b, rw, *, layer, alpha, tm=512, n_sub=4):
    T = x.shape[0]
    row = lambda w: pl.BlockSpec((tm, w), lambda i: (i, 0))
    full = lambda a: pl.BlockSpec(a.shape, lambda i: (0, 0))
    per_layer = lambda a: _layer_spec(a, layer)
    return pl.pallas_call(
        functools.partial(_outproj_kernel, alpha=alpha, n_sub=n_sub),
        grid=(T // tm,),
        in_specs=[row(RET_W), row(SWA_W), row(MLA_PAD_W), row(D_MODEL), per_layer(wr), per_layer(ws),
                  per_layer(wm), per_layer(g), per_layer(b), full(rw)],
        out_specs=[row(D_MODEL), row(D_MODEL), row(LANES)],
        out_shape=[jax.ShapeDtypeStruct((T, D_MODEL), F32), jax.ShapeDtypeStruct((T, D_MODEL), BF16),
                   jax.ShapeDtypeStruct((T, LANES), F32)],
        compiler_params=_params("parallel"),
        name="outproj",
    )(ret_o, swa_o, mla_o, x, wr, ws, wm, g, b, rw)


MOE_CHUNK = 128
MOE_SEG_ALIGN = 16


def _moe_rows(tm):
    return -(-(2 * tm + N_EXPERTS * (MOE_SEG_ALIGN - 1) + MOE_CHUNK) // LANES) * LANES


def _ring_partner(x, d, width):
    n = x.shape[0]
    ahead = pltpu.roll(x, n - d, 0)
    if width == n:
        return ahead
    row = lax.broadcasted_iota(jnp.int32, x.shape, 0)
    return jnp.where((row % width) + d < width, ahead, pltpu.roll(x, width - d, 0))


def _router_gates_t(logits_t, bias_col):
    row = lax.broadcasted_iota(jnp.int32, logits_t.shape, 0)
    scores = jax.nn.sigmoid(logits_t)
    biased = scores + bias_col
    G = EXPERTS_PER_GROUP
    member = row % G
    rank = jnp.zeros(logits_t.shape, jnp.int32)
    for d in range(1, G):
        other = _ring_partner(biased, d, G)
        other_first = (member + d) % G < member
        rank = rank + ((other > biased) | ((other == biased) & other_first)).astype(jnp.int32)
    top2 = rank < 2
    part = jnp.where(top2, biased, 0.0)
    grp_score = part
    for d in range(1, G):
        grp_score = grp_score + _ring_partner(part, d, G)
    beaten = jnp.zeros(logits_t.shape, jnp.bool_)
    grp = row // G
    for dg in range(1, N_GROUPS):
        other = _ring_partner(grp_score, dg * G, N_EXPERTS)
        other_first = (grp + dg) % N_GROUPS < grp
        beaten = beaten | (other > grp_score) | ((other == grp_score) & other_first)
    picked = jnp.where(top2 & jnp.logical_not(beaten), scores, 0.0)
    return picked / jnp.sum(picked, 0, keepdims=True)


def _moe_kernel(xb_ref, x_ref, logits_ref, rb_ref, earlier_ref, wgu_ref, wd_ref, g_ref, b_ref, o_ref,
                xs_ref, ys_ref, start_ref, chunks_ref, *, alpha):
    tm = xb_ref.shape[0]
    R = xs_ref.shape[0]
    big = float(4 * R)
    experts_t = _router_gates_t(logits_ref[...].T[:N_EXPERTS], rb_ref[...])
    gates_t = jnp.concatenate([experts_t, jnp.zeros((LANES - N_EXPERTS, tm), F32)], 0)
    gates = gates_t.T

    def strict_upper(n):
        return lax.broadcasted_iota(jnp.int32, (n, n), 0) < lax.broadcasted_iota(jnp.int32, (n, n), 1)

    def seg_pad(count):
        return jnp.floor((count + (MOE_SEG_ALIGN - 1.0)) * (1.0 / MOE_SEG_ALIGN)) * MOE_SEG_ALIGN

    one = lambda mask: jnp.where(mask, 1.0, 0.0).astype(BF16)

    sel = (gates > 0.0) & (lax.broadcasted_iota(jnp.int32, gates.shape, 1) < N_EXPERTS)
    earlier = earlier_ref[...]
    rank = _dot(earlier, one(sel))
    count = jnp.sum(jnp.where(sel, 1.0, 0.0), 0, keepdims=True)
    pad = jnp.broadcast_to(seg_pad(count), (8, LANES)).astype(BF16)
    offset = _dot(pad, one(strict_upper(LANES)))[0:1]
    slot = offset + rank
    slot_a = jnp.min(jnp.where(sel, slot, big), -1, keepdims=True)
    slot_b = jnp.max(jnp.where(sel, slot, -1.0), -1, keepdims=True)
    gate_a = jnp.sum(jnp.where(sel & (slot == slot_a), gates, 0.0), -1, keepdims=True)
    gate_b = jnp.sum(jnp.where(sel & (slot == slot_b), gates, 0.0), -1, keepdims=True)

    sel_t = experts_t > 0.0
    rank_t = _dot_nt(one(sel_t), earlier)
    on_diag = (lax.broadcasted_iota(jnp.int32, (N_EXPERTS, LANES), 0)
               == lax.broadcasted_iota(jnp.int32, (N_EXPERTS, LANES), 1))
    offset_col = jnp.sum(jnp.where(on_diag, offset, 0.0), -1, keepdims=True)
    slot_t = offset_col + rank_t
    slot_a_t = jnp.min(jnp.where(sel_t, slot_t, big), 0, keepdims=True)
    slot_b_t = jnp.max(jnp.where(sel_t, slot_t, -1.0), 0, keepdims=True)

    row_id = lax.broadcasted_iota(jnp.int32, (R, tm), 0)
    sort = one((row_id == slot_a_t.astype(jnp.int32)) | (row_id == slot_b_t.astype(jnp.int32)))
    xs_ref[...] = _dot(sort, xb_ref[...]).astype(BF16)

    @pl.when(pl.program_id(0) == 0)
    def _():
        ys_ref[...] = jnp.zeros_like(ys_ref)

    offset_i = offset.astype(jnp.int32)
    chunks_i = jnp.floor((count + (MOE_CHUNK - 1.0)) * (1.0 / MOE_CHUNK)).astype(jnp.int32)
    for e in range(N_EXPERTS):
        start_ref[e] = offset_i[0, e]
        chunks_ref[e] = chunks_i[0, e]

    def chunk_rows(e, c):
        return pl.ds(pl.multiple_of(start_ref[e] + c * MOE_CHUNK, MOE_SEG_ALIGN), MOE_CHUNK)

    def activation(e, rows):
        h = _dot(xs_ref[rows, :], wgu_ref[e])
        up, lin = h[:, :D_EXPERT], h[:, D_EXPERT:]
        return (up * jax.nn.sigmoid(up) * lin).astype(BF16)

    def project(e, rows, a):
        ys_ref[rows, :] = _dot(a, wd_ref[e]).astype(BF16)

    def later_chunks(e, _):
        def body(c, _):
            rows = chunk_rows(e, c)
            project(e, rows, activation(e, rows))
            return 0
        return lax.fori_loop(1, chunks_ref[e], body, 0)

    lax.fori_loop(0, N_EXPERTS, later_chunks, 0)

    a = activation(0, chunk_rows(0, 0))
    for e in range(N_EXPERTS):
        a_next = activation(e + 1, chunk_rows(e + 1, 0)) if e + 1 < N_EXPERTS else None
        project(e, chunk_rows(e, 0), a)
        a = a_next

    col_id = lax.broadcasted_iota(jnp.int32, (tm, R), 1)
    unsort = jnp.where(col_id == slot_a.astype(jnp.int32), gate_a,
                       jnp.where(col_id == slot_b.astype(jnp.int32), gate_b, 0.0)).astype(BF16)
    o_ref[...] = _layer_norm(alpha * x_ref[...] + _dot(unsort, ys_ref[...]), g_ref[...], b_ref[...])


def _moe(x1b, x1, logits, rb, wgu, wd, g, b, *, layer, alpha, tm=512):
    T = x1.shape[0]
    R = _moe_rows(tm)
    earlier = jnp.asarray(np.tril(np.ones((tm, tm), np.float32), -1), BF16)
    row = lambda w: pl.BlockSpec((tm, w), lambda i: (i, 0))
    full = lambda a: _layer_spec(a, layer)
    resident = lambda a: _layer_spec(a, layer, pipeline_mode=pl.Buffered(1))
    return pl.pallas_call(
        functools.partial(_moe_kernel, alpha=alpha),
        grid=(T // tm,),
        in_specs=[row(D_MODEL), row(D_MODEL), row(LANES), pl.BlockSpec(rb.shape, lambda i: (0, 0)),
                  pl.BlockSpec(earlier.shape, lambda i: (0, 0)), resident(wgu), resident(wd), full(g), full(b)],
        out_specs=row(D_MODEL),
        out_shape=jax.ShapeDtypeStruct((T, D_MODEL), F32),
        scratch_shapes=[pltpu.VMEM((R, D_MODEL), BF16), pltpu.VMEM((R, D_MODEL), BF16),
                        pltpu.SMEM((N_EXPERTS,), jnp.int32), pltpu.SMEM((N_EXPERTS,), jnp.int32)],
        compiler_params=_params("arbitrary"),
        name="moe",
    )(x1b, x1, logits, rb, earlier, wgu, wd, g, b)


def _rope_tables(seq):
    pos = np.arange(seq, dtype=np.float64)[:, None]
    lane = np.arange(LANES)

    def table(dim, active, offset):
        j = (lane - offset) % dim
        inv = ROPE_THETA ** (-(2.0 * (j % (dim // 2))) / dim)
        ang = pos * inv[None, :]
        sign = np.where(j < dim // 2, -1.0, 1.0)
        cos = np.where(active[None, :], np.cos(ang), 1.0)
        sin = np.where(active[None, :], np.sin(ang) * sign[None, :], 0.0)
        return jnp.asarray(cos, F32), jnp.asarray(sin, F32)

    cr, sr = table(RET_DK, np.ones(LANES, bool), 0)
    cm, sm = table(MLA_ROPE, (lane >= MLA_NOPE) & (lane < MLA_NOPE + MLA_ROPE), MLA_NOPE)
    return cr, sr, cm, sm


def _stacked_weights(w_in, mla_w_uq, mla_w_ukv, w_out):
    depth = w_in.shape[0]
    sizes = (256, 256, 256, 256, SWA_W, SWA_KV_W, SWA_KV_W, MLA_Q_RANK, MLA_KV_RANK, MLA_ROPE)
    o = [int(v) for v in np.concatenate([[0], np.cumsum(sizes)])]
    order = np.array(_SWA_HEAD_ORDER)
    pad_lanes = lambda a: jnp.pad(a, [(0, 0)] * (a.ndim - 1) + [(0, LANES - a.shape[-1])])

    sq = w_in[:, :, o[4]:o[5]].reshape(depth, D_MODEL, SWA_Q_HEADS, HEAD_DIM)[:, :, order]
    kr = jnp.pad(w_in[:, :, o[9]:o[10]], ((0, 0), (0, 0), (MLA_NOPE, LANES - MLA_NOPE - MLA_ROPE)))
    w1 = jnp.concatenate([w_in[:, :, :o[4]], sq.reshape(depth, D_MODEL, SWA_W), kr, w_in[:, :, o[5]:o[7]],
                          w_in[:, :, o[8]:o[9]], w_in[:, :, o[7]:o[8]]], 2).astype(BF16)

    dq = MLA_NOPE + MLA_ROPE
    wuq = pad_lanes(mla_w_uq.reshape(depth, MLA_Q_RANK, MLA_HEADS, dq)).reshape(depth, MLA_Q_RANK, MLA_PAD_W)
    ukv = mla_w_ukv.reshape(depth, MLA_KV_RANK, MLA_HEADS, MLA_NOPE + MLA_V)
    wuk = pad_lanes(ukv[..., :MLA_NOPE]).reshape(depth, MLA_KV_RANK, MLA_PAD_W)
    wuvt = jnp.swapaxes(pad_lanes(ukv[..., MLA_NOPE:]).reshape(depth, MLA_KV_RANK, MLA_PAD_W), 1, 2)

    wr = w_out[:, :RET_W]
    ws = w_out[:, RET_W:RET_W + SWA_W].reshape(depth, SWA_Q_HEADS, HEAD_DIM, D_MODEL)[:, order]
    wm = w_out[:, RET_W + SWA_W:].reshape(depth, MLA_HEADS, MLA_V, D_MODEL)
    wm = jnp.pad(wm, ((0, 0), (0, 0), (0, LANES - MLA_V), (0, 0)))
    to_bf16 = lambda a: a.astype(BF16)
    return (w1, to_bf16(wuq), to_bf16(wuk), to_bf16(wuvt), to_bf16(wr),
            to_bf16(ws.reshape(depth, SWA_W, D_MODEL)), to_bf16(wm.reshape(depth, MLA_PAD_W, D_MODEL)))


def kernel(x, w_in, ret_gn_w, swa_sinks, mla_q_norm_w, mla_kv_norm_w, mla_w_uq, mla_w_ukv, w_out,
           ln1_g, ln1_b, router_w, router_bias, exp_w_gate_up, exp_w_down, ln2_g, ln2_b):
    batch, seq, _ = x.shape
    depth = w_in.shape[0]
    alpha = (2 * depth) ** 0.25
    cr, sr, cm, sm = _rope_tables(seq)

    rw = jnp.pad(router_w, ((0, 0), (0, LANES - N_EXPERTS)))
    rwh = rw.astype(BF16)
    rw = jnp.concatenate([rwh, (rw - rwh.astype(F32)).astype(BF16)], 1)
    rb = router_bias.astype(F32).reshape(N_EXPERTS, 1)

    w1, wuq, wuk, wuvt, wr, ws, wm = _stacked_weights(w_in, mla_w_uq, mla_w_ukv, w_out)
    wgu, wd = exp_w_gate_up.astype(BF16), exp_w_down.astype(BF16)
    rows = lambda a: a.astype(F32).reshape(depth, 1, -1)
    qnw, kvnw, gnw = rows(mla_q_norm_w), rows(mla_kv_norm_w), rows(ret_gn_w)
    g1, b1, g2, b2 = rows(ln1_g), rows(ln1_b), rows(ln2_g), rows(ln2_b)
    sink_row = rows(jnp.repeat(swa_sinks, SWA_WINDOW, axis=1))

    t = x.reshape(batch * seq, D_MODEL)
    for l in range(depth):
        rq, rk, rv, rg, sq, sk, mq, mk, svt, mvt = _inproj(
            t, w1, cr, sr, cm, sm, qnw, kvnw, wuq, wuk, wuvt, layer=l, seq=seq, tm=MLA_TILE)
        ret_o = _retention(rq, rk, rv, rg, gnw, layer=l, batch=batch, seq=seq)
        swa_o = _swa(sq, sk, svt, sink_row, layer=l, seq=seq, blk=MLA_TILE)
        mla_o = _mla(mq, mk, mvt, batch=batch, seq=seq, tk=MLA_TILE)
        x1, x1b, logits = _outproj(ret_o, swa_o, mla_o, t, wr, ws, wm, g1, b1, rw, layer=l, alpha=alpha)
        t = _moe(x1b, x1, logits, rb, wgu, wd, g2, b2, layer=l, alpha=alpha)
    return t.reshape(batch, seq, D_MODEL)
```

```python
import functools
import math

import numpy as np
import jax
import jax.numpy as jnp
from jax import lax
from jax.experimental import pallas as pl
from jax.experimental.pallas import tpu as pltpu

D_MODEL = 1024
HEAD_DIM = 64
ROPE_THETA = 10000.0

RET_HEADS = 4
RET_DK = 64
RET_DV = 64
RET_W = RET_HEADS * RET_DV

SWA_Q_HEADS = 6
SWA_KV_HEADS = 2
SWA_WINDOW = 128
SWA_W = SWA_Q_HEADS * HEAD_DIM
SWA_KV_W = SWA_KV_HEADS * HEAD_DIM

MLA_HEADS = 6
MLA_Q_RANK = 384
MLA_KV_RANK = 256
MLA_NOPE = 64
MLA_ROPE = 32
MLA_V = 64

N_EXPERTS = 16
N_GROUPS = 4
EXPERTS_PER_GROUP = N_EXPERTS // N_GROUPS
D_EXPERT = 256

LN_EPS = 1e-5
RMS_EPS = 1e-6
NEG_INF = -1e30

LANES = 128
MLA_PAD_W = MLA_HEADS * LANES
MLA_TILE = 512
VMEM_LIMIT = 56 * 1024 * 1024

F32 = jnp.float32
BF16 = jnp.bfloat16

_C_RQ, _C_RK, _C_RV, _C_RG = 0, 256, 512, 768
_C_SQ, _C_KR, _C_SK, _C_SV = 1024, 1408, 1536, 1664
_C_CKV, _C_CQ, _C_END = 1792, 2048, 2432

_SWA_HEAD_ORDER = (0, 3, 1, 4, 2, 5)


def _params(*sem, flags=None):
    return pltpu.CompilerParams(dimension_semantics=sem, vmem_limit_bytes=VMEM_LIMIT, flags=flags)


def _layer_spec(a, layer, **kwargs):
    return pl.BlockSpec((None,) + a.shape[1:], lambda *_: (layer,) + (0,) * (a.ndim - 1), **kwargs)


def _dot(a, b):
    return jnp.dot(a, b, preferred_element_type=F32)


def _dot_nt(a, b):
    return lax.dot_general(a, b, (((1,), (1,)), ((), ())), preferred_element_type=F32)


def _dot_tn(a, b):
    return lax.dot_general(a, b, (((0,), (0,)), ((), ())), preferred_element_type=F32)


def _split_dot(x, w_bf16):
    hi = x.astype(BF16)
    lo = (x - hi.astype(F32)).astype(BF16)
    return _dot(hi, w_bf16) + _dot(lo, w_bf16)


def _rope(h, cos, sin_signed, half):
    fwd = pltpu.roll(h, LANES - half, 1)
    bwd = pltpu.roll(h, half, 1)
    lane = lax.broadcasted_iota(jnp.int32, h.shape, 1)
    rot = jnp.where((lane % (2 * half)) < half, fwd, bwd)
    return h * cos + rot * sin_signed


def _layer_norm(y, g, b):
    mu = jnp.mean(y, -1, keepdims=True)
    d = y - mu
    var = jnp.mean(d * d, -1, keepdims=True)
    return d * lax.rsqrt(var + LN_EPS) * g + b


def _rms_norm(x, g):
    return x * lax.rsqrt(jnp.mean(x * x, -1, keepdims=True) + RMS_EPS) * g


def _inproj_kernel(x_ref, w1_ref, cr_ref, sr_ref, cm_ref, sm_ref, qnw_ref, kvnw_ref,
                   wuq_ref, wuk_ref, wuvt_ref,
                   rq_ref, rk_ref, rv_ref, rg_ref, sq_ref, sk_ref, mq_ref, mk_ref, svt_ref, mvt_ref,
                   *, mla_q_scale):
    xb = x_ref[...].astype(BF16)

    def proj(a, b):
        return _dot(xb, w1_ref[:, a:b])

    cq = proj(_C_CQ, _C_END)
    ckv = proj(_C_CKV, _C_CQ)
    rq, rk = proj(_C_RQ, _C_RK), proj(_C_RK, _C_RV)
    rv, rg = proj(_C_RV, _C_RG), proj(_C_RG, _C_SQ)
    sq_kr = proj(_C_SQ, _C_SK)
    sk_sv = proj(_C_SK, _C_CKV)
    cq = _rms_norm(cq, qnw_ref[...]).astype(BF16)
    ckv = _rms_norm(ckv, kvnw_ref[...]).astype(BF16)
    q = _dot(cq, wuq_ref[...])
    kn = _dot(ckv, wuk_ref[...])
    vt = _dot_nt(wuvt_ref[...], ckv)

    cr, sr = cr_ref[...], sr_ref[...]
    for g in range(RET_W // LANES):
        lanes = slice(g * LANES, (g + 1) * LANES)
        rq_ref[:, lanes] = _rope(rq[:, lanes], cr, sr, RET_DK // 2).astype(BF16)
        rk_ref[:, lanes] = (_rope(rk[:, lanes], cr, sr, RET_DK // 2) * (RET_DK ** -0.5)).astype(BF16)
    rv_ref[...] = rv.astype(BF16)
    rg_ref[...] = rg.astype(BF16)
    sq_ref[...] = (sq_kr[:, :SWA_W] * (HEAD_DIM ** -0.5 * math.log2(math.e))).astype(BF16)
    sk_ref[...] = sk_sv[:, :SWA_KV_W].astype(BF16)
    svt_ref[0] = sk_sv[:, SWA_KV_W:].T.astype(BF16)

    cm, sm = cm_ref[...], sm_ref[...]
    kpe = _rope(sq_kr[:, SWA_W:], cm, sm, MLA_ROPE // 2)
    for h in range(MLA_HEADS):
        lo, hi = h * LANES, (h + 1) * LANES
        mq_ref[:, lo:hi] = (_rope(q[:, lo:hi], cm, sm, MLA_ROPE // 2) * mla_q_scale).astype(BF16)
        mk_ref[:, lo:hi] = (kn[:, lo:hi] + kpe).astype(BF16)
    feat = lax.broadcasted_iota(jnp.int32, vt.shape, 0) % LANES
    mvt_ref[0] = (vt + (feat == MLA_V).astype(F32)).astype(BF16)


def _inproj(x, w1, cr, sr, cm, sm, qnw, kvnw, wuq, wuk, wuvt, *, layer, seq, tm):
    T = x.shape[0]
    nt = T // tm
    npos = seq // tm
    row = lambda w: pl.BlockSpec((tm, w), lambda i: (i, 0))
    pos = lambda: pl.BlockSpec((tm, LANES), lambda i: (i % npos, 0))
    full = lambda a: _layer_spec(a, layer)
    widths = (RET_W, RET_W, RET_W, RET_W, SWA_W, SWA_KV_W, MLA_PAD_W, MLA_PAD_W)
    widths_t = (SWA_KV_W, MLA_PAD_W)
    mla_q_scale = (MLA_NOPE + MLA_ROPE) ** -0.5 * math.log2(math.e)
    return pl.pallas_call(
        functools.partial(_inproj_kernel, mla_q_scale=mla_q_scale),
        grid=(nt,),
        in_specs=[row(D_MODEL), full(w1), pos(), pos(), pos(), pos(), full(qnw), full(kvnw),
                  full(wuq), full(wuk), full(wuvt)],
        out_specs=[row(w) for w in widths] + [pl.BlockSpec((1, w, tm), lambda i: (i, 0, 0)) for w in widths_t],
        out_shape=[jax.ShapeDtypeStruct((T, w), BF16) for w in widths]
        + [jax.ShapeDtypeStruct((nt, w, tm), BF16) for w in widths_t],
        compiler_params=_params("parallel"),
        name="inproj",
    )(x, w1, cr, sr, cm, sm, qnw, kvnw, wuq, wuk, wuvt)


def _retention_kernel(q_ref, k_ref, v_ref, g_ref, gnw_ref, dec_ref, xi_ref, zeta_ref, dm_ref, bm_ref,
                      avg_ref, o_ref, state_ref, *, chunk, n_chunks):
    @pl.when(pl.program_id(1) == 0)
    def _():
        state_ref[...] = jnp.zeros_like(state_ref)

    lane_head = lax.broadcasted_iota(jnp.int32, (chunk, RET_W), 1) // RET_DV
    avg = avg_ref[...]
    chunks = [slice(c * chunk, (c + 1) * chunk) for c in range(n_chunks)]

    def raw_scores(rows):
        q = q_ref[rows, :]
        zero = jnp.zeros_like(q)
        q_heads = jnp.concatenate([jnp.where(lane_head == h, q, zero) for h in range(RET_HEADS)], 0)
        return _dot_nt(q_heads, k_ref[rows, :])

    scores = [raw_scores(rows) for rows in chunks]
    updates = [_dot_tn((k_ref[rows, :].astype(F32) * zeta_ref[...]).astype(BF16), v_ref[rows, :])
               for rows in chunks]
    inner = [_dot((s * dec_ref[...]).astype(BF16), v_ref[rows, :]) for s, rows in zip(scores, chunks)]

    state = state_ref[...]
    outs = []
    for rows, upd, inner_heads in zip(chunks, updates, inner):
        o = _dot((q_ref[rows, :].astype(F32) * xi_ref[...]).astype(BF16), state.astype(BF16))
        for h in range(RET_HEADS):
            o = o + jnp.where(lane_head == h, inner_heads[h * chunk:(h + 1) * chunk], 0.0)
        outs.append(o)
        state = dm_ref[...] * state + bm_ref[...] * upd
    state_ref[...] = state

    mus = [_split_dot(o, avg) for o in outs]
    devs = [o - mu for o, mu in zip(outs, mus)]
    variances = [_split_dot(d * d, avg) for d in devs]
    for rows, d, var in zip(chunks, devs, variances):
        on = d * lax.rsqrt(var + LN_EPS) * gnw_ref[...]
        gate = g_ref[rows, :].astype(F32)
        o_ref[rows, :] = (on * (gate * jax.nn.sigmoid(gate))).astype(BF16)


def _retention_tables(chunk):
    H = RET_HEADS
    gamma = 1.0 - 2.0 ** (-5.0 - np.arange(H, dtype=np.float64))
    log_g = np.log(gamma)
    idx = np.arange(chunk, dtype=np.float64)
    rel = idx[:, None] - idx[None, :]
    dec = np.where(rel[None] >= 0, np.exp(np.maximum(rel, 0.0)[None] * log_g[:, None, None]), 0.0)
    dec = dec.reshape(H * chunk, chunk)
    lane_head = np.arange(RET_W) // RET_DV
    xi = np.exp((idx[:, None] + 1.0) * log_g[lane_head][None, :])
    zeta = np.exp((chunk - 1.0 - idx[:, None]) * log_g[lane_head][None, :])
    same = lane_head[:, None] == lane_head[None, :]
    dm = np.where(same, np.exp(chunk * log_g)[lane_head][:, None], 0.0)
    bm = same.astype(np.float64)
    avg = bm / RET_DV
    f = lambda a: jnp.asarray(a, F32)
    return f(dec), f(xi), f(zeta), f(dm), f(bm), jnp.asarray(avg, BF16)


def _retention(rq, rk, rv, rg, gnw, *, layer, batch, seq, blk=512, chunk=128):
    T = rq.shape[0]
    nb = seq // blk
    tables = _retention_tables(chunk)
    row = pl.BlockSpec((blk, RET_W), lambda b, i: (b * nb + i, 0))
    full = lambda a: pl.BlockSpec(a.shape, lambda b, i: (0, 0))
    return pl.pallas_call(
        functools.partial(_retention_kernel, chunk=chunk, n_chunks=blk // chunk),
        grid=(batch, nb),
        in_specs=[row, row, row, row, _layer_spec(gnw, layer)] + [full(t) for t in tables],
        out_specs=row,
        out_shape=jax.ShapeDtypeStruct((T, RET_W), BF16),
        scratch_shapes=[pltpu.VMEM((RET_W, RET_W), F32)],
        compiler_params=_params("parallel", "arbitrary"),
        name="retention",
    )(rq, rk, rv, rg, gnw, *tables)


def _swa_kernel(q_ref, kc_ref, kp_ref, vtc_ref, vtp_ref, sink_ref, o_ref, *, blk, seq):
    L = SWA_WINDOW
    n_pairs = SWA_Q_HEADS // 2
    n_sub = blk // L
    lane_lo = lax.broadcasted_iota(jnp.int32, (L, LANES), 1) < HEAD_DIM
    feat_lo = lax.broadcasted_iota(jnp.int32, (LANES, L), 0) < HEAD_DIM
    key = lax.broadcasted_iota(jnp.int32, (2 * L, SWA_Q_HEADS * L), 0)
    qry = lax.broadcasted_iota(jnp.int32, (2 * L, SWA_Q_HEADS * L), 1) % L
    band = (key > qry) & (key <= qry + L)
    sink = sink_ref[...] * math.log2(math.e)

    def scores(sb):
        r0 = sb * L
        k_prev = kp_ref[...] if sb == 0 else kc_ref[r0 - L:r0, :]
        k_win = jnp.concatenate([k_prev, kc_ref[r0:r0 + L, :]], 0)
        pairs = [q_ref[r0:r0 + L, g * LANES:(g + 1) * LANES] for g in range(n_pairs)]
        zero = jnp.zeros_like(pairs[0])
        q_heads = jnp.concatenate([jnp.where(lane_lo, p, zero) for p in pairs]
                                  + [jnp.where(lane_lo, zero, p) for p in pairs], 0)
        return _dot_nt(k_win, q_heads)

    def attend(sb, st):
        r0 = sb * L
        valid = band
        if sb == 0:
            pos0 = (pl.program_id(0) * blk) % seq
            valid = band & (key >= jnp.where(pos0 > 0, 0, L))
        st = jnp.where(valid, st, NEG_INF)
        m = jnp.maximum(jnp.max(st, 0, keepdims=True), sink)
        p = jnp.exp2(st - m)
        denom = jnp.sum(p, 0, keepdims=True) + jnp.exp2(sink - m)
        if sb == 0:
            vt_win = jnp.concatenate([vtp_ref[0], vtc_ref[0, :, :L]], 1)
        else:
            vt_win = vtc_ref[0, :, r0 - L:r0 + L]
        ot = _dot(vt_win, p.astype(BF16)) * (1.0 / denom)
        for g in range(n_pairs):
            o_lo = ot[:, g * L:(g + 1) * L]
            o_hi = ot[:, (g + n_pairs) * L:(g + n_pairs + 1) * L]
            o_ref[r0:r0 + L, g * LANES:(g + 1) * LANES] = jnp.where(feat_lo, o_lo, o_hi).T.astype(BF16)

    st = scores(0)
    for sb in range(n_sub):
        st_next = scores(sb + 1) if sb + 1 < n_sub else None
        attend(sb, st)
        st = st_next


def _swa(sq, sk, svt, sink_row, *, layer, seq, blk):
    T = sq.shape[0]
    L = SWA_WINDOW
    per = blk // L
    cur = lambda w: pl.BlockSpec((blk, w), lambda i: (i, 0))
    k_prev = pl.BlockSpec((L, SWA_KV_W), lambda i: (jnp.maximum(i * per - 1, 0), 0))
    vt_cur = pl.BlockSpec((1, SWA_KV_W, blk), lambda i: (i, 0, 0))
    vt_prev = pl.BlockSpec((1, SWA_KV_W, L), lambda i: (jnp.maximum(i - 1, 0), 0, per - 1))
    return pl.pallas_call(
        functools.partial(_swa_kernel, blk=blk, seq=seq),
        grid=(T // blk,),
        in_specs=[cur(SWA_W), cur(SWA_KV_W), k_prev, vt_cur, vt_prev, _layer_spec(sink_row, layer)],
        out_specs=cur(SWA_W),
        out_shape=jax.ShapeDtypeStruct((T, SWA_W), BF16),
        compiler_params=_params("parallel"),
        name="swa",
    )(sq, sk, sk, svt, svt, sink_row)


def _mla_kernel(q_ref, k_ref, vt_ref, o_ref, st0_ref, st1_ref, m_ref, acc_ref, *, tq, tk, heads):
    i = pl.program_id(2)
    head_lanes = [slice(h * LANES, (h + 1) * LANES) for h in range(heads)]
    st_refs = (st0_ref, st1_ref)
    all_q = slice(0, tq)
    late_q = slice(tk, tq)

    def scores(j, lanes, qs):
        start = pl.multiple_of(j * tk, tk)
        return _dot_nt(k_ref[pl.ds(start, tk), lanes], q_ref[qs, lanes])

    def step(j, cur, qs=all_q, diagonal=False, next_qs=all_q):
        for h, lanes in enumerate(head_lanes):
            if next_qs is not None:
                st_refs[1 - cur][h, :, next_qs] = scores(j + 1, lanes, next_qs)
            st = st_refs[cur][h, :, qs]
            if diagonal:
                key = lax.broadcasted_iota(jnp.int32, st.shape, 0)
                qry = lax.broadcasted_iota(jnp.int32, st.shape, 1)
                st = jnp.where(key <= qry, st, NEG_INF)
            m = m_ref[h][:, qs]
            m_new = jnp.maximum(m, jnp.max(st, 0, keepdims=True))
            pt = jnp.exp2(st - m_new).astype(BF16)
            acc_ref[h, :, qs] = jnp.exp2(m - m_new) * acc_ref[h, :, qs] + _dot(vt_ref[j, lanes, :], pt)
            if qs is all_q:
                m_ref[h] = m_new

    m_ref[...] = jnp.full(m_ref.shape, NEG_INF, F32)
    acc_ref[...] = jnp.zeros(acc_ref.shape, F32)
    for h, lanes in enumerate(head_lanes):
        st0_ref[h] = scores(0, lanes, all_q)

    def pair(p, _):
        step(2 * p, 0)
        step(2 * p + 1, 1)
        return 0

    lax.fori_loop(0, i, pair, 0)
    step(2 * i, 0, diagonal=True, next_qs=late_q)
    step(2 * i + 1, 1, qs=late_q, diagonal=True, next_qs=None)
    for h in range(heads):
        acc = acc_ref[h]
        o_ref[:, h * LANES:(h + 1) * LANES] = (acc / acc[MLA_V:MLA_V + 1, :]).T.astype(BF16)


def _mla(mq, mk, mvt, *, batch, seq, tk, heads=3):
    T = mq.shape[0]
    tq = 2 * tk
    nq = seq // tq
    qo = pl.BlockSpec((tq, heads * LANES), lambda b, h, i: (b * nq + i, h))
    k = pl.BlockSpec((seq, heads * LANES), lambda b, h, i: (b, h))
    vt = pl.BlockSpec((seq // tk, heads * LANES, tk), lambda b, h, i: (b, h, 0))
    return pl.pallas_call(
        functools.partial(_mla_kernel, tq=tq, tk=tk, heads=heads),
        grid=(batch, MLA_HEADS // heads, nq),
        in_specs=[qo, k, vt],
        out_specs=qo,
        out_shape=jax.ShapeDtypeStruct((T, MLA_PAD_W), BF16),
        scratch_shapes=[pltpu.VMEM((heads, tk, tq), F32), pltpu.VMEM((heads, tk, tq), F32),
                        pltpu.VMEM((heads, 1, tq), F32), pltpu.VMEM((heads, LANES, tq), F32)],
        compiler_params=_params("parallel", "parallel", "arbitrary"),
        name="mla",
    )(mq, mk, mvt)


def _outproj_kernel(ret_ref, swa_ref, mla_ref, x_ref, wr_ref, ws_ref, wm_ref, g_ref, b_ref,
                    rw_ref, x1_ref, x1b_ref, logits_ref, *, alpha, n_sub):
    tm = x_ref.shape[0]
    subs = [slice(s * tm // n_sub, (s + 1) * tm // n_sub) for s in range(n_sub)]

    def mix(rows):
        return (_dot(ret_ref[rows, :], wr_ref[...]) + _dot(swa_ref[rows, :], ws_ref[...])
                + _dot(mla_ref[rows, :], wm_ref[...]))

    def finish(rows, mixed):
        x1 = _layer_norm(alpha * x_ref[rows, :] + mixed, g_ref[...], b_ref[...])
        x1_ref[rows, :] = x1
        hi = x1.astype(BF16)
        x1b_ref[rows, :] = hi
        lo = (x1 - hi.astype(F32)).astype(BF16)
        hi_part = _dot(hi, rw_ref[...])
        logits_ref[rows, :] = hi_part[:, :LANES] + hi_part[:, LANES:] + _dot(lo, rw_ref[:, :LANES])

    mixed = mix(subs[0])
    for s in range(n_sub):
        mixed_next = mix(subs[s + 1]) if s + 1 < n_sub else None
        finish(subs[s], mixed)
        mixed = mixed_next


def _outproj(ret_o, swa_o, mla_o, x, wr, ws, wm, g, b, rw, *, layer, alpha, tm=512, n_sub=4):
    T = x.shape[0]
    row = lambda w: pl.BlockSpec((tm, w), lambda i: (i, 0))
    full = lambda a: pl.BlockSpec(a.shape, lambda i: (0, 0))
    per_layer = lambda a: _layer_spec(a, layer)
    return pl.pallas_call(
        functools.partial(_outproj_kernel, alpha=alpha, n_sub=n_sub),
        grid=(T // tm,),
        in_specs=[row(RET_W), row(SWA_W), row(MLA_PAD_W), row(D_MODEL), per_layer(wr), per_layer(ws),
                  per_layer(wm), per_layer(g), per_layer(b), full(rw)],
        out_specs=[row(D_MODEL), row(D_MODEL), row(LANES)],
        out_shape=[jax.ShapeDtypeStruct((T, D_MODEL), F32), jax.ShapeDtypeStruct((T, D_MODEL), BF16),
                   jax.ShapeDtypeStruct((T, LANES), F32)],
        compiler_params=_params("parallel"),
        name="outproj",
    )(ret_o, swa_o, mla_o, x, wr, ws, wm, g, b, rw)


MOE_CHUNK = 128
MOE_SEG_ALIGN = 16


def _moe_rows(tm):
    return -(-(2 * tm + N_EXPERTS * (MOE_SEG_ALIGN - 1) + MOE_CHUNK) // LANES) * LANES


def _ring_partner(x, d, width):
    n = x.shape[0]
    ahead = pltpu.roll(x, n - d, 0)
    if width == n:
        return ahead
    row = lax.broadcasted_iota(jnp.int32, x.shape, 0)
    return jnp.where((row % width) + d < width, ahead, pltpu.roll(x, width - d, 0))


def _router_gates_t(logits_t, bias_col):
    row = lax.broadcasted_iota(jnp.int32, logits_t.shape, 0)
    scores = jax.nn.sigmoid(logits_t)
    biased = scores + bias_col
    G = EXPERTS_PER_GROUP
    member = row % G
    rank = jnp.zeros(logits_t.shape, jnp.int32)
    for d in range(1, G):
        other = _ring_partner(biased, d, G)
        other_first = (member + d) % G < member
        rank = rank + ((other > biased) | ((other == biased) & other_first)).astype(jnp.int32)
    top2 = rank < 2
    part = jnp.where(top2, biased, 0.0)
    grp_score = part
    for d in range(1, G):
        grp_score = grp_score + _ring_partner(part, d, G)
    beaten = jnp.zeros(logits_t.shape, jnp.bool_)
    grp = row // G
    for dg in range(1, N_GROUPS):
        other = _ring_partner(grp_score, dg * G, N_EXPERTS)
        other_first = (grp + dg) % N_GROUPS < grp
        beaten = beaten | (other > grp_score) | ((other == grp_score) & other_first)
    picked = jnp.where(top2 & jnp.logical_not(beaten), scores, 0.0)
    return picked / jnp.sum(picked, 0, keepdims=True)


def _moe_kernel(xb_ref, x_ref, logits_ref, rb_ref, earlier_ref, wgu_hbm, wd_hbm, g_ref, b_ref, o_ref,
                xs_ref, ys_ref, wgu_ref, wd_ref, gu_stage, d_stage, sem, start_ref, chunks_ref,
                *, alpha, layer):
    tm = xb_ref.shape[0]
    R = xs_ref.shape[0]
    big = float(4 * R)

    @pl.when(pl.program_id(0) == 0)
    def _():
        def expert_copies(e, slot):
            return (pltpu.make_async_copy(wgu_hbm.at[layer, e], gu_stage.at[slot], sem.at[0, slot]),
                    pltpu.make_async_copy(wd_hbm.at[layer, e], d_stage.at[slot], sem.at[1, slot]))

        for copy in expert_copies(0, 0):
            copy.start()
        for e in range(N_EXPERTS):
            slot = e % 2
            if e + 1 < N_EXPERTS:
                for copy in expert_copies(e + 1, 1 - slot):
                    copy.start()
            for copy in expert_copies(e, slot):
                copy.wait()
            wgu_ref[e] = gu_stage[slot].astype(BF16)
            wd_ref[e] = d_stage[slot].astype(BF16)

    experts_t = _router_gates_t(logits_ref[...].T[:N_EXPERTS], rb_ref[...])
    gates_t = jnp.concatenate([experts_t, jnp.zeros((LANES - N_EXPERTS, tm), F32)], 0)
    gates = gates_t.T

    def strict_upper(n):
        return lax.broadcasted_iota(jnp.int32, (n, n), 0) < lax.broadcasted_iota(jnp.int32, (n, n), 1)

    def seg_pad(count):
        return jnp.floor((count + (MOE_SEG_ALIGN - 1.0)) * (1.0 / MOE_SEG_ALIGN)) * MOE_SEG_ALIGN

    one = lambda mask: jnp.where(mask, 1.0, 0.0).astype(BF16)

    sel = (gates > 0.0) & (lax.broadcasted_iota(jnp.int32, gates.shape, 1) < N_EXPERTS)
    earlier = earlier_ref[...]
    rank = _dot(earlier, one(sel))
    count = jnp.sum(jnp.where(sel, 1.0, 0.0), 0, keepdims=True)
    pad = jnp.broadcast_to(seg_pad(count), (8, LANES)).astype(BF16)
    offset = _dot(pad, one(strict_upper(LANES)))[0:1]
    slot = offset + rank
    slot_a = jnp.min(jnp.where(sel, slot, big), -1, keepdims=True)
    slot_b = jnp.max(jnp.where(sel, slot, -1.0), -1, keepdims=True)
    gate_a = jnp.sum(jnp.where(sel & (slot == slot_a), gates, 0.0), -1, keepdims=True)
    gate_b = jnp.sum(jnp.where(sel & (slot == slot_b), gates, 0.0), -1, keepdims=True)

    sel_t = experts_t > 0.0
    rank_t = _dot_nt(one(sel_t), earlier)
    on_diag = (lax.broadcasted_iota(jnp.int32, (N_EXPERTS, LANES), 0)
               == lax.broadcasted_iota(jnp.int32, (N_EXPERTS, LANES), 1))
    offset_col = jnp.sum(jnp.where(on_diag, offset, 0.0), -1, keepdims=True)
    slot_t = offset_col + rank_t
    slot_a_t = jnp.min(jnp.where(sel_t, slot_t, big), 0, keepdims=True)
    slot_b_t = jnp.max(jnp.where(sel_t, slot_t, -1.0), 0, keepdims=True)

    row_id = lax.broadcasted_iota(jnp.int32, (R, tm), 0)
    sort = one((row_id == slot_a_t.astype(jnp.int32)) | (row_id == slot_b_t.astype(jnp.int32)))
    xs_ref[...] = _dot(sort, xb_ref[...]).astype(BF16)

    @pl.when(pl.program_id(0) == 0)
    def _():
        ys_ref[...] = jnp.zeros_like(ys_ref)

    offset_i = offset.astype(jnp.int32)
    chunks_i = jnp.floor((count + (MOE_CHUNK - 1.0)) * (1.0 / MOE_CHUNK)).astype(jnp.int32)
    for e in range(N_EXPERTS):
        start_ref[e] = offset_i[0, e]
        chunks_ref[e] = chunks_i[0, e]

    def chunk_rows(e, c):
        return pl.ds(pl.multiple_of(start_ref[e] + c * MOE_CHUNK, MOE_SEG_ALIGN), MOE_CHUNK)

    def activation(e, rows):
        h = _dot(xs_ref[rows, :], wgu_ref[e])
        up, lin = h[:, :D_EXPERT], h[:, D_EXPERT:]
        return (up * jax.nn.sigmoid(up) * lin).astype(BF16)

    def project(e, rows, a):
        ys_ref[rows, :] = _dot(a, wd_ref[e]).astype(BF16)

    def later_chunks(e, _):
        def body(c, _):
            rows = chunk_rows(e, c)
            project(e, rows, activation(e, rows))
            return 0
        return lax.fori_loop(1, chunks_ref[e], body, 0)

    lax.fori_loop(0, N_EXPERTS, later_chunks, 0)

    a = activation(0, chunk_rows(0, 0))
    for e in range(N_EXPERTS):
        a_next = activation(e + 1, chunk_rows(e + 1, 0)) if e + 1 < N_EXPERTS else None
        project(e, chunk_rows(e, 0), a)
        a = a_next

    col_id = lax.broadcasted_iota(jnp.int32, (tm, R), 1)
    unsort = jnp.where(col_id == slot_a.astype(jnp.int32), gate_a,
                       jnp.where(col_id == slot_b.astype(jnp.int32), gate_b, 0.0)).astype(BF16)
    o_ref[...] = _layer_norm(alpha * x_ref[...] + _dot(unsort, ys_ref[...]), g_ref[...], b_ref[...])


def _moe(x1b, x1, logits, rb, wgu, wd, g, b, *, layer, alpha, tm=512):
    T = x1.shape[0]
    R = _moe_rows(tm)
    earlier = jnp.asarray(np.tril(np.ones((tm, tm), np.float32), -1), BF16)
    row = lambda w: pl.BlockSpec((tm, w), lambda i: (i, 0))
    full = lambda a: _layer_spec(a, layer)
    assert wgu.dtype == F32 and wd.dtype == F32, "the staging buffers are f32"
    in_hbm = pl.BlockSpec(memory_space=pl.ANY)
    return pl.pallas_call(
        functools.partial(_moe_kernel, alpha=alpha, layer=layer),
        grid=(T // tm,),
        in_specs=[row(D_MODEL), row(D_MODEL), row(LANES), pl.BlockSpec(rb.shape, lambda i: (0, 0)),
                  pl.BlockSpec(earlier.shape, lambda i: (0, 0)), in_hbm, in_hbm, full(g), full(b)],
        out_specs=row(D_MODEL),
        out_shape=jax.ShapeDtypeStruct((T, D_MODEL), F32),
        scratch_shapes=[pltpu.VMEM((R, D_MODEL), BF16), pltpu.VMEM((R, D_MODEL), BF16),
                        pltpu.VMEM(wgu.shape[1:], BF16), pltpu.VMEM(wd.shape[1:], BF16),
                        pltpu.VMEM((2,) + wgu.shape[2:], F32), pltpu.VMEM((2,) + wd.shape[2:], F32),
                        pltpu.SemaphoreType.DMA((2, 2)),
                        pltpu.SMEM((N_EXPERTS,), jnp.int32), pltpu.SMEM((N_EXPERTS,), jnp.int32)],
        compiler_params=_params("arbitrary"),
        name="moe",
    )(x1b, x1, logits, rb, earlier, wgu, wd, g, b)


def _rope_tables(seq):
    pos = np.arange(seq, dtype=np.float64)[:, None]
    lane = np.arange(LANES)

    def table(dim, active, offset):
        j = (lane - offset) % dim
        inv = ROPE_THETA ** (-(2.0 * (j % (dim // 2))) / dim)
        ang = pos * inv[None, :]
        sign = np.where(j < dim // 2, -1.0, 1.0)
        cos = np.where(active[None, :], np.cos(ang), 1.0)
        sin = np.where(active[None, :], np.sin(ang) * sign[None, :], 0.0)
        return jnp.asarray(cos, F32), jnp.asarray(sin, F32)

    cr, sr = table(RET_DK, np.ones(LANES, bool), 0)
    cm, sm = table(MLA_ROPE, (lane >= MLA_NOPE) & (lane < MLA_NOPE + MLA_ROPE), MLA_NOPE)
    return cr, sr, cm, sm


def _stacked_weights(w_in, mla_w_uq, mla_w_ukv, w_out):
    depth = w_in.shape[0]
    sizes = (256, 256, 256, 256, SWA_W, SWA_KV_W, SWA_KV_W, MLA_Q_RANK, MLA_KV_RANK, MLA_ROPE)
    o = [int(v) for v in np.concatenate([[0], np.cumsum(sizes)])]
    order = np.array(_SWA_HEAD_ORDER)
    pad_lanes = lambda a: jnp.pad(a, [(0, 0)] * (a.ndim - 1) + [(0, LANES - a.shape[-1])])

    sq = w_in[:, :, o[4]:o[5]].reshape(depth, D_MODEL, SWA_Q_HEADS, HEAD_DIM)[:, :, order]
    kr = jnp.pad(w_in[:, :, o[9]:o[10]], ((0, 0), (0, 0), (MLA_NOPE, LANES - MLA_NOPE - MLA_ROPE)))
    w1 = jnp.concatenate([w_in[:, :, :o[4]], sq.reshape(depth, D_MODEL, SWA_W), kr, w_in[:, :, o[5]:o[7]],
                          w_in[:, :, o[8]:o[9]], w_in[:, :, o[7]:o[8]]], 2).astype(BF16)

    dq = MLA_NOPE + MLA_ROPE
    wuq = pad_lanes(mla_w_uq.reshape(depth, MLA_Q_RANK, MLA_HEADS, dq)).reshape(depth, MLA_Q_RANK, MLA_PAD_W)
    ukv = mla_w_ukv.reshape(depth, MLA_KV_RANK, MLA_HEADS, MLA_NOPE + MLA_V)
    wuk = pad_lanes(ukv[..., :MLA_NOPE]).reshape(depth, MLA_KV_RANK, MLA_PAD_W)
    wuvt = jnp.swapaxes(pad_lanes(ukv[..., MLA_NOPE:]).reshape(depth, MLA_KV_RANK, MLA_PAD_W), 1, 2)

    wr = w_out[:, :RET_W]
    ws = w_out[:, RET_W:RET_W + SWA_W].reshape(depth, SWA_Q_HEADS, HEAD_DIM, D_MODEL)[:, order]
    wm = w_out[:, RET_W + SWA_W:].reshape(depth, MLA_HEADS, MLA_V, D_MODEL)
    wm = jnp.pad(wm, ((0, 0), (0, 0), (0, LANES - MLA_V), (0, 0)))
    to_bf16 = lambda a: a.astype(BF16)
    return (w1, to_bf16(wuq), to_bf16(wuk), to_bf16(wuvt), to_bf16(wr),
            to_bf16(ws.reshape(depth, SWA_W, D_MODEL)), to_bf16(wm.reshape(depth, MLA_PAD_W, D_MODEL)))


def kernel(x, w_in, ret_gn_w, swa_sinks, mla_q_norm_w, mla_kv_norm_w, mla_w_uq, mla_w_ukv, w_out,
           ln1_g, ln1_b, router_w, router_bias, exp_w_gate_up, exp_w_down, ln2_g, ln2_b):
    batch, seq, _ = x.shape
    depth = w_in.shape[0]
    alpha = (2 * depth) ** 0.25
    cr, sr, cm, sm = _rope_tables(seq)

    rw = jnp.pad(router_w, ((0, 0), (0, LANES - N_EXPERTS)))
    rwh = rw.astype(BF16)
    rw = jnp.concatenate([rwh, (rw - rwh.astype(F32)).astype(BF16)], 1)
    rb = router_bias.astype(F32).reshape(N_EXPERTS, 1)

    w1, wuq, wuk, wuvt, wr, ws, wm = _stacked_weights(w_in, mla_w_uq, mla_w_ukv, w_out)
    rows = lambda a: a.astype(F32).reshape(depth, 1, -1)
    qnw, kvnw, gnw = rows(mla_q_norm_w), rows(mla_kv_norm_w), rows(ret_gn_w)
    g1, b1, g2, b2 = rows(ln1_g), rows(ln1_b), rows(ln2_g), rows(ln2_b)
    sink_row = rows(jnp.repeat(swa_sinks, SWA_WINDOW, axis=1))

    t = x.reshape(batch * seq, D_MODEL)
    for l in range(depth):
        rq, rk, rv, rg, sq, sk, mq, mk, svt, mvt = _inproj(
            t, w1, cr, sr, cm, sm, qnw, kvnw, wuq, wuk, wuvt, layer=l, seq=seq, tm=MLA_TILE)
        ret_o = _retention(rq, rk, rv, rg, gnw, layer=l, batch=batch, seq=seq)
        swa_o = _swa(sq, sk, svt, sink_row, layer=l, seq=seq, blk=MLA_TILE)
        mla_o = _mla(mq, mk, mvt, batch=batch, seq=seq, tk=MLA_TILE)
        x1, x1b, logits = _outproj(ret_o, swa_o, mla_o, t, wr, ws, wm, g1, b1, rw, layer=l, alpha=alpha)
        t = _moe(x1b, x1, logits, rb, exp_w_gate_up, exp_w_down, g2, b2, layer=l, alpha=alpha)
    return t.reshape(batch, seq, D_MODEL)
```

```python
import functools
import math

import numpy as np
import jax
import jax.numpy as jnp
from jax import lax
from jax.experimental import pallas as pl
from jax.experimental.pallas import tpu as pltpu

D_MODEL = 1024
HEAD_DIM = 64
ROPE_THETA = 10000.0

RET_HEADS = 4
RET_DK = 64
RET_DV = 64
RET_W = RET_HEADS * RET_DV

SWA_Q_HEADS = 6
SWA_KV_HEADS = 2
SWA_WINDOW = 128
SWA_W = SWA_Q_HEADS * HEAD_DIM
SWA_KV_W = SWA_KV_HEADS * HEAD_DIM

MLA_HEADS = 6
MLA_Q_RANK = 384
MLA_KV_RANK = 256
MLA_NOPE = 64
MLA_ROPE = 32
MLA_V = 64

N_EXPERTS = 16
N_GROUPS = 4
EXPERTS_PER_GROUP = N_EXPERTS // N_GROUPS
D_EXPERT = 256

LN_EPS = 1e-5
RMS_EPS = 1e-6
NEG_INF = -1e30

LANES = 128
MLA_PAD_W = MLA_HEADS * LANES
MLA_TILE = 512
VMEM_LIMIT = 56 * 1024 * 1024

F32 = jnp.float32
BF16 = jnp.bfloat16

_C_RQ, _C_RK, _C_RV, _C_RG = 0, 256, 512, 768
_C_SQ, _C_KR, _C_SK, _C_SV = 1024, 1408, 1536, 1664
_C_CKV, _C_CQ, _C_END = 1792, 2048, 2432

_SWA_HEAD_ORDER = (0, 3, 1, 4, 2, 5)


def _params(*sem, flags=None):
    return pltpu.CompilerParams(dimension_semantics=sem, vmem_limit_bytes=VMEM_LIMIT, flags=flags)


def _layer_spec(a, layer, **kwargs):
    return pl.BlockSpec((None,) + a.shape[1:], lambda *_: (layer,) + (0,) * (a.ndim - 1), **kwargs)


def _dot(a, b):
    return jnp.dot(a, b, preferred_element_type=F32)


def _dot_nt(a, b):
    return lax.dot_general(a, b, (((1,), (1,)), ((), ())), preferred_element_type=F32)


def _dot_tn(a, b):
    return lax.dot_general(a, b, (((0,), (0,)), ((), ())), preferred_element_type=F32)


def _split_dot(x, w_bf16):
    hi = x.astype(BF16)
    lo = (x - hi.astype(F32)).astype(BF16)
    return _dot(hi, w_bf16) + _dot(lo, w_bf16)


def _rope(h, cos, sin_signed, half):
    fwd = pltpu.roll(h, LANES - half, 1)
    bwd = pltpu.roll(h, half, 1)
    lane = lax.broadcasted_iota(jnp.int32, h.shape, 1)
    rot = jnp.where((lane % (2 * half)) < half, fwd, bwd)
    return h * cos + rot * sin_signed


def _layer_norm(y, g, b):
    mu = jnp.mean(y, -1, keepdims=True)
    d = y - mu
    var = jnp.mean(d * d, -1, keepdims=True)
    return d * lax.rsqrt(var + LN_EPS) * g + b


def _rms_norm(x, g):
    return x * lax.rsqrt(jnp.mean(x * x, -1, keepdims=True) + RMS_EPS) * g


def _inproj_kernel(x_ref, w1_ref, cr_ref, sr_ref, cm_ref, sm_ref, qnw_ref, kvnw_ref,
                   wuq_ref, wuk_ref, wuvt_ref,
                   rq_ref, rk_ref, rv_ref, rg_ref, sq_ref, sk_ref, mq_ref, mk_ref, svt_ref, mvt_ref,
                   *, mla_q_scale):
    tile = mvt_ref.shape[2]
    for s in range(x_ref.shape[0] // tile):
        _inproj_tile(slice(s * tile, (s + 1) * tile), s, x_ref, w1_ref, cr_ref, sr_ref, cm_ref, sm_ref,
                     qnw_ref, kvnw_ref, wuq_ref, wuk_ref, wuvt_ref, rq_ref, rk_ref, rv_ref, rg_ref,
                     sq_ref, sk_ref, mq_ref, mk_ref, svt_ref, mvt_ref, mla_q_scale)


def _inproj_tile(rows, s, x_ref, w1_ref, cr_ref, sr_ref, cm_ref, sm_ref, qnw_ref, kvnw_ref,
                 wuq_ref, wuk_ref, wuvt_ref, rq_ref, rk_ref, rv_ref, rg_ref, sq_ref, sk_ref, mq_ref,
                 mk_ref, svt_ref, mvt_ref, mla_q_scale):
    xb = x_ref[rows, :].astype(BF16)

    def proj(a, b):
        return _dot(xb, w1_ref[:, a:b])

    cq = proj(_C_CQ, _C_END)
    ckv = proj(_C_CKV, _C_CQ)
    rq, rk = proj(_C_RQ, _C_RK), proj(_C_RK, _C_RV)
    rv, rg = proj(_C_RV, _C_RG), proj(_C_RG, _C_SQ)
    sq_kr = proj(_C_SQ, _C_SK)
    sk_sv = proj(_C_SK, _C_CKV)
    cq = _rms_norm(cq, qnw_ref[...]).astype(BF16)
    ckv = _rms_norm(ckv, kvnw_ref[...]).astype(BF16)
    q = _dot(cq, wuq_ref[...])
    kn = _dot(ckv, wuk_ref[...])
    vt = _dot_nt(wuvt_ref[...], ckv)

    cr, sr = cr_ref[rows, :], sr_ref[rows, :]
    for g in range(RET_W // LANES):
        lanes = slice(g * LANES, (g + 1) * LANES)
        rq_ref[rows, lanes] = _rope(rq[:, lanes], cr, sr, RET_DK // 2).astype(BF16)
        rk_ref[rows, lanes] = (_rope(rk[:, lanes], cr, sr, RET_DK // 2) * (RET_DK ** -0.5)).astype(BF16)
    rv_ref[rows, :] = rv.astype(BF16)
    rg_ref[rows, :] = rg.astype(BF16)
    sq_ref[rows, :] = (sq_kr[:, :SWA_W] * (HEAD_DIM ** -0.5 * math.log2(math.e))).astype(BF16)
    sk_ref[rows, :] = sk_sv[:, :SWA_KV_W].astype(BF16)
    svt_ref[s] = sk_sv[:, SWA_KV_W:].T.astype(BF16)

    cm, sm = cm_ref[rows, :], sm_ref[rows, :]
    kpe = _rope(sq_kr[:, SWA_W:], cm, sm, MLA_ROPE // 2)
    for h in range(MLA_HEADS):
        lo, hi = h * LANES, (h + 1) * LANES
        mq_ref[rows, lo:hi] = (_rope(q[:, lo:hi], cm, sm, MLA_ROPE // 2) * mla_q_scale).astype(BF16)
        mk_ref[rows, lo:hi] = (kn[:, lo:hi] + kpe).astype(BF16)
    feat = lax.broadcasted_iota(jnp.int32, vt.shape, 0) % LANES
    mvt_ref[s] = (vt + (feat == MLA_V).astype(F32)).astype(BF16)


def _inproj(x, w1, cr, sr, cm, sm, qnw, kvnw, wuq, wuk, wuvt, *, layer, seq, tile, tiles_per_step=2):
    T = x.shape[0]
    tm = tile * tiles_per_step
    nt = T // tm
    npos = seq // tm
    row = lambda w: pl.BlockSpec((tm, w), lambda i: (i, 0))
    pos = lambda: pl.BlockSpec((tm, LANES), lambda i: (i % npos, 0))
    full = lambda a: _layer_spec(a, layer)
    widths = (RET_W, RET_W, RET_W, RET_W, SWA_W, SWA_KV_W, MLA_PAD_W, MLA_PAD_W)
    widths_t = (SWA_KV_W, MLA_PAD_W)
    mla_q_scale = (MLA_NOPE + MLA_ROPE) ** -0.5 * math.log2(math.e)
    return pl.pallas_call(
        functools.partial(_inproj_kernel, mla_q_scale=mla_q_scale),
        grid=(nt,),
        in_specs=[row(D_MODEL), full(w1), pos(), pos(), pos(), pos(), full(qnw), full(kvnw),
                  full(wuq), full(wuk), full(wuvt)],
        out_specs=[row(w) for w in widths]
        + [pl.BlockSpec((tiles_per_step, w, tile), lambda i: (i, 0, 0)) for w in widths_t],
        out_shape=[jax.ShapeDtypeStruct((T, w), BF16) for w in widths]
        + [jax.ShapeDtypeStruct((T // tile, w, tile), BF16) for w in widths_t],
        compiler_params=_params("parallel"),
        name="inproj",
    )(x, w1, cr, sr, cm, sm, qnw, kvnw, wuq, wuk, wuvt)


def _retention_kernel(q_ref, k_ref, v_ref, g_ref, gnw_ref, dec_ref, xi_ref, zeta_ref, dm_ref, bm_ref,
                      avg_ref, o_ref, state_ref, *, chunk, n_chunks):
    @pl.when(pl.program_id(1) == 0)
    def _():
        state_ref[...] = jnp.zeros_like(state_ref)

    lane_head = lax.broadcasted_iota(jnp.int32, (chunk, RET_W), 1) // RET_DV
    avg = avg_ref[...]
    chunks = [slice(c * chunk, (c + 1) * chunk) for c in range(n_chunks)]

    def raw_scores(rows):
        q = q_ref[rows, :]
        zero = jnp.zeros_like(q)
        q_heads = jnp.concatenate([jnp.where(lane_head == h, q, zero) for h in range(RET_HEADS)], 0)
        return _dot_nt(q_heads, k_ref[rows, :])

    scores = [raw_scores(rows) for rows in chunks]
    updates = [_dot_tn((k_ref[rows, :].astype(F32) * zeta_ref[...]).astype(BF16), v_ref[rows, :])
               for rows in chunks]
    inner = [_dot((s * dec_ref[...]).astype(BF16), v_ref[rows, :]) for s, rows in zip(scores, chunks)]

    state = state_ref[...]
    outs = []
    for rows, upd, inner_heads in zip(chunks, updates, inner):
        o = _dot((q_ref[rows, :].astype(F32) * xi_ref[...]).astype(BF16), state.astype(BF16))
        for h in range(RET_HEADS):
            o = o + jnp.where(lane_head == h, inner_heads[h * chunk:(h + 1) * chunk], 0.0)
        outs.append(o)
        state = dm_ref[...] * state + bm_ref[...] * upd
    state_ref[...] = state

    mus = [_split_dot(o, avg) for o in outs]
    devs = [o - mu for o, mu in zip(outs, mus)]
    variances = [_split_dot(d * d, avg) for d in devs]
    for rows, d, var in zip(chunks, devs, variances):
        on = d * lax.rsqrt(var + LN_EPS) * gnw_ref[...]
        gate = g_ref[rows, :].astype(F32)
        o_ref[rows, :] = (on * (gate * jax.nn.sigmoid(gate))).astype(BF16)


def _retention_tables(chunk):
    H = RET_HEADS
    gamma = 1.0 - 2.0 ** (-5.0 - np.arange(H, dtype=np.float64))
    log_g = np.log(gamma)
    idx = np.arange(chunk, dtype=np.float64)
    rel = idx[:, None] - idx[None, :]
    dec = np.where(rel[None] >= 0, np.exp(np.maximum(rel, 0.0)[None] * log_g[:, None, None]), 0.0)
    dec = dec.reshape(H * chunk, chunk)
    lane_head = np.arange(RET_W) // RET_DV
    xi = np.exp((idx[:, None] + 1.0) * log_g[lane_head][None, :])
    zeta = np.exp((chunk - 1.0 - idx[:, None]) * log_g[lane_head][None, :])
    same = lane_head[:, None] == lane_head[None, :]
    dm = np.where(same, np.exp(chunk * log_g)[lane_head][:, None], 0.0)
    bm = same.astype(np.float64)
    avg = bm / RET_DV
    f = lambda a: jnp.asarray(a, F32)
    return f(dec), f(xi), f(zeta), f(dm), f(bm), jnp.asarray(avg, BF16)


def _retention(rq, rk, rv, rg, gnw, *, layer, batch, seq, blk=1024, chunk=128):
    T = rq.shape[0]
    nb = seq // blk
    tables = _retention_tables(chunk)
    row = pl.BlockSpec((blk, RET_W), lambda b, i: (b * nb + i, 0))
    full = lambda a: pl.BlockSpec(a.shape, lambda b, i: (0, 0))
    return pl.pallas_call(
        functools.partial(_retention_kernel, chunk=chunk, n_chunks=blk // chunk),
        grid=(batch, nb),
        in_specs=[row, row, row, row, _layer_spec(gnw, layer)] + [full(t) for t in tables],
        out_specs=row,
        out_shape=jax.ShapeDtypeStruct((T, RET_W), BF16),
        scratch_shapes=[pltpu.VMEM((RET_W, RET_W), F32)],
        compiler_params=_params("parallel", "arbitrary"),
        name="retention",
    )(rq, rk, rv, rg, gnw, *tables)


def _swa_kernel(q_ref, kc_ref, kp_ref, vtc_ref, vtp_ref, sink_ref, o_ref, *, blk, seq):
    L = SWA_WINDOW
    n_pairs = SWA_Q_HEADS // 2
    n_sub = blk // L
    lane_lo = lax.broadcasted_iota(jnp.int32, (L, LANES), 1) < HEAD_DIM
    feat_lo = lax.broadcasted_iota(jnp.int32, (LANES, L), 0) < HEAD_DIM
    key = lax.broadcasted_iota(jnp.int32, (2 * L, SWA_Q_HEADS * L), 0)
    qry = lax.broadcasted_iota(jnp.int32, (2 * L, SWA_Q_HEADS * L), 1) % L
    band = (key > qry) & (key <= qry + L)
    sink = sink_ref[...] * math.log2(math.e)

    def scores(sb):
        r0 = sb * L
        k_prev = kp_ref[...] if sb == 0 else kc_ref[r0 - L:r0, :]
        k_win = jnp.concatenate([k_prev, kc_ref[r0:r0 + L, :]], 0)
        pairs = [q_ref[r0:r0 + L, g * LANES:(g + 1) * LANES] for g in range(n_pairs)]
        zero = jnp.zeros_like(pairs[0])
        q_heads = jnp.concatenate([jnp.where(lane_lo, p, zero) for p in pairs]
                                  + [jnp.where(lane_lo, zero, p) for p in pairs], 0)
        return _dot_nt(k_win, q_heads)

    def attend(sb, st):
        r0 = sb * L
        valid = band
        if sb == 0:
            pos0 = (pl.program_id(0) * blk) % seq
            valid = band & (key >= jnp.where(pos0 > 0, 0, L))
        st = jnp.where(valid, st, NEG_INF)
        m = jnp.maximum(jnp.max(st, 0, keepdims=True), sink)
        p = jnp.exp2(st - m)
        denom = jnp.sum(p, 0, keepdims=True) + jnp.exp2(sink - m)
        c, off = divmod(r0, vtc_ref.shape[2])
        if off == 0:
            vt_prev = vtp_ref[0] if c == 0 else vtc_ref[c - 1, :, vtc_ref.shape[2] - L:]
            vt_win = jnp.concatenate([vt_prev, vtc_ref[c, :, :L]], 1)
        else:
            vt_win = vtc_ref[c, :, off - L:off + L]
        ot = _dot(vt_win, p.astype(BF16)) * (1.0 / denom)
        for g in range(n_pairs):
            o_lo = ot[:, g * L:(g + 1) * L]
            o_hi = ot[:, (g + n_pairs) * L:(g + n_pairs + 1) * L]
            o_ref[r0:r0 + L, g * LANES:(g + 1) * LANES] = jnp.where(feat_lo, o_lo, o_hi).T.astype(BF16)

    st = scores(0)
    for sb in range(n_sub):
        st_next = scores(sb + 1) if sb + 1 < n_sub else None
        attend(sb, st)
        st = st_next


def _swa(sq, sk, svt, sink_row, *, layer, seq, blk):
    T = sq.shape[0]
    L = SWA_WINDOW
    per = blk // L
    tile = svt.shape[2]
    tiles = blk // tile
    cur = lambda w: pl.BlockSpec((blk, w), lambda i: (i, 0))
    k_prev = pl.BlockSpec((L, SWA_KV_W), lambda i: (jnp.maximum(i * per - 1, 0), 0))
    vt_cur = pl.BlockSpec((tiles, SWA_KV_W, tile), lambda i: (i, 0, 0))
    vt_prev = pl.BlockSpec((1, SWA_KV_W, L), lambda i: (jnp.maximum(i * tiles - 1, 0), 0, tile // L - 1))
    return pl.pallas_call(
        functools.partial(_swa_kernel, blk=blk, seq=seq),
        grid=(T // blk,),
        in_specs=[cur(SWA_W), cur(SWA_KV_W), k_prev, vt_cur, vt_prev, _layer_spec(sink_row, layer)],
        out_specs=cur(SWA_W),
        out_shape=jax.ShapeDtypeStruct((T, SWA_W), BF16),
        compiler_params=_params("parallel"),
        name="swa",
    )(sq, sk, sk, svt, svt, sink_row)


def _mla_kernel(q_ref, k_ref, vt_ref, o_ref, st0_ref, st1_ref, m_ref, acc_ref, *, tq, tk, heads):
    i = pl.program_id(2)
    head_lanes = [slice(h * LANES, (h + 1) * LANES) for h in range(heads)]
    st_refs = (st0_ref, st1_ref)
    all_q = slice(0, tq)
    late_q = slice(tk, tq)

    def scores(j, lanes, qs):
        start = pl.multiple_of(j * tk, tk)
        return _dot_nt(k_ref[pl.ds(start, tk), lanes], q_ref[qs, lanes])

    def step(j, cur, qs=all_q, diagonal=False, next_qs=all_q):
        for h, lanes in enumerate(head_lanes):
            if next_qs is not None:
                st_refs[1 - cur][h, :, next_qs] = scores(j + 1, lanes, next_qs)
            st = st_refs[cur][h, :, qs]
            if diagonal:
                key = lax.broadcasted_iota(jnp.int32, st.shape, 0)
                qry = lax.broadcasted_iota(jnp.int32, st.shape, 1)
                st = jnp.where(key <= qry, st, NEG_INF)
            m = m_ref[h][:, qs]
            m_new = jnp.maximum(m, jnp.max(st, 0, keepdims=True))
            pt = jnp.exp2(st - m_new).astype(BF16)
            acc_ref[h, :, qs] = jnp.exp2(m - m_new) * acc_ref[h, :, qs] + _dot(vt_ref[j, lanes, :], pt)
            if qs is all_q:
                m_ref[h] = m_new

    m_ref[...] = jnp.full(m_ref.shape, NEG_INF, F32)
    acc_ref[...] = jnp.zeros(acc_ref.shape, F32)
    for h, lanes in enumerate(head_lanes):
        st0_ref[h] = scores(0, lanes, all_q)

    def pair(p, _):
        step(2 * p, 0)
        step(2 * p + 1, 1)
        return 0

    lax.fori_loop(0, i, pair, 0)
    step(2 * i, 0, diagonal=True, next_qs=late_q)
    step(2 * i + 1, 1, qs=late_q, diagonal=True, next_qs=None)
    for h in range(heads):
        acc = acc_ref[h]
        o_ref[:, h * LANES:(h + 1) * LANES] = (acc / acc[MLA_V:MLA_V + 1, :]).T.astype(BF16)


def _mla(mq, mk, mvt, *, batch, seq, tk, heads=3):
    T = mq.shape[0]
    tq = 2 * tk
    nq = seq // tq
    qo = pl.BlockSpec((tq, heads * LANES), lambda b, h, i: (b * nq + i, h))
    k = pl.BlockSpec((seq, heads * LANES), lambda b, h, i: (b, h))
    vt = pl.BlockSpec((seq // tk, heads * LANES, tk), lambda b, h, i: (b, h, 0))
    return pl.pallas_call(
        functools.partial(_mla_kernel, tq=tq, tk=tk, heads=heads),
        grid=(batch, MLA_HEADS // heads, nq),
        in_specs=[qo, k, vt],
        out_specs=qo,
        out_shape=jax.ShapeDtypeStruct((T, MLA_PAD_W), BF16),
        scratch_shapes=[pltpu.VMEM((heads, tk, tq), F32), pltpu.VMEM((heads, tk, tq), F32),
                        pltpu.VMEM((heads, 1, tq), F32), pltpu.VMEM((heads, LANES, tq), F32)],
        compiler_params=_params("parallel", "parallel", "arbitrary"),
        name="mla",
    )(mq, mk, mvt)


def _outproj_kernel(ret_ref, swa_ref, mla_ref, x_ref, wr_ref, ws_ref, wm_ref, g_ref, b_ref,
                    rw_ref, x1_ref, x1b_ref, logits_ref, *, alpha, n_sub):
    tm = x_ref.shape[0]
    subs = [slice(s * tm // n_sub, (s + 1) * tm // n_sub) for s in range(n_sub)]

    def mix(rows):
        return (_dot(ret_ref[rows, :], wr_ref[...]) + _dot(swa_ref[rows, :], ws_ref[...])
                + _dot(mla_ref[rows, :], wm_ref[...]))

    def finish(rows, mixed):
        x1 = _layer_norm(alpha * x_ref[rows, :] + mixed, g_ref[...], b_ref[...])
        x1_ref[rows, :] = x1
        hi = x1.astype(BF16)
        x1b_ref[rows, :] = hi
        lo = (x1 - hi.astype(F32)).astype(BF16)
        hi_part = _dot(hi, rw_ref[...])
        logits_ref[rows, :] = hi_part[:, :LANES] + hi_part[:, LANES:] + _dot(lo, rw_ref[:, :LANES])

    mixed = mix(subs[0])
    for s in range(n_sub):
        mixed_next = mix(subs[s + 1]) if s + 1 < n_sub else None
        finish(subs[s], mixed)
        mixed = mixed_next


def _outproj(ret_o, swa_o, mla_o, x, wr, ws, wm, g, b, rw, *, layer, alpha, tm=1024, n_sub=8):
    T = x.shape[0]
    row = lambda w: pl.BlockSpec((tm, w), lambda i: (i, 0))
    full = lambda a: pl.BlockSpec(a.shape, lambda i: (0, 0))
    per_layer = lambda a: _layer_spec(a, layer)
    return pl.pallas_call(
        functools.partial(_outproj_kernel, alpha=alpha, n_sub=n_sub),
        grid=(T // tm,),
        in_specs=[row(RET_W), row(SWA_W), row(MLA_PAD_W), row(D_MODEL), per_layer(wr), per_layer(ws),
                  per_layer(wm), per_layer(g), per_layer(b), full(rw)],
        out_specs=[row(D_MODEL), row(D_MODEL), row(LANES)],
        out_shape=[jax.ShapeDtypeStruct((T, D_MODEL), F32), jax.ShapeDtypeStruct((T, D_MODEL), BF16),
                   jax.ShapeDtypeStruct((T, LANES), F32)],
        compiler_params=_params("parallel"),
        name="outproj",
    )(ret_o, swa_o, mla_o, x, wr, ws, wm, g, b, rw)


MOE_CHUNK = 128
MOE_SEG_ALIGN = 16


def _moe_rows(tm):
    return -(-(2 * tm + N_EXPERTS * (MOE_SEG_ALIGN - 1) + MOE_CHUNK) // LANES) * LANES


def _ring_partner(x, d, width):
    n = x.shape[0]
    ahead = pltpu.roll(x, n - d, 0)
    if width == n:
        return ahead
    row = lax.broadcasted_iota(jnp.int32, x.shape, 0)
    return jnp.where((row % width) + d < width, ahead, pltpu.roll(x, width - d, 0))


def _router_gates_t(logits_t, bias_col):
    row = lax.broadcasted_iota(jnp.int32, logits_t.shape, 0)
    scores = jax.nn.sigmoid(logits_t)
    biased = scores + bias_col
    G = EXPERTS_PER_GROUP
    member = row % G
    rank = jnp.zeros(logits_t.shape, jnp.int32)
    for d in range(1, G):
        other = _ring_partner(biased, d, G)
        other_first = (member + d) % G < member
        rank = rank + ((other > biased) | ((other == biased) & other_first)).astype(jnp.int32)
    top2 = rank < 2
    part = jnp.where(top2, biased, 0.0)
    grp_score = part
    for d in range(1, G):
        grp_score = grp_score + _ring_partner(part, d, G)
    beaten = jnp.zeros(logits_t.shape, jnp.bool_)
    grp = row // G
    for dg in range(1, N_GROUPS):
        other = _ring_partner(grp_score, dg * G, N_EXPERTS)
        other_first = (grp + dg) % N_GROUPS < grp
        beaten = beaten | (other > grp_score) | ((other == grp_score) & other_first)
    picked = jnp.where(top2 & jnp.logical_not(beaten), scores, 0.0)
    return picked / jnp.sum(picked, 0, keepdims=True)


def _moe_kernel(xb_ref, x_ref, logits_ref, rb_ref, earlier_ref, wgu_hbm, wd_hbm, g_ref, b_ref, o_ref,
                xs_ref, ys_ref, wgu_ref, wd_ref, gu_stage, d_stage, sem, start_ref, chunks_ref,
                *, alpha, layer):
    tm = xb_ref.shape[0]
    R = xs_ref.shape[0]
    big = float(4 * R)

    @pl.when(pl.program_id(0) == 0)
    def _():
        def expert_copies(e, slot):
            return (pltpu.make_async_copy(wgu_hbm.at[layer, e], gu_stage.at[slot], sem.at[0, slot]),
                    pltpu.make_async_copy(wd_hbm.at[layer, e], d_stage.at[slot], sem.at[1, slot]))

        for copy in expert_copies(0, 0):
            copy.start()
        for e in range(N_EXPERTS):
            slot = e % 2
            if e + 1 < N_EXPERTS:
                for copy in expert_copies(e + 1, 1 - slot):
                    copy.start()
            for copy in expert_copies(e, slot):
                copy.wait()
            wgu_ref[e] = gu_stage[slot].astype(BF16)
            wd_ref[e] = d_stage[slot].astype(BF16)

    experts_t = _router_gates_t(logits_ref[...].T[:N_EXPERTS], rb_ref[...])
    gates_t = jnp.concatenate([experts_t, jnp.zeros((LANES - N_EXPERTS, tm), F32)], 0)
    gates = gates_t.T

    def strict_upper(n):
        return lax.broadcasted_iota(jnp.int32, (n, n), 0) < lax.broadcasted_iota(jnp.int32, (n, n), 1)

    def seg_pad(count):
        return jnp.floor((count + (MOE_SEG_ALIGN - 1.0)) * (1.0 / MOE_SEG_ALIGN)) * MOE_SEG_ALIGN

    one = lambda mask: jnp.where(mask, 1.0, 0.0).astype(BF16)

    sel = (gates > 0.0) & (lax.broadcasted_iota(jnp.int32, gates.shape, 1) < N_EXPERTS)
    earlier = earlier_ref[...]
    rank = _dot(earlier, one(sel))
    count = jnp.sum(jnp.where(sel, 1.0, 0.0), 0, keepdims=True)
    pad = jnp.broadcast_to(seg_pad(count), (8, LANES)).astype(BF16)
    offset = _dot(pad, one(strict_upper(LANES)))[0:1]
    slot = offset + rank
    slot_a = jnp.min(jnp.where(sel, slot, big), -1, keepdims=True)
    slot_b = jnp.max(jnp.where(sel, slot, -1.0), -1, keepdims=True)
    gate_a = jnp.sum(jnp.where(sel & (slot == slot_a), gates, 0.0), -1, keepdims=True)
    gate_b = jnp.sum(jnp.where(sel & (slot == slot_b), gates, 0.0), -1, keepdims=True)

    sel_t = experts_t > 0.0
    rank_t = _dot_nt(one(sel_t), earlier)
    on_diag = (lax.broadcasted_iota(jnp.int32, (N_EXPERTS, LANES), 0)
               == lax.broadcasted_iota(jnp.int32, (N_EXPERTS, LANES), 1))
    offset_col = jnp.sum(jnp.where(on_diag, offset, 0.0), -1, keepdims=True)
    slot_t = offset_col + rank_t
    slot_a_t = jnp.min(jnp.where(sel_t, slot_t, big), 0, keepdims=True)
    slot_b_t = jnp.max(jnp.where(sel_t, slot_t, -1.0), 0, keepdims=True)

    row_id = lax.broadcasted_iota(jnp.int32, (R, tm), 0)
    sort = one((row_id == slot_a_t.astype(jnp.int32)) | (row_id == slot_b_t.astype(jnp.int32)))
    xs_ref[...] = _dot(sort, xb_ref[...]).astype(BF16)

    @pl.when(pl.program_id(0) == 0)
    def _():
        ys_ref[...] = jnp.zeros_like(ys_ref)

    offset_i = offset.astype(jnp.int32)
    chunks_i = jnp.floor((count + (MOE_CHUNK - 1.0)) * (1.0 / MOE_CHUNK)).astype(jnp.int32)
    for e in range(N_EXPERTS):
        start_ref[e] = offset_i[0, e]
        chunks_ref[e] = chunks_i[0, e]

    def chunk_rows(e, c):
        return pl.ds(pl.multiple_of(start_ref[e] + c * MOE_CHUNK, MOE_SEG_ALIGN), MOE_CHUNK)

    def activation(e, rows):
        h = _dot(xs_ref[rows, :], wgu_ref[e])
        up, lin = h[:, :D_EXPERT], h[:, D_EXPERT:]
        return (up * jax.nn.sigmoid(up) * lin).astype(BF16)

    def project(e, rows, a):
        ys_ref[rows, :] = _dot(a, wd_ref[e]).astype(BF16)

    def later_chunks(e, _):
        def body(c, _):
            rows = chunk_rows(e, c)
            project(e, rows, activation(e, rows))
            return 0
        return lax.fori_loop(1, chunks_ref[e], body, 0)

    lax.fori_loop(0, N_EXPERTS, later_chunks, 0)

    a = activation(0, chunk_rows(0, 0))
    for e in range(N_EXPERTS):
        a_next = activation(e + 1, chunk_rows(e + 1, 0)) if e + 1 < N_EXPERTS else None
        project(e, chunk_rows(e, 0), a)
        a = a_next

    col_id = lax.broadcasted_iota(jnp.int32, (tm, R), 1)
    unsort = jnp.where(col_id == slot_a.astype(jnp.int32), gate_a,
                       jnp.where(col_id == slot_b.astype(jnp.int32), gate_b, 0.0)).astype(BF16)
    o_ref[...] = _layer_norm(alpha * x_ref[...] + _dot(unsort, ys_ref[...]), g_ref[...], b_ref[...])


def _moe(x1b, x1, logits, rb, wgu, wd, g, b, *, layer, alpha, tm=512):
    T = x1.shape[0]
    R = _moe_rows(tm)
    earlier = jnp.asarray(np.tril(np.ones((tm, tm), np.float32), -1), BF16)
    row = lambda w: pl.BlockSpec((tm, w), lambda i: (i, 0))
    full = lambda a: _layer_spec(a, layer)
    assert wgu.dtype == F32 and wd.dtype == F32, "the staging buffers are f32"
    in_hbm = pl.BlockSpec(memory_space=pl.ANY)
    return pl.pallas_call(
        functools.partial(_moe_kernel, alpha=alpha, layer=layer),
        grid=(T // tm,),
        in_specs=[row(D_MODEL), row(D_MODEL), row(LANES), pl.BlockSpec(rb.shape, lambda i: (0, 0)),
                  pl.BlockSpec(earlier.shape, lambda i: (0, 0)), in_hbm, in_hbm, full(g), full(b)],
        out_specs=row(D_MODEL),
        out_shape=jax.ShapeDtypeStruct((T, D_MODEL), F32),
        scratch_shapes=[pltpu.VMEM((R, D_MODEL), BF16), pltpu.VMEM((R, D_MODEL), BF16),
                        pltpu.VMEM(wgu.shape[1:], BF16), pltpu.VMEM(wd.shape[1:], BF16),
                        pltpu.VMEM((2,) + wgu.shape[2:], F32), pltpu.VMEM((2,) + wd.shape[2:], F32),
                        pltpu.SemaphoreType.DMA((2, 2)),
                        pltpu.SMEM((N_EXPERTS,), jnp.int32), pltpu.SMEM((N_EXPERTS,), jnp.int32)],
        compiler_params=_params("arbitrary"),
        name="moe",
    )(x1b, x1, logits, rb, earlier, wgu, wd, g, b)


def _rope_tables(seq):
    pos = np.arange(seq, dtype=np.float64)[:, None]
    lane = np.arange(LANES)

    def table(dim, active, offset):
        j = (lane - offset) % dim
        inv = ROPE_THETA ** (-(2.0 * (j % (dim // 2))) / dim)
        ang = pos * inv[None, :]
        sign = np.where(j < dim // 2, -1.0, 1.0)
        cos = np.where(active[None, :], np.cos(ang), 1.0)
        sin = np.where(active[None, :], np.sin(ang) * sign[None, :], 0.0)
        return jnp.asarray(cos, F32), jnp.asarray(sin, F32)

    cr, sr = table(RET_DK, np.ones(LANES, bool), 0)
    cm, sm = table(MLA_ROPE, (lane >= MLA_NOPE) & (lane < MLA_NOPE + MLA_ROPE), MLA_NOPE)
    return cr, sr, cm, sm


def _stacked_weights(w_in, mla_w_uq, mla_w_ukv, w_out):
    depth = w_in.shape[0]
    sizes = (256, 256, 256, 256, SWA_W, SWA_KV_W, SWA_KV_W, MLA_Q_RANK, MLA_KV_RANK, MLA_ROPE)
    o = [int(v) for v in np.concatenate([[0], np.cumsum(sizes)])]
    order = np.array(_SWA_HEAD_ORDER)
    pad_lanes = lambda a: jnp.pad(a, [(0, 0)] * (a.ndim - 1) + [(0, LANES - a.shape[-1])])

    sq = w_in[:, :, o[4]:o[5]].reshape(depth, D_MODEL, SWA_Q_HEADS, HEAD_DIM)[:, :, order]
    kr = jnp.pad(w_in[:, :, o[9]:o[10]], ((0, 0), (0, 0), (MLA_NOPE, LANES - MLA_NOPE - MLA_ROPE)))
    w1 = jnp.concatenate([w_in[:, :, :o[4]], sq.reshape(depth, D_MODEL, SWA_W), kr, w_in[:, :, o[5]:o[7]],
                          w_in[:, :, o[8]:o[9]], w_in[:, :, o[7]:o[8]]], 2).astype(BF16)

    dq = MLA_NOPE + MLA_ROPE
    wuq = pad_lanes(mla_w_uq.reshape(depth, MLA_Q_RANK, MLA_HEADS, dq)).reshape(depth, MLA_Q_RANK, MLA_PAD_W)
    ukv = mla_w_ukv.reshape(depth, MLA_KV_RANK, MLA_HEADS, MLA_NOPE + MLA_V)
    wuk = pad_lanes(ukv[..., :MLA_NOPE]).reshape(depth, MLA_KV_RANK, MLA_PAD_W)
    wuvt = jnp.swapaxes(pad_lanes(ukv[..., MLA_NOPE:]).reshape(depth, MLA_KV_RANK, MLA_PAD_W), 1, 2)

    wr = w_out[:, :RET_W]
    ws = w_out[:, RET_W:RET_W + SWA_W].reshape(depth, SWA_Q_HEADS, HEAD_DIM, D_MODEL)[:, order]
    wm = w_out[:, RET_W + SWA_W:].reshape(depth, MLA_HEADS, MLA_V, D_MODEL)
    wm = jnp.pad(wm, ((0, 0), (0, 0), (0, LANES - MLA_V), (0, 0)))
    to_bf16 = lambda a: a.astype(BF16)
    return (w1, to_bf16(wuq), to_bf16(wuk), to_bf16(wuvt), to_bf16(wr),
            to_bf16(ws.reshape(depth, SWA_W, D_MODEL)), to_bf16(wm.reshape(depth, MLA_PAD_W, D_MODEL)))


def kernel(x, w_in, ret_gn_w, swa_sinks, mla_q_norm_w, mla_kv_norm_w, mla_w_uq, mla_w_ukv, w_out,
           ln1_g, ln1_b, router_w, router_bias, exp_w_gate_up, exp_w_down, ln2_g, ln2_b):
    batch, seq, _ = x.shape
    depth = w_in.shape[0]
    alpha = (2 * depth) ** 0.25
    cr, sr, cm, sm = _rope_tables(seq)

    rw = jnp.pad(router_w, ((0, 0), (0, LANES - N_EXPERTS)))
    rwh = rw.astype(BF16)
    rw = jnp.concatenate([rwh, (rw - rwh.astype(F32)).astype(BF16)], 1)
    rb = router_bias.astype(F32).reshape(N_EXPERTS, 1)

    w1, wuq, wuk, wuvt, wr, ws, wm = _stacked_weights(w_in, mla_w_uq, mla_w_ukv, w_out)
    rows = lambda a: a.astype(F32).reshape(depth, 1, -1)
    qnw, kvnw, gnw = rows(mla_q_norm_w), rows(mla_kv_norm_w), rows(ret_gn_w)
    g1, b1, g2, b2 = rows(ln1_g), rows(ln1_b), rows(ln2_g), rows(ln2_b)
    sink_row = rows(jnp.repeat(swa_sinks, SWA_WINDOW, axis=1))

    t = x.reshape(batch * seq, D_MODEL)
    for l in range(depth):
        rq, rk, rv, rg, sq, sk, mq, mk, svt, mvt = _inproj(
            t, w1, cr, sr, cm, sm, qnw, kvnw, wuq, wuk, wuvt, layer=l, seq=seq, tile=MLA_TILE)
        ret_o = _retention(rq, rk, rv, rg, gnw, layer=l, batch=batch, seq=seq)
        swa_o = _swa(sq, sk, svt, sink_row, layer=l, seq=seq, blk=2 * MLA_TILE)
        mla_o = _mla(mq, mk, mvt, batch=batch, seq=seq, tk=MLA_TILE)
        x1, x1b, logits = _outproj(ret_o, swa_o, mla_o, t, wr, ws, wm, g1, b1, rw, layer=l, alpha=alpha)
        t = _moe(x1b, x1, logits, rb, exp_w_gate_up, exp_w_down, g2, b2, layer=l, alpha=alpha)
    return t.reshape(batch, seq, D_MODEL)
```

```python
import functools
import math

import numpy as np
import jax
import jax.numpy as jnp
from jax import lax
from jax.experimental import pallas as pl
from jax.experimental.pallas import tpu as pltpu

D_MODEL = 1024
HEAD_DIM = 64
ROPE_THETA = 10000.0

RET_HEADS = 4
RET_DK = 64
RET_DV = 64
RET_W = RET_HEADS * RET_DV

SWA_Q_HEADS = 6
SWA_KV_HEADS = 2
SWA_WINDOW = 128
SWA_W = SWA_Q_HEADS * HEAD_DIM
SWA_KV_W = SWA_KV_HEADS * HEAD_DIM

MLA_HEADS = 6
MLA_Q_RANK = 384
MLA_KV_RANK = 256
MLA_NOPE = 64
MLA_ROPE = 32
MLA_V = 64

N_EXPERTS = 16
N_GROUPS = 4
EXPERTS_PER_GROUP = N_EXPERTS // N_GROUPS
D_EXPERT = 256

LN_EPS = 1e-5
RMS_EPS = 1e-6
NEG_INF = -1e30

LANES = 128
MLA_PAD_W = MLA_HEADS * LANES
MLA_TILE = 512
VMEM_LIMIT = 56 * 1024 * 1024

F32 = jnp.float32
BF16 = jnp.bfloat16

_C_RQ, _C_RK, _C_RV, _C_RG = 0, 256, 512, 768
_C_SQ, _C_KR, _C_SK, _C_SV = 1024, 1408, 1536, 1664
_C_CKV, _C_CQ, _C_END = 1792, 2048, 2432

_SWA_HEAD_ORDER = (0, 3, 1, 4, 2, 5)


def _params(*sem, flags=None):
    return pltpu.CompilerParams(dimension_semantics=sem, vmem_limit_bytes=VMEM_LIMIT, flags=flags)


def _layer_spec(a, layer, **kwargs):
    return pl.BlockSpec((None,) + a.shape[1:], lambda *_: (layer,) + (0,) * (a.ndim - 1), **kwargs)


def _dot(a, b):
    return jnp.dot(a, b, preferred_element_type=F32)


def _dot_nt(a, b):
    return lax.dot_general(a, b, (((1,), (1,)), ((), ())), preferred_element_type=F32)


def _dot_tn(a, b):
    return lax.dot_general(a, b, (((0,), (0,)), ((), ())), preferred_element_type=F32)


def _split_dot(x, w_bf16):
    hi = x.astype(BF16)
    lo = (x - hi.astype(F32)).astype(BF16)
    return _dot(hi, w_bf16) + _dot(lo, w_bf16)


def _rope(h, cos, sin_signed, half):
    fwd = pltpu.roll(h, LANES - half, 1)
    bwd = pltpu.roll(h, half, 1)
    lane = lax.broadcasted_iota(jnp.int32, h.shape, 1)
    rot = jnp.where((lane % (2 * half)) < half, fwd, bwd)
    return h * cos + rot * sin_signed


def _layer_norm(y, g, b):
    mu = jnp.mean(y, -1, keepdims=True)
    d = y - mu
    var = jnp.mean(d * d, -1, keepdims=True)
    return d * lax.rsqrt(var + LN_EPS) * g + b


def _rms_norm(x, g):
    return x * lax.rsqrt(jnp.mean(x * x, -1, keepdims=True) + RMS_EPS) * g


def _inproj_kernel(x_ref, w1_ref, cr_ref, sr_ref, cm_ref, sm_ref, qnw_ref, kvnw_ref,
                   wuq_ref, wuk_ref, wuvt_ref,
                   rq_ref, rk_ref, rv_ref, rg_ref, sq_ref, sk_ref, mq_ref, mk_ref, svt_ref, mvt_ref,
                   *, mla_q_scale):
    tile = mvt_ref.shape[2]
    for s in range(x_ref.shape[0] // tile):
        _inproj_tile(slice(s * tile, (s + 1) * tile), s, x_ref, w1_ref, cr_ref, sr_ref, cm_ref, sm_ref,
                     qnw_ref, kvnw_ref, wuq_ref, wuk_ref, wuvt_ref, rq_ref, rk_ref, rv_ref, rg_ref,
                     sq_ref, sk_ref, mq_ref, mk_ref, svt_ref, mvt_ref, mla_q_scale)


def _inproj_tile(rows, s, x_ref, w1_ref, cr_ref, sr_ref, cm_ref, sm_ref, qnw_ref, kvnw_ref,
                 wuq_ref, wuk_ref, wuvt_ref, rq_ref, rk_ref, rv_ref, rg_ref, sq_ref, sk_ref, mq_ref,
                 mk_ref, svt_ref, mvt_ref, mla_q_scale):
    xb = x_ref[rows, :].astype(BF16)

    def proj(a, b):
        return _dot(xb, w1_ref[:, a:b])

    cq = proj(_C_CQ, _C_END)
    ckv = proj(_C_CKV, _C_CQ)
    rq, rk = proj(_C_RQ, _C_RK), proj(_C_RK, _C_RV)
    rv, rg = proj(_C_RV, _C_RG), proj(_C_RG, _C_SQ)
    sq_kr = proj(_C_SQ, _C_SK)
    sk_sv = proj(_C_SK, _C_CKV)
    cq = _rms_norm(cq, qnw_ref[...]).astype(BF16)
    ckv = _rms_norm(ckv, kvnw_ref[...]).astype(BF16)
    q = _dot(cq, wuq_ref[...])
    kn = _dot(ckv, wuk_ref[...])
    vt = _dot_nt(wuvt_ref[...], ckv)

    cr, sr = cr_ref[rows, :], sr_ref[rows, :]
    for g in range(RET_W // LANES):
        lanes = slice(g * LANES, (g + 1) * LANES)
        rq_ref[rows, lanes] = _rope(rq[:, lanes], cr, sr, RET_DK // 2).astype(BF16)
        rk_ref[rows, lanes] = (_rope(rk[:, lanes], cr, sr, RET_DK // 2) * (RET_DK ** -0.5)).astype(BF16)
    rv_ref[rows, :] = rv.astype(BF16)
    rg_ref[rows, :] = rg.astype(BF16)
    sq_ref[rows, :] = (sq_kr[:, :SWA_W] * (HEAD_DIM ** -0.5 * math.log2(math.e))).astype(BF16)
    sk_ref[rows, :] = sk_sv[:, :SWA_KV_W].astype(BF16)
    svt_ref[s] = sk_sv[:, SWA_KV_W:].T.astype(BF16)

    cm, sm = cm_ref[rows, :], sm_ref[rows, :]
    kpe = _rope(sq_kr[:, SWA_W:], cm, sm, MLA_ROPE // 2)
    for h in range(MLA_HEADS):
        lo, hi = h * LANES, (h + 1) * LANES
        mq_ref[rows, lo:hi] = (_rope(q[:, lo:hi], cm, sm, MLA_ROPE // 2) * mla_q_scale).astype(BF16)
        mk_ref[rows, lo:hi] = (kn[:, lo:hi] + kpe).astype(BF16)
    feat = lax.broadcasted_iota(jnp.int32, vt.shape, 0) % LANES
    mvt_ref[s] = (vt + (feat == MLA_V).astype(F32)).astype(BF16)


def _inproj(x, w1, cr, sr, cm, sm, qnw, kvnw, wuq, wuk, wuvt, *, layer, seq, tile, tiles_per_step=2):
    T = x.shape[0]
    tm = tile * tiles_per_step
    nt = T // tm
    npos = seq // tm
    row = lambda w: pl.BlockSpec((tm, w), lambda i: (i, 0))
    pos = lambda: pl.BlockSpec((tm, LANES), lambda i: (i % npos, 0))
    full = lambda a: _layer_spec(a, layer)
    widths = (RET_W, RET_W, RET_W, RET_W, SWA_W, SWA_KV_W, MLA_PAD_W, MLA_PAD_W)
    widths_t = (SWA_KV_W, MLA_PAD_W)
    mla_q_scale = (MLA_NOPE + MLA_ROPE) ** -0.5 * math.log2(math.e)
    return pl.pallas_call(
        functools.partial(_inproj_kernel, mla_q_scale=mla_q_scale),
        grid=(nt,),
        in_specs=[row(D_MODEL), full(w1), pos(), pos(), pos(), pos(), full(qnw), full(kvnw),
                  full(wuq), full(wuk), full(wuvt)],
        out_specs=[row(w) for w in widths]
        + [pl.BlockSpec((tiles_per_step, w, tile), lambda i: (i, 0, 0)) for w in widths_t],
        out_shape=[jax.ShapeDtypeStruct((T, w), BF16) for w in widths]
        + [jax.ShapeDtypeStruct((T // tile, w, tile), BF16) for w in widths_t],
        compiler_params=_params("parallel"),
        name="inproj",
    )(x, w1, cr, sr, cm, sm, qnw, kvnw, wuq, wuk, wuvt)


def _retention_kernel(q_ref, k_ref, v_ref, g_ref, gnw_ref, dec_ref, xi_ref, zeta_ref, dm_ref, bm_ref,
                      avg_ref, o_ref, state_ref, *, chunk, n_chunks):
    @pl.when(pl.program_id(1) == 0)
    def _():
        state_ref[...] = jnp.zeros_like(state_ref)

    lane_head = lax.broadcasted_iota(jnp.int32, (chunk, RET_W), 1) // RET_DV
    avg = avg_ref[...]
    chunks = [slice(c * chunk, (c + 1) * chunk) for c in range(n_chunks)]

    def raw_scores(rows):
        q = q_ref[rows, :]
        zero = jnp.zeros_like(q)
        q_heads = jnp.concatenate([jnp.where(lane_head == h, q, zero) for h in range(RET_HEADS)], 0)
        return _dot_nt(q_heads, k_ref[rows, :])

    scores = [raw_scores(rows) for rows in chunks]
    updates = [_dot_tn((k_ref[rows, :].astype(F32) * zeta_ref[...]).astype(BF16), v_ref[rows, :])
               for rows in chunks]
    inner = [_dot((s * dec_ref[...]).astype(BF16), v_ref[rows, :]) for s, rows in zip(scores, chunks)]

    state = state_ref[...]
    outs = []
    for rows, upd, inner_heads in zip(chunks, updates, inner):
        o = _dot((q_ref[rows, :].astype(F32) * xi_ref[...]).astype(BF16), state.astype(BF16))
        for h in range(RET_HEADS):
            o = o + jnp.where(lane_head == h, inner_heads[h * chunk:(h + 1) * chunk], 0.0)
        outs.append(o)
        state = dm_ref[...] * state + bm_ref[...] * upd
    state_ref[...] = state

    mus = [_split_dot(o, avg) for o in outs]
    devs = [o - mu for o, mu in zip(outs, mus)]
    variances = [_split_dot(d * d, avg) for d in devs]
    for rows, d, var in zip(chunks, devs, variances):
        on = d * lax.rsqrt(var + LN_EPS) * gnw_ref[...]
        gate = g_ref[rows, :].astype(F32)
        o_ref[rows, :] = (on * (gate * jax.nn.sigmoid(gate))).astype(BF16)


def _retention_tables(chunk):
    H = RET_HEADS
    gamma = 1.0 - 2.0 ** (-5.0 - np.arange(H, dtype=np.float64))
    log_g = np.log(gamma)
    idx = np.arange(chunk, dtype=np.float64)
    rel = idx[:, None] - idx[None, :]
    dec = np.where(rel[None] >= 0, np.exp(np.maximum(rel, 0.0)[None] * log_g[:, None, None]), 0.0)
    dec = dec.reshape(H * chunk, chunk)
    lane_head = np.arange(RET_W) // RET_DV
    xi = np.exp((idx[:, None] + 1.0) * log_g[lane_head][None, :])
    zeta = np.exp((chunk - 1.0 - idx[:, None]) * log_g[lane_head][None, :])
    same = lane_head[:, None] == lane_head[None, :]
    dm = np.where(same, np.exp(chunk * log_g)[lane_head][:, None], 0.0)
    bm = same.astype(np.float64)
    avg = bm / RET_DV
    f = lambda a: jnp.asarray(a, F32)
    return f(dec), f(xi), f(zeta), f(dm), f(bm), jnp.asarray(avg, BF16)


def _retention(rq, rk, rv, rg, gnw, *, layer, batch, seq, blk=1024, chunk=128):
    T = rq.shape[0]
    nb = seq // blk
    tables = _retention_tables(chunk)
    row = pl.BlockSpec((blk, RET_W), lambda b, i: (b * nb + i, 0))
    full = lambda a: pl.BlockSpec(a.shape, lambda b, i: (0, 0))
    return pl.pallas_call(
        functools.partial(_retention_kernel, chunk=chunk, n_chunks=blk // chunk),
        grid=(batch, nb),
        in_specs=[row, row, row, row, _layer_spec(gnw, layer)] + [full(t) for t in tables],
        out_specs=row,
        out_shape=jax.ShapeDtypeStruct((T, RET_W), BF16),
        scratch_shapes=[pltpu.VMEM((RET_W, RET_W), F32)],
        compiler_params=_params("parallel", "arbitrary"),
        name="retention",
    )(rq, rk, rv, rg, gnw, *tables)


def _swa_kernel(q_ref, kc_ref, kp_ref, vtc_ref, vtp_ref, sink_ref, o_ref, *, blk, seq):
    L = SWA_WINDOW
    n_pairs = SWA_Q_HEADS // 2
    n_sub = blk // L
    lane_lo = lax.broadcasted_iota(jnp.int32, (L, LANES), 1) < HEAD_DIM
    feat_lo = lax.broadcasted_iota(jnp.int32, (LANES, L), 0) < HEAD_DIM
    key = lax.broadcasted_iota(jnp.int32, (2 * L, SWA_Q_HEADS * L), 0)
    qry = lax.broadcasted_iota(jnp.int32, (2 * L, SWA_Q_HEADS * L), 1) % L
    band = (key > qry) & (key <= qry + L)
    sink = sink_ref[...] * math.log2(math.e)

    def scores(sb):
        r0 = sb * L
        k_prev = kp_ref[...] if sb == 0 else kc_ref[r0 - L:r0, :]
        k_win = jnp.concatenate([k_prev, kc_ref[r0:r0 + L, :]], 0)
        pairs = [q_ref[r0:r0 + L, g * LANES:(g + 1) * LANES] for g in range(n_pairs)]
        zero = jnp.zeros_like(pairs[0])
        q_heads = jnp.concatenate([jnp.where(lane_lo, p, zero) for p in pairs]
                                  + [jnp.where(lane_lo, zero, p) for p in pairs], 0)
        return _dot_nt(k_win, q_heads)

    def attend(sb, st):
        r0 = sb * L
        valid = band
        if sb == 0:
            pos0 = (pl.program_id(0) * blk) % seq
            valid = band & (key >= jnp.where(pos0 > 0, 0, L))
        st = jnp.where(valid, st, NEG_INF)
        m = jnp.maximum(jnp.max(st, 0, keepdims=True), sink)
        p = jnp.exp2(st - m)
        denom = jnp.sum(p, 0, keepdims=True) + jnp.exp2(sink - m)
        c, off = divmod(r0, vtc_ref.shape[2])
        if off == 0:
            vt_prev = vtp_ref[0] if c == 0 else vtc_ref[c - 1, :, vtc_ref.shape[2] - L:]
            vt_win = jnp.concatenate([vt_prev, vtc_ref[c, :, :L]], 1)
        else:
            vt_win = vtc_ref[c, :, off - L:off + L]
        ot = _dot(vt_win, p.astype(BF16)) * (1.0 / denom)
        for g in range(n_pairs):
            o_lo = ot[:, g * L:(g + 1) * L]
            o_hi = ot[:, (g + n_pairs) * L:(g + n_pairs + 1) * L]
            o_ref[r0:r0 + L, g * LANES:(g + 1) * LANES] = jnp.where(feat_lo, o_lo, o_hi).T.astype(BF16)

    st = scores(0)
    for sb in range(n_sub):
        st_next = scores(sb + 1) if sb + 1 < n_sub else None
        attend(sb, st)
        st = st_next


def _swa(sq, sk, svt, sink_row, *, layer, seq, blk):
    T = sq.shape[0]
    L = SWA_WINDOW
    per = blk // L
    tile = svt.shape[2]
    tiles = blk // tile
    cur = lambda w: pl.BlockSpec((blk, w), lambda i: (i, 0))
    k_prev = pl.BlockSpec((L, SWA_KV_W), lambda i: (jnp.maximum(i * per - 1, 0), 0))
    vt_cur = pl.BlockSpec((tiles, SWA_KV_W, tile), lambda i: (i, 0, 0))
    vt_prev = pl.BlockSpec((1, SWA_KV_W, L), lambda i: (jnp.maximum(i * tiles - 1, 0), 0, tile // L - 1))
    return pl.pallas_call(
        functools.partial(_swa_kernel, blk=blk, seq=seq),
        grid=(T // blk,),
        in_specs=[cur(SWA_W), cur(SWA_KV_W), k_prev, vt_cur, vt_prev, _layer_spec(sink_row, layer)],
        out_specs=cur(SWA_W),
        out_shape=jax.ShapeDtypeStruct((T, SWA_W), BF16),
        compiler_params=_params("parallel"),
        name="swa",
    )(sq, sk, sk, svt, svt, sink_row)


def _mla_kernel(q_ref, k_ref, vt_ref, o_ref, st0_ref, st1_ref, cmax0_ref, cmax1_ref, m_ref, acc_ref,
                *, tq, tk, heads):
    i = pl.program_id(2)
    head_lanes = [slice(h * LANES, (h + 1) * LANES) for h in range(heads)]
    st_refs = (st0_ref, st1_ref)
    cmax_refs = (cmax0_ref, cmax1_ref)
    all_q = slice(0, tq)
    late_q = slice(tk, tq)

    def scores(j, lanes, qs):
        start = pl.multiple_of(j * tk, tk)
        return _dot_nt(k_ref[pl.ds(start, tk), lanes], q_ref[qs, lanes])

    def step(j, cur, qs=all_q, diagonal=False, next_qs=all_q):
        for h, lanes in enumerate(head_lanes):
            if next_qs is not None:
                st_next = scores(j + 1, lanes, next_qs)
                st_refs[1 - cur][h, :, next_qs] = st_next
                if next_qs is all_q:
                    cmax_refs[1 - cur][h] = jnp.max(st_next, 0, keepdims=True)
            st = st_refs[cur][h, :, qs]
            if diagonal:
                key = lax.broadcasted_iota(jnp.int32, st.shape, 0)
                qry = lax.broadcasted_iota(jnp.int32, st.shape, 1)
                st = jnp.where(key <= qry, st, NEG_INF)
                col_max = jnp.max(st, 0, keepdims=True)
            else:
                col_max = cmax_refs[cur][h]
            m = m_ref[h][:, qs]
            m_new = jnp.maximum(m, col_max)
            pt = jnp.exp2(st - m_new).astype(BF16)
            acc_ref[h, :, qs] = jnp.exp2(m - m_new) * acc_ref[h, :, qs] + _dot(vt_ref[j, lanes, :], pt)
            if qs is all_q:
                m_ref[h] = m_new

    m_ref[...] = jnp.full(m_ref.shape, NEG_INF, F32)
    acc_ref[...] = jnp.zeros(acc_ref.shape, F32)
    for h, lanes in enumerate(head_lanes):
        st0_ref[h] = scores(0, lanes, all_q)
        cmax0_ref[h] = jnp.max(st0_ref[h], 0, keepdims=True)

    def pair(p, _):
        step(2 * p, 0)
        step(2 * p + 1, 1)
        return 0

    lax.fori_loop(0, i, pair, 0)
    step(2 * i, 0, diagonal=True, next_qs=late_q)
    step(2 * i + 1, 1, qs=late_q, diagonal=True, next_qs=None)
    for h in range(heads):
        acc = acc_ref[h]
        o_ref[:, h * LANES:(h + 1) * LANES] = (acc / acc[MLA_V:MLA_V + 1, :]).T.astype(BF16)


def _mla(mq, mk, mvt, *, batch, seq, tk, heads=3):
    T = mq.shape[0]
    tq = 2 * tk
    nq = seq // tq
    qo = pl.BlockSpec((tq, heads * LANES), lambda b, h, i: (b * nq + i, h))
    k = pl.BlockSpec((seq, heads * LANES), lambda b, h, i: (b, h))
    vt = pl.BlockSpec((seq // tk, heads * LANES, tk), lambda b, h, i: (b, h, 0))
    return pl.pallas_call(
        functools.partial(_mla_kernel, tq=tq, tk=tk, heads=heads),
        grid=(batch, MLA_HEADS // heads, nq),
        in_specs=[qo, k, vt],
        out_specs=qo,
        out_shape=jax.ShapeDtypeStruct((T, MLA_PAD_W), BF16),
        scratch_shapes=[pltpu.VMEM((heads, tk, tq), F32), pltpu.VMEM((heads, tk, tq), F32),
                        pltpu.VMEM((heads, 1, tq), F32), pltpu.VMEM((heads, 1, tq), F32),
                        pltpu.VMEM((heads, 1, tq), F32), pltpu.VMEM((heads, LANES, tq), F32)],
        compiler_params=_params("parallel", "parallel", "arbitrary"),
        name="mla",
    )(mq, mk, mvt)


def _outproj_kernel(ret_ref, swa_ref, mla_ref, x_ref, wr_ref, ws_ref, wm_ref, g_ref, b_ref,
                    rw_ref, x1_ref, x1b_ref, logits_ref, *, alpha, n_sub):
    tm = x_ref.shape[0]
    subs = [slice(s * tm // n_sub, (s + 1) * tm // n_sub) for s in range(n_sub)]

    def mix(rows):
        return (_dot(ret_ref[rows, :], wr_ref[...]) + _dot(swa_ref[rows, :], ws_ref[...])
                + _dot(mla_ref[rows, :], wm_ref[...]))

    def finish(rows, mixed):
        x1 = _layer_norm(alpha * x_ref[rows, :] + mixed, g_ref[...], b_ref[...])
        x1_ref[rows, :] = x1
        hi = x1.astype(BF16)
        x1b_ref[rows, :] = hi
        lo = (x1 - hi.astype(F32)).astype(BF16)
        hi_part = _dot(hi, rw_ref[...])
        logits_ref[rows, :] = hi_part[:, :LANES] + hi_part[:, LANES:] + _dot(lo, rw_ref[:, :LANES])

    mixed = mix(subs[0])
    for s in range(n_sub):
        mixed_next = mix(subs[s + 1]) if s + 1 < n_sub else None
        finish(subs[s], mixed)
        mixed = mixed_next


def _outproj(ret_o, swa_o, mla_o, x, wr, ws, wm, g, b, rw, *, layer, alpha, tm=1024, n_sub=8):
    T = x.shape[0]
    row = lambda w: pl.BlockSpec((tm, w), lambda i: (i, 0))
    full = lambda a: pl.BlockSpec(a.shape, lambda i: (0, 0))
    per_layer = lambda a: _layer_spec(a, layer)
    return pl.pallas_call(
        functools.partial(_outproj_kernel, alpha=alpha, n_sub=n_sub),
        grid=(T // tm,),
        in_specs=[row(RET_W), row(SWA_W), row(MLA_PAD_W), row(D_MODEL), per_layer(wr), per_layer(ws),
                  per_layer(wm), per_layer(g), per_layer(b), full(rw)],
        out_specs=[row(D_MODEL), row(D_MODEL), row(LANES)],
        out_shape=[jax.ShapeDtypeStruct((T, D_MODEL), F32), jax.ShapeDtypeStruct((T, D_MODEL), BF16),
                   jax.ShapeDtypeStruct((T, LANES), F32)],
        compiler_params=_params("parallel"),
        name="outproj",
    )(ret_o, swa_o, mla_o, x, wr, ws, wm, g, b, rw)


MOE_CHUNK = 128
MOE_SEG_ALIGN = 16


def _moe_rows(tm):
    return -(-(2 * tm + N_EXPERTS * (MOE_SEG_ALIGN - 1) + MOE_CHUNK) // LANES) * LANES


def _ring_partner(x, d, width):
    n = x.shape[0]
    ahead = pltpu.roll(x, n - d, 0)
    if width == n:
        return ahead
    row = lax.broadcasted_iota(jnp.int32, x.shape, 0)
    return jnp.where((row % width) + d < width, ahead, pltpu.roll(x, width - d, 0))


def _router_gates_t(logits_t, bias_col):
    row = lax.broadcasted_iota(jnp.int32, logits_t.shape, 0)
    scores = jax.nn.sigmoid(logits_t)
    biased = scores + bias_col
    G = EXPERTS_PER_GROUP
    member = row % G
    rank = jnp.zeros(logits_t.shape, jnp.int32)
    for d in range(1, G):
        other = _ring_partner(biased, d, G)
        other_first = (member + d) % G < member
        rank = rank + ((other > biased) | ((other == biased) & other_first)).astype(jnp.int32)
    top2 = rank < 2
    part = jnp.where(top2, biased, 0.0)
    grp_score = part
    for d in range(1, G):
        grp_score = grp_score + _ring_partner(part, d, G)
    beaten = jnp.zeros(logits_t.shape, jnp.bool_)
    grp = row // G
    for dg in range(1, N_GROUPS):
        other = _ring_partner(grp_score, dg * G, N_EXPERTS)
        other_first = (grp + dg) % N_GROUPS < grp
        beaten = beaten | (other > grp_score) | ((other == grp_score) & other_first)
    picked = jnp.where(top2 & jnp.logical_not(beaten), scores, 0.0)
    return picked / jnp.sum(picked, 0, keepdims=True)


def _moe_kernel(xb_ref, x_ref, logits_ref, rb_ref, earlier_ref, wgu_hbm, wd_hbm, g_ref, b_ref, o_ref,
                xs_ref, ys_ref, wgu_ref, wd_ref, gu_stage, d_stage, sem, start_ref, chunks_ref,
                *, alpha, layer):
    tm = xb_ref.shape[0]
    R = xs_ref.shape[0]
    big = float(4 * R)

    @pl.when(pl.program_id(0) == 0)
    def _():
        def expert_copies(e, slot):
            return (pltpu.make_async_copy(wgu_hbm.at[layer, e], gu_stage.at[slot], sem.at[0, slot]),
                    pltpu.make_async_copy(wd_hbm.at[layer, e], d_stage.at[slot], sem.at[1, slot]))

        for copy in expert_copies(0, 0):
            copy.start()
        for e in range(N_EXPERTS):
            slot = e % 2
            if e + 1 < N_EXPERTS:
                for copy in expert_copies(e + 1, 1 - slot):
                    copy.start()
            for copy in expert_copies(e, slot):
                copy.wait()
            wgu_ref[e] = gu_stage[slot].astype(BF16)
            wd_ref[e] = d_stage[slot].astype(BF16)

    experts_t = _router_gates_t(logits_ref[...].T[:N_EXPERTS], rb_ref[...])
    gates_t = jnp.concatenate([experts_t, jnp.zeros((LANES - N_EXPERTS, tm), F32)], 0)
    gates = gates_t.T

    def strict_upper(n):
        return lax.broadcasted_iota(jnp.int32, (n, n), 0) < lax.broadcasted_iota(jnp.int32, (n, n), 1)

    def seg_pad(count):
        return jnp.floor((count + (MOE_SEG_ALIGN - 1.0)) * (1.0 / MOE_SEG_ALIGN)) * MOE_SEG_ALIGN

    one = lambda mask: jnp.where(mask, 1.0, 0.0).astype(BF16)

    sel = (gates > 0.0) & (lax.broadcasted_iota(jnp.int32, gates.shape, 1) < N_EXPERTS)
    earlier = earlier_ref[...]
    rank = _dot(earlier, one(sel))
    count = jnp.sum(jnp.where(sel, 1.0, 0.0), 0, keepdims=True)
    pad = jnp.broadcast_to(seg_pad(count), (8, LANES)).astype(BF16)
    offset = _dot(pad, one(strict_upper(LANES)))[0:1]
    slot = offset + rank
    slot_a = jnp.min(jnp.where(sel, slot, big), -1, keepdims=True)
    slot_b = jnp.max(jnp.where(sel, slot, -1.0), -1, keepdims=True)
    gate_a = jnp.sum(jnp.where(sel & (slot == slot_a), gates, 0.0), -1, keepdims=True)
    gate_b = jnp.sum(jnp.where(sel & (slot == slot_b), gates, 0.0), -1, keepdims=True)

    sel_t = experts_t > 0.0
    rank_t = _dot_nt(one(sel_t), earlier)
    on_diag = (lax.broadcasted_iota(jnp.int32, (N_EXPERTS, LANES), 0)
               == lax.broadcasted_iota(jnp.int32, (N_EXPERTS, LANES), 1))
    offset_col = jnp.sum(jnp.where(on_diag, offset, 0.0), -1, keepdims=True)
    slot_t = offset_col + rank_t
    slot_a_t = jnp.min(jnp.where(sel_t, slot_t, big), 0, keepdims=True)
    slot_b_t = jnp.max(jnp.where(sel_t, slot_t, -1.0), 0, keepdims=True)

    row_id = lax.broadcasted_iota(jnp.int32, (R, tm), 0)
    sort = one((row_id == slot_a_t.astype(jnp.int32)) | (row_id == slot_b_t.astype(jnp.int32)))
    xs_ref[...] = _dot(sort, xb_ref[...]).astype(BF16)

    @pl.when(pl.program_id(0) == 0)
    def _():
        ys_ref[...] = jnp.zeros_like(ys_ref)

    offset_i = offset.astype(jnp.int32)
    chunks_i = jnp.floor((count + (MOE_CHUNK - 1.0)) * (1.0 / MOE_CHUNK)).astype(jnp.int32)
    for e in range(N_EXPERTS):
        start_ref[e] = offset_i[0, e]
        chunks_ref[e] = chunks_i[0, e]

    def chunk_rows(e, c):
        return pl.ds(pl.multiple_of(start_ref[e] + c * MOE_CHUNK, MOE_SEG_ALIGN), MOE_CHUNK)

    def activation(e, rows):
        h = _dot(xs_ref[rows, :], wgu_ref[e])
        up, lin = h[:, :D_EXPERT], h[:, D_EXPERT:]
        return (up * jax.nn.sigmoid(up) * lin).astype(BF16)

    def project(e, rows, a):
        ys_ref[rows, :] = _dot(a, wd_ref[e]).astype(BF16)

    def later_chunks(e, _):
        def body(c, _):
            rows = chunk_rows(e, c)
            project(e, rows, activation(e, rows))
            return 0
        return lax.fori_loop(1, chunks_ref[e], body, 0)

    lax.fori_loop(0, N_EXPERTS, later_chunks, 0)

    a = activation(0, chunk_rows(0, 0))
    for e in range(N_EXPERTS):
        a_next = activation(e + 1, chunk_rows(e + 1, 0)) if e + 1 < N_EXPERTS else None
        project(e, chunk_rows(e, 0), a)
        a = a_next

    col_id = lax.broadcasted_iota(jnp.int32, (tm, R), 1)
    unsort = jnp.where(col_id == slot_a.astype(jnp.int32), gate_a,
                       jnp.where(col_id == slot_b.astype(jnp.int32), gate_b, 0.0)).astype(BF16)
    o_ref[...] = _layer_norm(alpha * x_ref[...] + _dot(unsort, ys_ref[...]), g_ref[...], b_ref[...])


def _moe(x1b, x1, logits, rb, wgu, wd, g, b, *, layer, alpha, tm=512):
    T = x1.shape[0]
    R = _moe_rows(tm)
    earlier = jnp.asarray(np.tril(np.ones((tm, tm), np.float32), -1), BF16)
    row = lambda w: pl.BlockSpec((tm, w), lambda i: (i, 0))
    full = lambda a: _layer_spec(a, layer)
    assert wgu.dtype == F32 and wd.dtype == F32, "the staging buffers are f32"
    in_hbm = pl.BlockSpec(memory_space=pl.ANY)
    return pl.pallas_call(
        functools.partial(_moe_kernel, alpha=alpha, layer=layer),
        grid=(T // tm,),
        in_specs=[row(D_MODEL), row(D_MODEL), row(LANES), pl.BlockSpec(rb.shape, lambda i: (0, 0)),
                  pl.BlockSpec(earlier.shape, lambda i: (0, 0)), in_hbm, in_hbm, full(g), full(b)],
        out_specs=row(D_MODEL),
        out_shape=jax.ShapeDtypeStruct((T, D_MODEL), F32),
        scratch_shapes=[pltpu.VMEM((R, D_MODEL), BF16), pltpu.VMEM((R, D_MODEL), BF16),
                        pltpu.VMEM(wgu.shape[1:], BF16), pltpu.VMEM(wd.shape[1:], BF16),
                        pltpu.VMEM((2,) + wgu.shape[2:], F32), pltpu.VMEM((2,) + wd.shape[2:], F32),
                        pltpu.SemaphoreType.DMA((2, 2)),
                        pltpu.SMEM((N_EXPERTS,), jnp.int32), pltpu.SMEM((N_EXPERTS,), jnp.int32)],
        compiler_params=_params("arbitrary"),
        name="moe",
    )(x1b, x1, logits, rb, earlier, wgu, wd, g, b)


def _rope_tables(seq):
    pos = np.arange(seq, dtype=np.float64)[:, None]
    lane = np.arange(LANES)

    def table(dim, active, offset):
        j = (lane - offset) % dim
        inv = ROPE_THETA ** (-(2.0 * (j % (dim // 2))) / dim)
        ang = pos * inv[None, :]
        sign = np.where(j < dim // 2, -1.0, 1.0)
        cos = np.where(active[None, :], np.cos(ang), 1.0)
        sin = np.where(active[None, :], np.sin(ang) * sign[None, :], 0.0)
        return jnp.asarray(cos, F32), jnp.asarray(sin, F32)

    cr, sr = table(RET_DK, np.ones(LANES, bool), 0)
    cm, sm = table(MLA_ROPE, (lane >= MLA_NOPE) & (lane < MLA_NOPE + MLA_ROPE), MLA_NOPE)
    return cr, sr, cm, sm


def _stacked_weights(w_in, mla_w_uq, mla_w_ukv, w_out):
    depth = w_in.shape[0]
    sizes = (256, 256, 256, 256, SWA_W, SWA_KV_W, SWA_KV_W, MLA_Q_RANK, MLA_KV_RANK, MLA_ROPE)
    o = [int(v) for v in np.concatenate([[0], np.cumsum(sizes)])]
    order = np.array(_SWA_HEAD_ORDER)
    pad_lanes = lambda a: jnp.pad(a, [(0, 0)] * (a.ndim - 1) + [(0, LANES - a.shape[-1])])

    sq = w_in[:, :, o[4]:o[5]].reshape(depth, D_MODEL, SWA_Q_HEADS, HEAD_DIM)[:, :, order]
    kr = jnp.pad(w_in[:, :, o[9]:o[10]], ((0, 0), (0, 0), (MLA_NOPE, LANES - MLA_NOPE - MLA_ROPE)))
    w1 = jnp.concatenate([w_in[:, :, :o[4]], sq.reshape(depth, D_MODEL, SWA_W), kr, w_in[:, :, o[5]:o[7]],
                          w_in[:, :, o[8]:o[9]], w_in[:, :, o[7]:o[8]]], 2).astype(BF16)

    dq = MLA_NOPE + MLA_ROPE
    wuq = pad_lanes(mla_w_uq.reshape(depth, MLA_Q_RANK, MLA_HEADS, dq)).reshape(depth, MLA_Q_RANK, MLA_PAD_W)
    ukv = mla_w_ukv.reshape(depth, MLA_KV_RANK, MLA_HEADS, MLA_NOPE + MLA_V)
    wuk = pad_lanes(ukv[..., :MLA_NOPE]).reshape(depth, MLA_KV_RANK, MLA_PAD_W)
    wuvt = jnp.swapaxes(pad_lanes(ukv[..., MLA_NOPE:]).reshape(depth, MLA_KV_RANK, MLA_PAD_W), 1, 2)

    wr = w_out[:, :RET_W]
    ws = w_out[:, RET_W:RET_W + SWA_W].reshape(depth, SWA_Q_HEADS, HEAD_DIM, D_MODEL)[:, order]
    wm = w_out[:, RET_W + SWA_W:].reshape(depth, MLA_HEADS, MLA_V, D_MODEL)
    wm = jnp.pad(wm, ((0, 0), (0, 0), (0, LANES - MLA_V), (0, 0)))
    to_bf16 = lambda a: a.astype(BF16)
    return (w1, to_bf16(wuq), to_bf16(wuk), to_bf16(wuvt), to_bf16(wr),
            to_bf16(ws.reshape(depth, SWA_W, D_MODEL)), to_bf16(wm.reshape(depth, MLA_PAD_W, D_MODEL)))


def kernel(x, w_in, ret_gn_w, swa_sinks, mla_q_norm_w, mla_kv_norm_w, mla_w_uq, mla_w_ukv, w_out,
           ln1_g, ln1_b, router_w, router_bias, exp_w_gate_up, exp_w_down, ln2_g, ln2_b):
    batch, seq, _ = x.shape
    depth = w_in.shape[0]
    alpha = (2 * depth) ** 0.25
    cr, sr, cm, sm = _rope_tables(seq)

    rw = jnp.pad(router_w, ((0, 0), (0, LANES - N_EXPERTS)))
    rwh = rw.astype(BF16)
    rw = jnp.concatenate([rwh, (rw - rwh.astype(F32)).astype(BF16)], 1)
    rb = router_bias.astype(F32).reshape(N_EXPERTS, 1)

    w1, wuq, wuk, wuvt, wr, ws, wm = _stacked_weights(w_in, mla_w_uq, mla_w_ukv, w_out)
    rows = lambda a: a.astype(F32).reshape(depth, 1, -1)
    qnw, kvnw, gnw = rows(mla_q_norm_w), rows(mla_kv_norm_w), rows(ret_gn_w)
    g1, b1, g2, b2 = rows(ln1_g), rows(ln1_b), rows(ln2_g), rows(ln2_b)
    sink_row = rows(jnp.repeat(swa_sinks, SWA_WINDOW, axis=1))

    t = x.reshape(batch * seq, D_MODEL)
    for l in range(depth):
        rq, rk, rv, rg, sq, sk, mq, mk, svt, mvt = _inproj(
            t, w1, cr, sr, cm, sm, qnw, kvnw, wuq, wuk, wuvt, layer=l, seq=seq, tile=MLA_TILE)
        ret_o = _retention(rq, rk, rv, rg, gnw, layer=l, batch=batch, seq=seq)
        swa_o = _swa(sq, sk, svt, sink_row, layer=l, seq=seq, blk=2 * MLA_TILE)
        mla_o = _mla(mq, mk, mvt, batch=batch, seq=seq, tk=MLA_TILE)
        x1, x1b, logits = _outproj(ret_o, swa_o, mla_o, t, wr, ws, wm, g1, b1, rw, layer=l, alpha=alpha)
        t = _moe(x1b, x1, logits, rb, exp_w_gate_up, exp_w_down, g2, b2, layer=l, alpha=alpha)
    return t.reshape(batch, seq, D_MODEL)
```

```python
import functools
import math

import numpy as np
import jax
import jax.numpy as jnp
from jax import lax
from jax.experimental import pallas as pl
from jax.experimental.pallas import tpu as pltpu

D_MODEL = 1024
HEAD_DIM = 64
ROPE_THETA = 10000.0

RET_HEADS = 4
RET_DK = 64
RET_DV = 64
RET_W = RET_HEADS * RET_DV

SWA_Q_HEADS = 6
SWA_KV_HEADS = 2
SWA_WINDOW = 128
SWA_W = SWA_Q_HEADS * HEAD_DIM
SWA_KV_W = SWA_KV_HEADS * HEAD_DIM

MLA_HEADS = 6
MLA_Q_RANK = 384
MLA_KV_RANK = 256
MLA_NOPE = 64
MLA_ROPE = 32
MLA_V = 64

N_EXPERTS = 16
N_GROUPS = 4
EXPERTS_PER_GROUP = N_EXPERTS // N_GROUPS
D_EXPERT = 256

LN_EPS = 1e-5
RMS_EPS = 1e-6
NEG_INF = -1e30

LANES = 128
SUBLANES = 8
MLA_PAD_W = MLA_HEADS * LANES
MLA_TILE = 512
VMEM_LIMIT = 56 * 1024 * 1024

F32 = jnp.float32
BF16 = jnp.bfloat16

_C_RQ, _C_RK, _C_RV, _C_RG = 0, 256, 512, 768
_C_SQ, _C_KR, _C_SK, _C_SV = 1024, 1408, 1536, 1664
_C_CKV, _C_CQ, _C_END = 1792, 2048, 2432

_SWA_HEAD_ORDER = (0, 3, 1, 4, 2, 5)


def _params(*sem, flags=None):
    return pltpu.CompilerParams(dimension_semantics=sem, vmem_limit_bytes=VMEM_LIMIT, flags=flags)


def _layer_spec(a, layer, **kwargs):
    return pl.BlockSpec((None,) + a.shape[1:], lambda *_: (layer,) + (0,) * (a.ndim - 1), **kwargs)


def _dot(a, b):
    return jnp.dot(a, b, preferred_element_type=F32)


def _dot_nt(a, b):
    return lax.dot_general(a, b, (((1,), (1,)), ((), ())), preferred_element_type=F32)


def _dot_tn(a, b):
    return lax.dot_general(a, b, (((0,), (0,)), ((), ())), preferred_element_type=F32)


def _split_dot(x, w_bf16):
    hi = x.astype(BF16)
    lo = (x - hi.astype(F32)).astype(BF16)
    return _dot(hi, w_bf16) + _dot(lo, w_bf16)


def _rope(h, cos, sin_signed, half):
    fwd = pltpu.roll(h, LANES - half, 1)
    bwd = pltpu.roll(h, half, 1)
    lane = lax.broadcasted_iota(jnp.int32, h.shape, 1)
    rot = jnp.where((lane % (2 * half)) < half, fwd, bwd)
    return h * cos + rot * sin_signed


def _layer_norm(y, g, b):
    mu = jnp.mean(y, -1, keepdims=True)
    d = y - mu
    var = jnp.mean(d * d, -1, keepdims=True)
    return d * lax.rsqrt(var + LN_EPS) * g + b


def _rms_norm(x, g):
    return x * lax.rsqrt(jnp.mean(x * x, -1, keepdims=True) + RMS_EPS) * g


def _inproj_kernel(x_ref, w1_ref, cr_ref, sr_ref, cm_ref, sm_ref, qnw_ref, kvnw_ref,
                   wuq_ref, wuk_ref, wuvt_ref,
                   rq_ref, rk_ref, rv_ref, rg_ref, sq_ref, sk_ref, mq_ref, mk_ref, svt_ref, mvt_ref,
                   *, mla_q_scale):
    tile = mvt_ref.shape[2]
    for s in range(x_ref.shape[0] // tile):
        _inproj_tile(slice(s * tile, (s + 1) * tile), s, x_ref, w1_ref, cr_ref, sr_ref, cm_ref, sm_ref,
                     qnw_ref, kvnw_ref, wuq_ref, wuk_ref, wuvt_ref, rq_ref, rk_ref, rv_ref, rg_ref,
                     sq_ref, sk_ref, mq_ref, mk_ref, svt_ref, mvt_ref, mla_q_scale)


def _inproj_tile(rows, s, x_ref, w1_ref, cr_ref, sr_ref, cm_ref, sm_ref, qnw_ref, kvnw_ref,
                 wuq_ref, wuk_ref, wuvt_ref, rq_ref, rk_ref, rv_ref, rg_ref, sq_ref, sk_ref, mq_ref,
                 mk_ref, svt_ref, mvt_ref, mla_q_scale):
    xb = x_ref[rows, :].astype(BF16)

    def proj(a, b):
        return _dot(xb, w1_ref[:, a:b])

    cq = proj(_C_CQ, _C_END)
    ckv = proj(_C_CKV, _C_CQ)
    rq, rk = proj(_C_RQ, _C_RK), proj(_C_RK, _C_RV)
    rv, rg = proj(_C_RV, _C_RG), proj(_C_RG, _C_SQ)
    sq_kr = proj(_C_SQ, _C_SK)
    sk_sv = proj(_C_SK, _C_CKV)
    cq = _rms_norm(cq, qnw_ref[...]).astype(BF16)
    ckv = _rms_norm(ckv, kvnw_ref[...]).astype(BF16)
    q = _dot(cq, wuq_ref[...])
    kn = _dot(ckv, wuk_ref[...])
    vt = _dot_nt(wuvt_ref[...], ckv)

    cr, sr = cr_ref[rows, :], sr_ref[rows, :]
    for g in range(RET_W // LANES):
        lanes = slice(g * LANES, (g + 1) * LANES)
        rq_ref[rows, lanes] = _rope(rq[:, lanes], cr, sr, RET_DK // 2).astype(BF16)
        rk_ref[rows, lanes] = (_rope(rk[:, lanes], cr, sr, RET_DK // 2) * (RET_DK ** -0.5)).astype(BF16)
    rv_ref[rows, :] = rv.astype(BF16)
    rg_ref[rows, :] = rg.astype(BF16)
    sq_ref[rows, :] = (sq_kr[:, :SWA_W] * (HEAD_DIM ** -0.5 * math.log2(math.e))).astype(BF16)
    sk_ref[rows, :] = sk_sv[:, :SWA_KV_W].astype(BF16)
    svt_ref[s] = sk_sv[:, SWA_KV_W:].T.astype(BF16)

    cm, sm = cm_ref[rows, :], sm_ref[rows, :]
    kpe = _rope(sq_kr[:, SWA_W:], cm, sm, MLA_ROPE // 2)
    for h in range(MLA_HEADS):
        lo, hi = h * LANES, (h + 1) * LANES
        mq_ref[rows, lo:hi] = (_rope(q[:, lo:hi], cm, sm, MLA_ROPE // 2) * mla_q_scale).astype(BF16)
        mk_ref[rows, lo:hi] = (kn[:, lo:hi] + kpe).astype(BF16)
    feat = lax.broadcasted_iota(jnp.int32, vt.shape, 0) % LANES
    mvt_ref[s] = (vt + (feat == MLA_V).astype(F32)).astype(BF16)


def _inproj(x, w1, cr, sr, cm, sm, qnw, kvnw, wuq, wuk, wuvt, *, layer, seq, tile, tiles_per_step=2):
    T = x.shape[0]
    tm = tile * tiles_per_step
    nt = T // tm
    npos = seq // tm
    row = lambda w: pl.BlockSpec((tm, w), lambda i: (i, 0))
    pos = lambda: pl.BlockSpec((tm, LANES), lambda i: (i % npos, 0))
    full = lambda a: _layer_spec(a, layer)
    widths = (RET_W, RET_W, RET_W, RET_W, SWA_W, SWA_KV_W, MLA_PAD_W, MLA_PAD_W)
    widths_t = (SWA_KV_W, MLA_PAD_W)
    mla_q_scale = (MLA_NOPE + MLA_ROPE) ** -0.5 * math.log2(math.e)
    return pl.pallas_call(
        functools.partial(_inproj_kernel, mla_q_scale=mla_q_scale),
        grid=(nt,),
        in_specs=[row(D_MODEL), full(w1), pos(), pos(), pos(), pos(), full(qnw), full(kvnw),
                  full(wuq), full(wuk), full(wuvt)],
        out_specs=[row(w) for w in widths]
        + [pl.BlockSpec((tiles_per_step, w, tile), lambda i: (i, 0, 0)) for w in widths_t],
        out_shape=[jax.ShapeDtypeStruct((T, w), BF16) for w in widths]
        + [jax.ShapeDtypeStruct((T // tile, w, tile), BF16) for w in widths_t],
        compiler_params=_params("parallel"),
        name="inproj",
    )(x, w1, cr, sr, cm, sm, qnw, kvnw, wuq, wuk, wuvt)


def _retention_kernel(q_ref, k_ref, v_ref, g_ref, gnw_ref, dec_ref, xi_ref, zeta_ref, dm_ref, bm_ref,
                      avg_ref, o_ref, state_ref, *, chunk, n_chunks):
    @pl.when(pl.program_id(1) == 0)
    def _():
        state_ref[...] = jnp.zeros_like(state_ref)

    lane_head = lax.broadcasted_iota(jnp.int32, (chunk, RET_W), 1) // RET_DV
    avg = avg_ref[...]
    chunks = [slice(c * chunk, (c + 1) * chunk) for c in range(n_chunks)]

    def raw_scores(rows):
        q = q_ref[rows, :]
        zero = jnp.zeros_like(q)
        q_heads = jnp.concatenate([jnp.where(lane_head == h, q, zero) for h in range(RET_HEADS)], 0)
        return _dot_nt(q_heads, k_ref[rows, :])

    scores = [raw_scores(rows) for rows in chunks]
    updates = [_dot_tn((k_ref[rows, :].astype(F32) * zeta_ref[...]).astype(BF16), v_ref[rows, :])
               for rows in chunks]
    inner = [_dot((s * dec_ref[...]).astype(BF16), v_ref[rows, :]) for s, rows in zip(scores, chunks)]

    state = state_ref[...]
    outs = []
    for rows, upd, inner_heads in zip(chunks, updates, inner):
        o = _dot((q_ref[rows, :].astype(F32) * xi_ref[...]).astype(BF16), state.astype(BF16))
        for h in range(RET_HEADS):
            o = o + jnp.where(lane_head == h, inner_heads[h * chunk:(h + 1) * chunk], 0.0)
        outs.append(o)
        state = dm_ref[...] * state + bm_ref[...] * upd
    state_ref[...] = state

    mus = [_split_dot(o, avg) for o in outs]
    devs = [o - mu for o, mu in zip(outs, mus)]
    variances = [_split_dot(d * d, avg) for d in devs]
    for rows, d, var in zip(chunks, devs, variances):
        on = d * lax.rsqrt(var + LN_EPS) * gnw_ref[...]
        gate = g_ref[rows, :].astype(F32)
        o_ref[rows, :] = (on * (gate * jax.nn.sigmoid(gate))).astype(BF16)


def _retention_tables(chunk):
    H = RET_HEADS
    gamma = 1.0 - 2.0 ** (-5.0 - np.arange(H, dtype=np.float64))
    log_g = np.log(gamma)
    idx = np.arange(chunk, dtype=np.float64)
    rel = idx[:, None] - idx[None, :]
    dec = np.where(rel[None] >= 0, np.exp(np.maximum(rel, 0.0)[None] * log_g[:, None, None]), 0.0)
    dec = dec.reshape(H * chunk, chunk)
    lane_head = np.arange(RET_W) // RET_DV
    xi = np.exp((idx[:, None] + 1.0) * log_g[lane_head][None, :])
    zeta = np.exp((chunk - 1.0 - idx[:, None]) * log_g[lane_head][None, :])
    same = lane_head[:, None] == lane_head[None, :]
    dm = np.where(same, np.exp(chunk * log_g)[lane_head][:, None], 0.0)
    bm = same.astype(np.float64)
    avg = bm / RET_DV
    f = lambda a: jnp.asarray(a, F32)
    return f(dec), f(xi), f(zeta), f(dm), f(bm), jnp.asarray(avg, BF16)


def _retention(rq, rk, rv, rg, gnw, *, layer, batch, seq, blk=1024, chunk=128):
    T = rq.shape[0]
    nb = seq // blk
    tables = _retention_tables(chunk)
    row = pl.BlockSpec((blk, RET_W), lambda b, i: (b * nb + i, 0))
    full = lambda a: pl.BlockSpec(a.shape, lambda b, i: (0, 0))
    return pl.pallas_call(
        functools.partial(_retention_kernel, chunk=chunk, n_chunks=blk // chunk),
        grid=(batch, nb),
        in_specs=[row, row, row, row, _layer_spec(gnw, layer)] + [full(t) for t in tables],
        out_specs=row,
        out_shape=jax.ShapeDtypeStruct((T, RET_W), BF16),
        scratch_shapes=[pltpu.VMEM((RET_W, RET_W), F32)],
        compiler_params=_params("parallel", "arbitrary"),
        name="retention",
    )(rq, rk, rv, rg, gnw, *tables)


def _swa_kernel(q_ref, kc_ref, kp_ref, vtc_ref, vtp_ref, sink_ref, o_ref, *, blk, seq):
    L = SWA_WINDOW
    n_pairs = SWA_Q_HEADS // 2
    n_sub = blk // L
    lane_lo = lax.broadcasted_iota(jnp.int32, (L, LANES), 1) < HEAD_DIM
    feat_lo = lax.broadcasted_iota(jnp.int32, (LANES, L), 0) < HEAD_DIM
    key = lax.broadcasted_iota(jnp.int32, (2 * L, SWA_Q_HEADS * L), 0)
    qry = lax.broadcasted_iota(jnp.int32, (2 * L, SWA_Q_HEADS * L), 1) % L
    band = (key > qry) & (key <= qry + L)
    sink = sink_ref[...] * math.log2(math.e)

    def scores(sb):
        r0 = sb * L
        k_prev = kp_ref[...] if sb == 0 else kc_ref[r0 - L:r0, :]
        k_win = jnp.concatenate([k_prev, kc_ref[r0:r0 + L, :]], 0)
        pairs = [q_ref[r0:r0 + L, g * LANES:(g + 1) * LANES] for g in range(n_pairs)]
        zero = jnp.zeros_like(pairs[0])
        q_heads = jnp.concatenate([jnp.where(lane_lo, p, zero) for p in pairs]
                                  + [jnp.where(lane_lo, zero, p) for p in pairs], 0)
        return _dot_nt(k_win, q_heads)

    def attend(sb, st):
        r0 = sb * L
        valid = band
        if sb == 0:
            pos0 = (pl.program_id(0) * blk) % seq
            valid = band & (key >= jnp.where(pos0 > 0, 0, L))
        st = jnp.where(valid, st, NEG_INF)
        m = jnp.maximum(jnp.max(st, 0, keepdims=True), sink)
        p = jnp.exp2(st - m)
        denom = jnp.sum(p, 0, keepdims=True) + jnp.exp2(sink - m)
        c, off = divmod(r0, vtc_ref.shape[2])
        if off == 0:
            vt_prev = vtp_ref[0] if c == 0 else vtc_ref[c - 1, :, vtc_ref.shape[2] - L:]
            vt_win = jnp.concatenate([vt_prev, vtc_ref[c, :, :L]], 1)
        else:
            vt_win = vtc_ref[c, :, off - L:off + L]
        ot = _dot(vt_win, p.astype(BF16)) * (1.0 / denom)
        for g in range(n_pairs):
            o_lo = ot[:, g * L:(g + 1) * L]
            o_hi = ot[:, (g + n_pairs) * L:(g + n_pairs + 1) * L]
            o_ref[r0:r0 + L, g * LANES:(g + 1) * LANES] = jnp.where(feat_lo, o_lo, o_hi).T.astype(BF16)

    st = scores(0)
    for sb in range(n_sub):
        st_next = scores(sb + 1) if sb + 1 < n_sub else None
        attend(sb, st)
        st = st_next


def _swa(sq, sk, svt, sink_row, *, layer, seq, blk):
    T = sq.shape[0]
    L = SWA_WINDOW
    per = blk // L
    tile = svt.shape[2]
    tiles = blk // tile
    cur = lambda w: pl.BlockSpec((blk, w), lambda i: (i, 0))
    k_prev = pl.BlockSpec((L, SWA_KV_W), lambda i: (jnp.maximum(i * per - 1, 0), 0))
    vt_cur = pl.BlockSpec((tiles, SWA_KV_W, tile), lambda i: (i, 0, 0))
    vt_prev = pl.BlockSpec((1, SWA_KV_W, L), lambda i: (jnp.maximum(i * tiles - 1, 0), 0, tile // L - 1))
    return pl.pallas_call(
        functools.partial(_swa_kernel, blk=blk, seq=seq),
        grid=(T // blk,),
        in_specs=[cur(SWA_W), cur(SWA_KV_W), k_prev, vt_cur, vt_prev, _layer_spec(sink_row, layer)],
        out_specs=cur(SWA_W),
        out_shape=jax.ShapeDtypeStruct((T, SWA_W), BF16),
        compiler_params=_params("parallel"),
        name="swa",
    )(sq, sk, sk, svt, svt, sink_row)


def _mla_kernel(q_ref, k_ref, vt_ref, o_ref, st0_ref, st1_ref, cmax0_ref, cmax1_ref, m_ref, acc_ref,
                *, tq, tk, heads):
    i = pl.program_id(2)
    head_lanes = [slice(h * LANES, (h + 1) * LANES) for h in range(heads)]
    st_refs = (st0_ref, st1_ref)
    cmax_refs = (cmax0_ref, cmax1_ref)
    all_q = slice(0, tq)
    late_q = slice(tk, tq)

    def scores(j, lanes, qs):
        start = pl.multiple_of(j * tk, tk)
        return _dot_nt(k_ref[pl.ds(start, tk), lanes], q_ref[qs, lanes])

    def step(j, cur, qs=all_q, diagonal=False, next_qs=all_q):
        for h, lanes in enumerate(head_lanes):
            if next_qs is not None:
                st_next = scores(j + 1, lanes, next_qs)
                st_refs[1 - cur][h, :, next_qs] = st_next
                if next_qs is all_q:
                    cmax_refs[1 - cur][h] = jnp.max(st_next, 0, keepdims=True)
            st = st_refs[cur][h, :, qs]
            if diagonal:
                key = lax.broadcasted_iota(jnp.int32, st.shape, 0)
                qry = lax.broadcasted_iota(jnp.int32, st.shape, 1)
                st = jnp.where(key <= qry, st, NEG_INF)
                col_max = jnp.max(st, 0, keepdims=True)
            else:
                col_max = cmax_refs[cur][h]
            m = m_ref[h][:, qs]
            m_new = jnp.maximum(m, col_max)
            pt = jnp.exp2(st - m_new).astype(BF16)
            acc_ref[h, :, qs] = jnp.exp2(m - m_new) * acc_ref[h, :, qs] + _dot(vt_ref[j, lanes, :], pt)
            if qs is all_q:
                m_ref[h] = m_new

    m_ref[...] = jnp.full(m_ref.shape, NEG_INF, F32)
    acc_ref[...] = jnp.zeros(acc_ref.shape, F32)
    for h, lanes in enumerate(head_lanes):
        st_first = scores(0, lanes, all_q)
        st0_ref[h] = st_first
        cmax0_ref[h] = jnp.max(st_first, 0, keepdims=True)

    def pair(p, _):
        step(2 * p, 0)
        step(2 * p + 1, 1)
        return 0

    lax.fori_loop(0, i, pair, 0)
    step(2 * i, 0, diagonal=True, next_qs=late_q)
    step(2 * i + 1, 1, qs=late_q, diagonal=True, next_qs=None)
    for h in range(heads):
        acc = acc_ref[h]
        o_ref[:, h * LANES:(h + 1) * LANES] = (acc / acc[MLA_V:MLA_V + 1, :]).T.astype(BF16)


def _mla(mq, mk, mvt, *, batch, seq, tk, heads=3):
    T = mq.shape[0]
    tq = 2 * tk
    nq = seq // tq
    qo = pl.BlockSpec((tq, heads * LANES), lambda b, h, i: (b * nq + i, h))
    k = pl.BlockSpec((seq, heads * LANES), lambda b, h, i: (b, h))
    vt = pl.BlockSpec((seq // tk, heads * LANES, tk), lambda b, h, i: (b, h, 0))
    return pl.pallas_call(
        functools.partial(_mla_kernel, tq=tq, tk=tk, heads=heads),
        grid=(batch, MLA_HEADS // heads, nq),
        in_specs=[qo, k, vt],
        out_specs=qo,
        out_shape=jax.ShapeDtypeStruct((T, MLA_PAD_W), BF16),
        scratch_shapes=[pltpu.VMEM((heads, tk, tq), F32), pltpu.VMEM((heads, tk, tq), F32),
                        pltpu.VMEM((heads, 1, tq), F32), pltpu.VMEM((heads, 1, tq), F32),
                        pltpu.VMEM((heads, 1, tq), F32), pltpu.VMEM((heads, LANES, tq), F32)],
        compiler_params=_params("parallel", "parallel", "arbitrary"),
        name="mla",
    )(mq, mk, mvt)


def _outproj_kernel(ret_ref, swa_ref, mla_ref, x_ref, wr_ref, ws_ref, wm_ref, g_ref, b_ref,
                    rw_ref, x1_ref, x1b_ref, logits_ref, *, alpha, n_sub):
    tm = x_ref.shape[0]
    subs = [slice(s * tm // n_sub, (s + 1) * tm // n_sub) for s in range(n_sub)]

    def mix(rows):
        return (_dot(ret_ref[rows, :], wr_ref[...]) + _dot(swa_ref[rows, :], ws_ref[...])
                + _dot(mla_ref[rows, :], wm_ref[...]))

    def finish(rows, mixed):
        x1 = _layer_norm(alpha * x_ref[rows, :] + mixed, g_ref[...], b_ref[...])
        x1_ref[rows, :] = x1
        hi = x1.astype(BF16)
        x1b_ref[rows, :] = hi
        lo = (x1 - hi.astype(F32)).astype(BF16)
        hi_part = _dot(hi, rw_ref[...])
        logits_ref[rows, :] = hi_part[:, :LANES] + hi_part[:, LANES:] + _dot(lo, rw_ref[:, :LANES])

    mixed = mix(subs[0])
    for s in range(n_sub):
        mixed_next = mix(subs[s + 1]) if s + 1 < n_sub else None
        finish(subs[s], mixed)
        mixed = mixed_next


def _outproj(ret_o, swa_o, mla_o, x, wr, ws, wm, g, b, rw, *, layer, alpha, tm=1024, n_sub=8):
    T = x.shape[0]
    row = lambda w: pl.BlockSpec((tm, w), lambda i: (i, 0))
    full = lambda a: pl.BlockSpec(a.shape, lambda i: (0, 0))
    per_layer = lambda a: _layer_spec(a, layer)
    return pl.pallas_call(
        functools.partial(_outproj_kernel, alpha=alpha, n_sub=n_sub),
        grid=(T // tm,),
        in_specs=[row(RET_W), row(SWA_W), row(MLA_PAD_W), row(D_MODEL), per_layer(wr), per_layer(ws),
                  per_layer(wm), per_layer(g), per_layer(b), full(rw)],
        out_specs=[row(D_MODEL), row(D_MODEL), row(LANES)],
        out_shape=[jax.ShapeDtypeStruct((T, D_MODEL), F32), jax.ShapeDtypeStruct((T, D_MODEL), BF16),
                   jax.ShapeDtypeStruct((T, LANES), F32)],
        compiler_params=_params("parallel"),
        name="outproj",
    )(ret_o, swa_o, mla_o, x, wr, ws, wm, g, b, rw)


MOE_CHUNK = 128
MOE_SEG_ALIGN = 16


def _moe_rows(tm):
    return -(-(2 * tm + N_EXPERTS * (MOE_SEG_ALIGN - 1) + MOE_CHUNK) // LANES) * LANES


def _ring_partner(x, d, width):
    n = x.shape[0]
    ahead = pltpu.roll(x, n - d, 0)
    if width == n:
        return ahead
    row = lax.broadcasted_iota(jnp.int32, x.shape, 0)
    return jnp.where((row % width) + d < width, ahead, pltpu.roll(x, width - d, 0))


def _router_gates_t(logits_t, bias_col):
    row = lax.broadcasted_iota(jnp.int32, logits_t.shape, 0)
    scores = jax.nn.sigmoid(logits_t)
    biased = scores + bias_col
    G = EXPERTS_PER_GROUP
    member = row % G
    rank = jnp.zeros(logits_t.shape, jnp.int32)
    for d in range(1, G):
        other = _ring_partner(biased, d, G)
        other_first = (member + d) % G < member
        rank = rank + ((other > biased) | ((other == biased) & other_first)).astype(jnp.int32)
    top2 = rank < 2
    part = jnp.where(top2, biased, 0.0)
    grp_score = part
    for d in range(1, G):
        grp_score = grp_score + _ring_partner(part, d, G)
    beaten = jnp.zeros(logits_t.shape, jnp.bool_)
    grp = row // G
    for dg in range(1, N_GROUPS):
        other = _ring_partner(grp_score, dg * G, N_EXPERTS)
        other_first = (grp + dg) % N_GROUPS < grp
        beaten = beaten | (other > grp_score) | ((other == grp_score) & other_first)
    picked = jnp.where(top2 & jnp.logical_not(beaten), scores, 0.0)
    return picked / jnp.sum(picked, 0, keepdims=True)


def _moe_kernel(xb_ref, x_ref, logits_ref, rb_ref, earlier_ref, wgu_hbm, wd_hbm, g_ref, b_ref, o_ref,
                xs_ref, ys_ref, wgu_ref, wd_ref, gu_stage, d_stage, sem, start_ref, chunks_ref,
                *, alpha, layer):
    tm = xb_ref.shape[0]
    R = xs_ref.shape[0]
    big = float(4 * R)

    @pl.when(pl.program_id(0) == 0)
    def _():
        def expert_copies(e, slot):
            return (pltpu.make_async_copy(wgu_hbm.at[layer, e], gu_stage.at[slot], sem.at[0, slot]),
                    pltpu.make_async_copy(wd_hbm.at[layer, e], d_stage.at[slot], sem.at[1, slot]))

        for copy in expert_copies(0, 0):
            copy.start()
        for e in range(N_EXPERTS):
            slot = e % 2
            if e + 1 < N_EXPERTS:
                for copy in expert_copies(e + 1, 1 - slot):
                    copy.start()
            for copy in expert_copies(e, slot):
                copy.wait()
            wgu_ref[e] = gu_stage[slot].astype(BF16)
            wd_ref[e] = d_stage[slot].astype(BF16)

    experts_t = _router_gates_t(logits_ref[...].T[:N_EXPERTS], rb_ref[...])
    gates_t = jnp.concatenate([experts_t, jnp.zeros((LANES - N_EXPERTS, tm), F32)], 0)
    gates = gates_t.T

    def strict_upper(n):
        return lax.broadcasted_iota(jnp.int32, (n, n), 0) < lax.broadcasted_iota(jnp.int32, (n, n), 1)

    def seg_pad(count):
        return jnp.floor((count + (MOE_SEG_ALIGN - 1.0)) * (1.0 / MOE_SEG_ALIGN)) * MOE_SEG_ALIGN

    one = lambda mask: jnp.where(mask, 1.0, 0.0).astype(BF16)

    sel = (gates > 0.0) & (lax.broadcasted_iota(jnp.int32, gates.shape, 1) < N_EXPERTS)
    earlier = earlier_ref[...]
    rank = _dot(earlier, one(sel))
    count = jnp.sum(jnp.where(sel, 1.0, 0.0), 0, keepdims=True)
    pad = jnp.broadcast_to(seg_pad(count), (SUBLANES, LANES)).astype(BF16)
    offset = _dot(pad, one(strict_upper(LANES)))[0:1]
    slot = offset + rank
    slot_a = jnp.min(jnp.where(sel, slot, big), -1, keepdims=True)
    slot_b = jnp.max(jnp.where(sel, slot, -1.0), -1, keepdims=True)
    gate_a = jnp.sum(jnp.where(sel & (slot == slot_a), gates, 0.0), -1, keepdims=True)
    gate_b = jnp.sum(jnp.where(sel & (slot == slot_b), gates, 0.0), -1, keepdims=True)

    sel_t = experts_t > 0.0
    rank_t = _dot_nt(one(sel_t), earlier)
    on_diag = (lax.broadcasted_iota(jnp.int32, (N_EXPERTS, LANES), 0)
               == lax.broadcasted_iota(jnp.int32, (N_EXPERTS, LANES), 1))
    offset_col = jnp.sum(jnp.where(on_diag, offset, 0.0), -1, keepdims=True)
    slot_t = offset_col + rank_t
    slot_a_t = jnp.min(jnp.where(sel_t, slot_t, big), 0, keepdims=True)
    slot_b_t = jnp.max(jnp.where(sel_t, slot_t, -1.0), 0, keepdims=True)

    row_id = lax.broadcasted_iota(jnp.int32, (R, tm), 0)
    sort = one((row_id == slot_a_t.astype(jnp.int32)) | (row_id == slot_b_t.astype(jnp.int32)))
    xs_ref[...] = _dot(sort, xb_ref[...]).astype(BF16)

    @pl.when(pl.program_id(0) == 0)
    def _():
        ys_ref[...] = jnp.zeros_like(ys_ref)

    offset_i = offset.astype(jnp.int32)
    chunks_i = jnp.floor((count + (MOE_CHUNK - 1.0)) * (1.0 / MOE_CHUNK)).astype(jnp.int32)
    for e in range(N_EXPERTS):
        start_ref[e] = offset_i[0, e]
        chunks_ref[e] = chunks_i[0, e]

    def chunk_rows(e, c):
        return pl.ds(pl.multiple_of(start_ref[e] + c * MOE_CHUNK, MOE_SEG_ALIGN), MOE_CHUNK)

    def activation(e, rows):
        h = _dot(xs_ref[rows, :], wgu_ref[e])
        up, lin = h[:, :D_EXPERT], h[:, D_EXPERT:]
        return (up * jax.nn.sigmoid(up) * lin).astype(BF16)

    def project(e, rows, a):
        ys_ref[rows, :] = _dot(a, wd_ref[e]).astype(BF16)

    def later_chunks(e, _):
        def body(c, _):
            rows = chunk_rows(e, c)
            project(e, rows, activation(e, rows))
            return 0
        return lax.fori_loop(1, chunks_ref[e], body, 0)

    lax.fori_loop(0, N_EXPERTS, later_chunks, 0)

    a = activation(0, chunk_rows(0, 0))
    for e in range(N_EXPERTS):
        a_next = activation(e + 1, chunk_rows(e + 1, 0)) if e + 1 < N_EXPERTS else None
        project(e, chunk_rows(e, 0), a)
        a = a_next

    col_id = lax.broadcasted_iota(jnp.int32, (tm, R), 1)
    unsort = jnp.where(col_id == slot_a.astype(jnp.int32), gate_a,
                       jnp.where(col_id == slot_b.astype(jnp.int32), gate_b, 0.0)).astype(BF16)
    o_ref[...] = _layer_norm(alpha * x_ref[...] + _dot(unsort, ys_ref[...]), g_ref[...], b_ref[...])


def _moe(x1b, x1, logits, rb, wgu, wd, g, b, *, layer, alpha, tm=512):
    T = x1.shape[0]
    R = _moe_rows(tm)
    earlier = jnp.asarray(np.tril(np.ones((tm, tm), np.float32), -1), BF16)
    row = lambda w: pl.BlockSpec((tm, w), lambda i: (i, 0))
    full = lambda a: _layer_spec(a, layer)
    assert wgu.dtype == F32 and wd.dtype == F32, "the staging buffers are f32"
    in_hbm = pl.BlockSpec(memory_space=pl.ANY)
    return pl.pallas_call(
        functools.partial(_moe_kernel, alpha=alpha, layer=layer),
        grid=(T // tm,),
        in_specs=[row(D_MODEL), row(D_MODEL), row(LANES), pl.BlockSpec(rb.shape, lambda i: (0, 0)),
                  pl.BlockSpec(earlier.shape, lambda i: (0, 0)), in_hbm, in_hbm, full(g), full(b)],
        out_specs=row(D_MODEL),
        out_shape=jax.ShapeDtypeStruct((T, D_MODEL), F32),
        scratch_shapes=[pltpu.VMEM((R, D_MODEL), BF16), pltpu.VMEM((R, D_MODEL), BF16),
                        pltpu.VMEM(wgu.shape[1:], BF16), pltpu.VMEM(wd.shape[1:], BF16),
                        pltpu.VMEM((2,) + wgu.shape[2:], F32), pltpu.VMEM((2,) + wd.shape[2:], F32),
                        pltpu.SemaphoreType.DMA((2, 2)),
                        pltpu.SMEM((N_EXPERTS,), jnp.int32), pltpu.SMEM((N_EXPERTS,), jnp.int32)],
        compiler_params=_params("arbitrary"),
        name="moe",
    )(x1b, x1, logits, rb, earlier, wgu, wd, g, b)


def _rope_tables(seq):
    pos = np.arange(seq, dtype=np.float64)[:, None]
    lane = np.arange(LANES)

    def table(dim, active, offset):
        j = (lane - offset) % dim
        inv = ROPE_THETA ** (-(2.0 * (j % (dim // 2))) / dim)
        ang = pos * inv[None, :]
        sign = np.where(j < dim // 2, -1.0, 1.0)
        cos = np.where(active[None, :], np.cos(ang), 1.0)
        sin = np.where(active[None, :], np.sin(ang) * sign[None, :], 0.0)
        return jnp.asarray(cos, F32), jnp.asarray(sin, F32)

    cr, sr = table(RET_DK, np.ones(LANES, bool), 0)
    cm, sm = table(MLA_ROPE, (lane >= MLA_NOPE) & (lane < MLA_NOPE + MLA_ROPE), MLA_NOPE)
    return cr, sr, cm, sm


def _stacked_weights(w_in, mla_w_uq, mla_w_ukv, w_out):
    depth = w_in.shape[0]
    sizes = (256, 256, 256, 256, SWA_W, SWA_KV_W, SWA_KV_W, MLA_Q_RANK, MLA_KV_RANK, MLA_ROPE)
    o = [int(v) for v in np.concatenate([[0], np.cumsum(sizes)])]
    order = np.array(_SWA_HEAD_ORDER)
    pad_lanes = lambda a: jnp.pad(a, [(0, 0)] * (a.ndim - 1) + [(0, LANES - a.shape[-1])])

    sq = w_in[:, :, o[4]:o[5]].reshape(depth, D_MODEL, SWA_Q_HEADS, HEAD_DIM)[:, :, order]
    kr = jnp.pad(w_in[:, :, o[9]:o[10]], ((0, 0), (0, 0), (MLA_NOPE, LANES - MLA_NOPE - MLA_ROPE)))
    w1 = jnp.concatenate([w_in[:, :, :o[4]], sq.reshape(depth, D_MODEL, SWA_W), kr, w_in[:, :, o[5]:o[7]],
                          w_in[:, :, o[8]:o[9]], w_in[:, :, o[7]:o[8]]], 2).astype(BF16)

    dq = MLA_NOPE + MLA_ROPE
    wuq = pad_lanes(mla_w_uq.reshape(depth, MLA_Q_RANK, MLA_HEADS, dq)).reshape(depth, MLA_Q_RANK, MLA_PAD_W)
    ukv = mla_w_ukv.reshape(depth, MLA_KV_RANK, MLA_HEADS, MLA_NOPE + MLA_V)
    wuk = pad_lanes(ukv[..., :MLA_NOPE]).reshape(depth, MLA_KV_RANK, MLA_PAD_W)
    wuvt = jnp.swapaxes(pad_lanes(ukv[..., MLA_NOPE:]).reshape(depth, MLA_KV_RANK, MLA_PAD_W), 1, 2)

    wr = w_out[:, :RET_W]
    ws = w_out[:, RET_W:RET_W + SWA_W].reshape(depth, SWA_Q_HEADS, HEAD_DIM, D_MODEL)[:, order]
    wm = w_out[:, RET_W + SWA_W:].reshape(depth, MLA_HEADS, MLA_V, D_MODEL)
    wm = jnp.pad(wm, ((0, 0), (0, 0), (0, LANES - MLA_V), (0, 0)))
    to_bf16 = lambda a: a.astype(BF16)
    return (w1, to_bf16(wuq), to_bf16(wuk), to_bf16(wuvt), to_bf16(wr),
            to_bf16(ws.reshape(depth, SWA_W, D_MODEL)), to_bf16(wm.reshape(depth, MLA_PAD_W, D_MODEL)))


def kernel(x, w_in, ret_gn_w, swa_sinks, mla_q_norm_w, mla_kv_norm_w, mla_w_uq, mla_w_ukv, w_out,
           ln1_g, ln1_b, router_w, router_bias, exp_w_gate_up, exp_w_down, ln2_g, ln2_b):
    batch, seq, d_model = x.shape
    depth = w_in.shape[0]
    assert d_model == D_MODEL and seq % (2 * MLA_TILE) == 0, (x.shape, MLA_TILE)
    assert w_in.shape == (depth, D_MODEL, _C_END - LANES + MLA_ROPE), w_in.shape
    alpha = (2 * depth) ** 0.25
    cr, sr, cm, sm = _rope_tables(seq)

    rw = jnp.pad(router_w, ((0, 0), (0, LANES - N_EXPERTS)))
    rwh = rw.astype(BF16)
    rw = jnp.concatenate([rwh, (rw - rwh.astype(F32)).astype(BF16)], 1)
    rb = router_bias.astype(F32).reshape(N_EXPERTS, 1)

    w1, wuq, wuk, wuvt, wr, ws, wm = _stacked_weights(w_in, mla_w_uq, mla_w_ukv, w_out)
    rows = lambda a: a.astype(F32).reshape(depth, 1, -1)
    qnw, kvnw, gnw = rows(mla_q_norm_w), rows(mla_kv_norm_w), rows(ret_gn_w)
    g1, b1, g2, b2 = rows(ln1_g), rows(ln1_b), rows(ln2_g), rows(ln2_b)
    sink_row = rows(jnp.repeat(swa_sinks, SWA_WINDOW, axis=1))

    t = x.reshape(batch * seq, D_MODEL)
    for l in range(depth):
        rq, rk, rv, rg, sq, sk, mq, mk, svt, mvt = _inproj(
            t, w1, cr, sr, cm, sm, qnw, kvnw, wuq, wuk, wuvt, layer=l, seq=seq, tile=MLA_TILE)
        ret_o = _retention(rq, rk, rv, rg, gnw, layer=l, batch=batch, seq=seq)
        swa_o = _swa(sq, sk, svt, sink_row, layer=l, seq=seq, blk=2 * MLA_TILE)
        mla_o = _mla(mq, mk, mvt, batch=batch, seq=seq, tk=MLA_TILE)
        x1, x1b, logits = _outproj(ret_o, swa_o, mla_o, t, wr, ws, wm, g1, b1, rw, layer=l, alpha=alpha)
        t = _moe(x1b, x1, logits, rb, exp_w_gate_up, exp_w_down, g2, b2, layer=l, alpha=alpha)
    return t.reshape(batch, seq, D_MODEL)
```

```python
import functools
import math

import numpy as np
import jax
import jax.numpy as jnp
from jax import lax
from jax.experimental import pallas as pl
from jax.experimental.pallas import tpu as pltpu

D_MODEL = 1024
HEAD_DIM = 64
ROPE_THETA = 10000.0

RET_HEADS = 4
RET_DK = 64
RET_DV = 64
RET_W = RET_HEADS * RET_DV

SWA_Q_HEADS = 6
SWA_KV_HEADS = 2
SWA_WINDOW = 128
SWA_W = SWA_Q_HEADS * HEAD_DIM
SWA_KV_W = SWA_KV_HEADS * HEAD_DIM

MLA_HEADS = 6
MLA_Q_RANK = 384
MLA_KV_RANK = 256
MLA_NOPE = 64
MLA_ROPE = 32
MLA_V = 64

N_EXPERTS = 16
N_GROUPS = 4
EXPERTS_PER_GROUP = N_EXPERTS // N_GROUPS
D_EXPERT = 256

LN_EPS = 1e-5
RMS_EPS = 1e-6
NEG_INF = -1e30

LANES = 128
SUBLANES = 8
MLA_PAD_W = MLA_HEADS * LANES
MLA_VT_ROWS = 80
MLA_VT_W = MLA_HEADS * MLA_VT_ROWS
MLA_TILE = 512
VMEM_LIMIT = 56 * 1024 * 1024

F32 = jnp.float32
BF16 = jnp.bfloat16

_C_RQ, _C_RK, _C_RV, _C_RG = 0, 256, 512, 768
_C_SQ, _C_KR, _C_SK, _C_SV = 1024, 1408, 1536, 1664
_C_CKV, _C_CQ, _C_END = 1792, 2048, 2432

_SWA_HEAD_ORDER = (0, 3, 1, 4, 2, 5)


def _params(*sem, flags=None):
    return pltpu.CompilerParams(dimension_semantics=sem, vmem_limit_bytes=VMEM_LIMIT, flags=flags)


def _layer_spec(a, layer, **kwargs):
    return pl.BlockSpec((None,) + a.shape[1:], lambda *_: (layer,) + (0,) * (a.ndim - 1), **kwargs)


def _dot(a, b):
    return jnp.dot(a, b, preferred_element_type=F32)


def _dot_nt(a, b):
    return lax.dot_general(a, b, (((1,), (1,)), ((), ())), preferred_element_type=F32)


def _dot_tn(a, b):
    return lax.dot_general(a, b, (((0,), (0,)), ((), ())), preferred_element_type=F32)


def _split_dot(x, w_bf16):
    hi = x.astype(BF16)
    lo = (x - hi.astype(F32)).astype(BF16)
    return _dot(hi, w_bf16) + _dot(lo, w_bf16)


def _rope(h, cos, sin_signed, half):
    fwd = pltpu.roll(h, LANES - half, 1)
    bwd = pltpu.roll(h, half, 1)
    lane = lax.broadcasted_iota(jnp.int32, h.shape, 1)
    rot = jnp.where((lane % (2 * half)) < half, fwd, bwd)
    return h * cos + rot * sin_signed


def _layer_norm(y, g, b):
    mu = jnp.mean(y, -1, keepdims=True)
    d = y - mu
    var = jnp.mean(d * d, -1, keepdims=True)
    return d * lax.rsqrt(var + LN_EPS) * g + b


def _rms_norm(x, g):
    return x * lax.rsqrt(jnp.mean(x * x, -1, keepdims=True) + RMS_EPS) * g


def _inproj_kernel(x_ref, w1_ref, cr_ref, sr_ref, cm_ref, sm_ref, qnw_ref, kvnw_ref,
                   wuq_ref, wuk_ref, wuvt_ref,
                   rq_ref, rk_ref, rv_ref, rg_ref, sq_ref, sk_ref, mq_ref, mk_ref, svt_ref, mvt_ref,
                   *, mla_q_scale):
    tile = mvt_ref.shape[2]
    for s in range(x_ref.shape[0] // tile):
        _inproj_tile(slice(s * tile, (s + 1) * tile), s, x_ref, w1_ref, cr_ref, sr_ref, cm_ref, sm_ref,
                     qnw_ref, kvnw_ref, wuq_ref, wuk_ref, wuvt_ref, rq_ref, rk_ref, rv_ref, rg_ref,
                     sq_ref, sk_ref, mq_ref, mk_ref, svt_ref, mvt_ref, mla_q_scale)


def _inproj_tile(rows, s, x_ref, w1_ref, cr_ref, sr_ref, cm_ref, sm_ref, qnw_ref, kvnw_ref,
                 wuq_ref, wuk_ref, wuvt_ref, rq_ref, rk_ref, rv_ref, rg_ref, sq_ref, sk_ref, mq_ref,
                 mk_ref, svt_ref, mvt_ref, mla_q_scale):
    xb = x_ref[rows, :].astype(BF16)

    def proj(a, b):
        return _dot(xb, w1_ref[:, a:b])

    cq = proj(_C_CQ, _C_END)
    ckv = proj(_C_CKV, _C_CQ)
    rq, rk = proj(_C_RQ, _C_RK), proj(_C_RK, _C_RV)
    rv, rg = proj(_C_RV, _C_RG), proj(_C_RG, _C_SQ)
    sq_kr = proj(_C_SQ, _C_SK)
    sk_sv = proj(_C_SK, _C_CKV)
    cq = _rms_norm(cq, qnw_ref[...]).astype(BF16)
    ckv = _rms_norm(ckv, kvnw_ref[...]).astype(BF16)
    q = _dot(cq, wuq_ref[...])
    kn = _dot(ckv, wuk_ref[...])
    vt = _dot_nt(wuvt_ref[...], ckv)

    cr, sr = cr_ref[rows, :], sr_ref[rows, :]
    for g in range(RET_W // LANES):
        lanes = slice(g * LANES, (g + 1) * LANES)
        rq_ref[rows, lanes] = _rope(rq[:, lanes], cr, sr, RET_DK // 2).astype(BF16)
        rk_ref[rows, lanes] = (_rope(rk[:, lanes], cr, sr, RET_DK // 2) * (RET_DK ** -0.5)).astype(BF16)
    rv_ref[rows, :] = rv.astype(BF16)
    rg_ref[rows, :] = rg.astype(BF16)
    sq_ref[rows, :] = (sq_kr[:, :SWA_W] * (HEAD_DIM ** -0.5 * math.log2(math.e))).astype(BF16)
    sk_ref[rows, :] = sk_sv[:, :SWA_KV_W].astype(BF16)
    svt_ref[s] = sk_sv[:, SWA_KV_W:].T.astype(BF16)

    cm, sm = cm_ref[rows, :], sm_ref[rows, :]
    kpe = _rope(sq_kr[:, SWA_W:], cm, sm, MLA_ROPE // 2)
    for h in range(MLA_HEADS):
        lo, hi = h * LANES, (h + 1) * LANES
        mq_ref[rows, lo:hi] = (_rope(q[:, lo:hi], cm, sm, MLA_ROPE // 2) * mla_q_scale).astype(BF16)
        mk_ref[rows, lo:hi] = (kn[:, lo:hi] + kpe).astype(BF16)
    feat = lax.broadcasted_iota(jnp.int32, vt.shape, 0) % MLA_VT_ROWS
    mvt_ref[s] = (vt + (feat == MLA_V).astype(F32)).astype(BF16)


def _inproj(x, w1, cr, sr, cm, sm, qnw, kvnw, wuq, wuk, wuvt, *, layer, seq, tile, tiles_per_step=2):
    T = x.shape[0]
    tm = tile * tiles_per_step
    nt = T // tm
    npos = seq // tm
    row = lambda w: pl.BlockSpec((tm, w), lambda i: (i, 0))
    pos = lambda: pl.BlockSpec((tm, LANES), lambda i: (i % npos, 0))
    full = lambda a: _layer_spec(a, layer)
    widths = (RET_W, RET_W, RET_W, RET_W, SWA_W, SWA_KV_W, MLA_PAD_W, MLA_PAD_W)
    widths_t = (SWA_KV_W, MLA_VT_W)
    mla_q_scale = (MLA_NOPE + MLA_ROPE) ** -0.5 * math.log2(math.e)
    return pl.pallas_call(
        functools.partial(_inproj_kernel, mla_q_scale=mla_q_scale),
        grid=(nt,),
        in_specs=[row(D_MODEL), full(w1), pos(), pos(), pos(), pos(), full(qnw), full(kvnw),
                  full(wuq), full(wuk), full(wuvt)],
        out_specs=[row(w) for w in widths]
        + [pl.BlockSpec((tiles_per_step, w, tile), lambda i: (i, 0, 0)) for w in widths_t],
        out_shape=[jax.ShapeDtypeStruct((T, w), BF16) for w in widths]
        + [jax.ShapeDtypeStruct((T // tile, w, tile), BF16) for w in widths_t],
        compiler_params=_params("parallel"),
        name="inproj",
    )(x, w1, cr, sr, cm, sm, qnw, kvnw, wuq, wuk, wuvt)


def _retention_kernel(q_ref, k_ref, v_ref, g_ref, gnw_ref, dec_ref, xi_ref, zeta_ref, dm_ref, bm_ref,
                      avg_ref, o_ref, state_ref, *, chunk, n_chunks):
    @pl.when(pl.program_id(1) == 0)
    def _():
        state_ref[...] = jnp.zeros_like(state_ref)

    lane_head = lax.broadcasted_iota(jnp.int32, (chunk, RET_W), 1) // RET_DV
    avg = avg_ref[...]
    chunks = [slice(c * chunk, (c + 1) * chunk) for c in range(n_chunks)]

    def raw_scores(rows):
        q = q_ref[rows, :]
        zero = jnp.zeros_like(q)
        q_heads = jnp.concatenate([jnp.where(lane_head == h, q, zero) for h in range(RET_HEADS)], 0)
        return _dot_nt(q_heads, k_ref[rows, :])

    scores = [raw_scores(rows) for rows in chunks]
    updates = [_dot_tn((k_ref[rows, :].astype(F32) * zeta_ref[...]).astype(BF16), v_ref[rows, :])
               for rows in chunks]
    inner = [_dot((s * dec_ref[...]).astype(BF16), v_ref[rows, :]) for s, rows in zip(scores, chunks)]

    state = state_ref[...]
    outs = []
    for rows, upd, inner_heads in zip(chunks, updates, inner):
        o = _dot((q_ref[rows, :].astype(F32) * xi_ref[...]).astype(BF16), state.astype(BF16))
        for h in range(RET_HEADS):
            o = o + jnp.where(lane_head == h, inner_heads[h * chunk:(h + 1) * chunk], 0.0)
        outs.append(o)
        state = dm_ref[...] * state + bm_ref[...] * upd
    state_ref[...] = state

    mus = [_split_dot(o, avg) for o in outs]
    devs = [o - mu for o, mu in zip(outs, mus)]
    variances = [_split_dot(d * d, avg) for d in devs]
    for rows, d, var in zip(chunks, devs, variances):
        on = d * lax.rsqrt(var + LN_EPS) * gnw_ref[...]
        gate = g_ref[rows, :].astype(F32)
        o_ref[rows, :] = (on * (gate * jax.nn.sigmoid(gate))).astype(BF16)


def _retention_tables(chunk):
    H = RET_HEADS
    gamma = 1.0 - 2.0 ** (-5.0 - np.arange(H, dtype=np.float64))
    log_g = np.log(gamma)
    idx = np.arange(chunk, dtype=np.float64)
    rel = idx[:, None] - idx[None, :]
    dec = np.where(rel[None] >= 0, np.exp(np.maximum(rel, 0.0)[None] * log_g[:, None, None]), 0.0)
    dec = dec.reshape(H * chunk, chunk)
    lane_head = np.arange(RET_W) // RET_DV
    xi = np.exp((idx[:, None] + 1.0) * log_g[lane_head][None, :])
    zeta = np.exp((chunk - 1.0 - idx[:, None]) * log_g[lane_head][None, :])
    same = lane_head[:, None] == lane_head[None, :]
    dm = np.where(same, np.exp(chunk * log_g)[lane_head][:, None], 0.0)
    bm = same.astype(np.float64)
    avg = bm / RET_DV
    f = lambda a: jnp.asarray(a, F32)
    return f(dec), f(xi), f(zeta), f(dm), f(bm), jnp.asarray(avg, BF16)


def _retention(rq, rk, rv, rg, gnw, *, layer, batch, seq, blk=1024, chunk=128):
    T = rq.shape[0]
    nb = seq // blk
    tables = _retention_tables(chunk)
    row = pl.BlockSpec((blk, RET_W), lambda b, i: (b * nb + i, 0))
    full = lambda a: pl.BlockSpec(a.shape, lambda b, i: (0, 0))
    return pl.pallas_call(
        functools.partial(_retention_kernel, chunk=chunk, n_chunks=blk // chunk),
        grid=(batch, nb),
        in_specs=[row, row, row, row, _layer_spec(gnw, layer)] + [full(t) for t in tables],
        out_specs=row,
        out_shape=jax.ShapeDtypeStruct((T, RET_W), BF16),
        scratch_shapes=[pltpu.VMEM((RET_W, RET_W), F32)],
        compiler_params=_params("parallel", "arbitrary"),
        name="retention",
    )(rq, rk, rv, rg, gnw, *tables)


def _swa_kernel(q_ref, kc_ref, kp_ref, vtc_ref, vtp_ref, sink_ref, o_ref, *, blk, seq):
    L = SWA_WINDOW
    n_pairs = SWA_Q_HEADS // 2
    n_sub = blk // L
    lane_lo = lax.broadcasted_iota(jnp.int32, (L, LANES), 1) < HEAD_DIM
    feat_lo = lax.broadcasted_iota(jnp.int32, (LANES, L), 0) < HEAD_DIM
    key = lax.broadcasted_iota(jnp.int32, (2 * L, SWA_Q_HEADS * L), 0)
    qry = lax.broadcasted_iota(jnp.int32, (2 * L, SWA_Q_HEADS * L), 1) % L
    band = (key > qry) & (key <= qry + L)
    sink = sink_ref[...] * math.log2(math.e)

    def scores(sb):
        r0 = sb * L
        k_prev = kp_ref[...] if sb == 0 else kc_ref[r0 - L:r0, :]
        k_win = jnp.concatenate([k_prev, kc_ref[r0:r0 + L, :]], 0)
        pairs = [q_ref[r0:r0 + L, g * LANES:(g + 1) * LANES] for g in range(n_pairs)]
        zero = jnp.zeros_like(pairs[0])
        q_heads = jnp.concatenate([jnp.where(lane_lo, p, zero) for p in pairs]
                                  + [jnp.where(lane_lo, zero, p) for p in pairs], 0)
        return _dot_nt(k_win, q_heads)

    def attend(sb, st):
        r0 = sb * L
        valid = band
        if sb == 0:
            pos0 = (pl.program_id(0) * blk) % seq
            valid = band & (key >= jnp.where(pos0 > 0, 0, L))
        st = jnp.where(valid, st, NEG_INF)
        m = jnp.maximum(jnp.max(st, 0, keepdims=True), sink)
        p = jnp.exp2(st - m)
        denom = jnp.sum(p, 0, keepdims=True) + jnp.exp2(sink - m)
        c, off = divmod(r0, vtc_ref.shape[2])
        if off == 0:
            vt_prev = vtp_ref[0] if c == 0 else vtc_ref[c - 1, :, vtc_ref.shape[2] - L:]
            vt_win = jnp.concatenate([vt_prev, vtc_ref[c, :, :L]], 1)
        else:
            vt_win = vtc_ref[c, :, off - L:off + L]
        ot = _dot(vt_win, p.astype(BF16)) * (1.0 / denom)
        for g in range(n_pairs):
            o_lo = ot[:, g * L:(g + 1) * L]
            o_hi = ot[:, (g + n_pairs) * L:(g + n_pairs + 1) * L]
            o_ref[r0:r0 + L, g * LANES:(g + 1) * LANES] = jnp.where(feat_lo, o_lo, o_hi).T.astype(BF16)

    st = scores(0)
    for sb in range(n_sub):
        st_next = scores(sb + 1) if sb + 1 < n_sub else None
        attend(sb, st)
        st = st_next


def _swa(sq, sk, svt, sink_row, *, layer, seq, blk):
    T = sq.shape[0]
    L = SWA_WINDOW
    per = blk // L
    tile = svt.shape[2]
    tiles = blk // tile
    cur = lambda w: pl.BlockSpec((blk, w), lambda i: (i, 0))
    k_prev = pl.BlockSpec((L, SWA_KV_W), lambda i: (jnp.maximum(i * per - 1, 0), 0))
    vt_cur = pl.BlockSpec((tiles, SWA_KV_W, tile), lambda i: (i, 0, 0))
    vt_prev = pl.BlockSpec((1, SWA_KV_W, L), lambda i: (jnp.maximum(i * tiles - 1, 0), 0, tile // L - 1))
    return pl.pallas_call(
        functools.partial(_swa_kernel, blk=blk, seq=seq),
        grid=(T // blk,),
        in_specs=[cur(SWA_W), cur(SWA_KV_W), k_prev, vt_cur, vt_prev, _layer_spec(sink_row, layer)],
        out_specs=cur(SWA_W),
        out_shape=jax.ShapeDtypeStruct((T, SWA_W), BF16),
        compiler_params=_params("parallel"),
        name="swa",
    )(sq, sk, sk, svt, svt, sink_row)


def _mla_kernel(q_ref, k_ref, vt_ref, o_ref, st0_ref, st1_ref, cmax0_ref, cmax1_ref, m_ref, acc_ref,
                *, tq, tk, heads):
    i = pl.program_id(2)
    head_lanes = [slice(h * LANES, (h + 1) * LANES) for h in range(heads)]
    st_refs = (st0_ref, st1_ref)
    cmax_refs = (cmax0_ref, cmax1_ref)
    all_q = slice(0, tq)
    late_q = slice(tk, tq)

    def scores(j, lanes, qs):
        start = pl.multiple_of(j * tk, tk)
        return _dot_nt(k_ref[pl.ds(start, tk), lanes], q_ref[qs, lanes])

    def step(j, cur, qs=all_q, diagonal=False, next_qs=all_q):
        for h, lanes in enumerate(head_lanes):
            if next_qs is not None:
                st_next = scores(j + 1, lanes, next_qs)
                st_refs[1 - cur][h, :, next_qs] = st_next
                if next_qs is all_q:
                    cmax_refs[1 - cur][h] = jnp.max(st_next, 0, keepdims=True)
            st = st_refs[cur][h, :, qs]
            if diagonal:
                key = lax.broadcasted_iota(jnp.int32, st.shape, 0)
                qry = lax.broadcasted_iota(jnp.int32, st.shape, 1)
                st = jnp.where(key <= qry, st, NEG_INF)
                col_max = jnp.max(st, 0, keepdims=True)
            else:
                col_max = cmax_refs[cur][h]
            m = m_ref[h][:, qs]
            m_new = jnp.maximum(m, col_max)
            pt = jnp.exp2(st - m_new).astype(BF16)
            vt = vt_ref[j, h * MLA_VT_ROWS:(h + 1) * MLA_VT_ROWS, :]
            acc_ref[h, :, qs] = jnp.exp2(m - m_new) * acc_ref[h, :, qs] + _dot(vt, pt)
            if qs is all_q:
                m_ref[h] = m_new

    m_ref[...] = jnp.full(m_ref.shape, NEG_INF, F32)
    acc_ref[...] = jnp.zeros(acc_ref.shape, F32)
    for h, lanes in enumerate(head_lanes):
        st_first = scores(0, lanes, all_q)
        st0_ref[h] = st_first
        cmax0_ref[h] = jnp.max(st_first, 0, keepdims=True)

    def pair(p, _):
        step(2 * p, 0)
        step(2 * p + 1, 1)
        return 0

    lax.fori_loop(0, i, pair, 0)
    step(2 * i, 0, diagonal=True, next_qs=late_q)
    step(2 * i + 1, 1, qs=late_q, diagonal=True, next_qs=None)
    for h in range(heads):
        acc = acc_ref[h]
        out_t = jnp.concatenate([acc[:MLA_V] / acc[MLA_V:MLA_V + 1], jnp.zeros((LANES - MLA_V, tq), F32)], 0)
        o_ref[:, h * LANES:(h + 1) * LANES] = out_t.T.astype(BF16)


def _mla(mq, mk, mvt, *, batch, seq, tk, heads=3):
    T = mq.shape[0]
    tq = 2 * tk
    nq = seq // tq
    qo = pl.BlockSpec((tq, heads * LANES), lambda b, h, i: (b * nq + i, h))
    k = pl.BlockSpec((seq, heads * LANES), lambda b, h, i: (b, h))
    vt = pl.BlockSpec((seq // tk, heads * MLA_VT_ROWS, tk), lambda b, h, i: (b, h, 0))
    return pl.pallas_call(
        functools.partial(_mla_kernel, tq=tq, tk=tk, heads=heads),
        grid=(batch, MLA_HEADS // heads, nq),
        in_specs=[qo, k, vt],
        out_specs=qo,
        out_shape=jax.ShapeDtypeStruct((T, MLA_PAD_W), BF16),
        scratch_shapes=[pltpu.VMEM((heads, tk, tq), F32), pltpu.VMEM((heads, tk, tq), F32),
                        pltpu.VMEM((heads, 1, tq), F32), pltpu.VMEM((heads, 1, tq), F32),
                        pltpu.VMEM((heads, 1, tq), F32), pltpu.VMEM((heads, MLA_VT_ROWS, tq), F32)],
        compiler_params=_params("parallel", "parallel", "arbitrary"),
        name="mla",
    )(mq, mk, mvt)


def _outproj_kernel(ret_ref, swa_ref, mla_ref, x_ref, wr_ref, ws_ref, wm_ref, g_ref, b_ref,
                    rw_ref, x1_ref, x1b_ref, logits_ref, *, alpha, n_sub):
    tm = x_ref.shape[0]
    subs = [slice(s * tm // n_sub, (s + 1) * tm // n_sub) for s in range(n_sub)]

    def mix(rows):
        return (_dot(ret_ref[rows, :], wr_ref[...]) + _dot(swa_ref[rows, :], ws_ref[...])
                + _dot(mla_ref[rows, :], wm_ref[...]))

    def finish(rows, mixed):
        x1 = _layer_norm(alpha * x_ref[rows, :] + mixed, g_ref[...], b_ref[...])
        x1_ref[rows, :] = x1
        hi = x1.astype(BF16)
        x1b_ref[rows, :] = hi
        lo = (x1 - hi.astype(F32)).astype(BF16)
        hi_part = _dot(hi, rw_ref[...])
        logits_ref[rows, :] = hi_part[:, :LANES] + hi_part[:, LANES:] + _dot(lo, rw_ref[:, :LANES])

    mixed = mix(subs[0])
    for s in range(n_sub):
        mixed_next = mix(subs[s + 1]) if s + 1 < n_sub else None
        finish(subs[s], mixed)
        mixed = mixed_next


def _outproj(ret_o, swa_o, mla_o, x, wr, ws, wm, g, b, rw, *, layer, alpha, tm=1024, n_sub=8):
    T = x.shape[0]
    row = lambda w: pl.BlockSpec((tm, w), lambda i: (i, 0))
    full = lambda a: pl.BlockSpec(a.shape, lambda i: (0, 0))
    per_layer = lambda a: _layer_spec(a, layer)
    return pl.pallas_call(
        functools.partial(_outproj_kernel, alpha=alpha, n_sub=n_sub),
        grid=(T // tm,),
        in_specs=[row(RET_W), row(SWA_W), row(MLA_PAD_W), row(D_MODEL), per_layer(wr), per_layer(ws),
                  per_layer(wm), per_layer(g), per_layer(b), full(rw)],
        out_specs=[row(D_MODEL), row(D_MODEL), row(LANES)],
        out_shape=[jax.ShapeDtypeStruct((T, D_MODEL), F32), jax.ShapeDtypeStruct((T, D_MODEL), BF16),
                   jax.ShapeDtypeStruct((T, LANES), F32)],
        compiler_params=_params("parallel"),
        name="outproj",
    )(ret_o, swa_o, mla_o, x, wr, ws, wm, g, b, rw)


MOE_CHUNK = 128
MOE_SEG_ALIGN = 16


def _moe_rows(tm):
    return -(-(2 * tm + N_EXPERTS * (MOE_SEG_ALIGN - 1) + MOE_CHUNK) // LANES) * LANES


def _ring_partner(x, d, width):
    n = x.shape[0]
    ahead = pltpu.roll(x, n - d, 0)
    if width == n:
        return ahead
    row = lax.broadcasted_iota(jnp.int32, x.shape, 0)
    return jnp.where((row % width) + d < width, ahead, pltpu.roll(x, width - d, 0))


def _router_gates_t(logits_t, bias_col):
    row = lax.broadcasted_iota(jnp.int32, logits_t.shape, 0)
    scores = jax.nn.sigmoid(logits_t)
    biased = scores + bias_col
    G = EXPERTS_PER_GROUP
    member = row % G
    rank = jnp.zeros(logits_t.shape, jnp.int32)
    for d in range(1, G):
        other = _ring_partner(biased, d, G)
        other_first = (member + d) % G < member
        rank = rank + ((other > biased) | ((other == biased) & other_first)).astype(jnp.int32)
    top2 = rank < 2
    part = jnp.where(top2, biased, 0.0)
    grp_score = part
    for d in range(1, G):
        grp_score = grp_score + _ring_partner(part, d, G)
    beaten = jnp.zeros(logits_t.shape, jnp.bool_)
    grp = row // G
    for dg in range(1, N_GROUPS):
        other = _ring_partner(grp_score, dg * G, N_EXPERTS)
        other_first = (grp + dg) % N_GROUPS < grp
        beaten = beaten | (other > grp_score) | ((other == grp_score) & other_first)
    picked = jnp.where(top2 & jnp.logical_not(beaten), scores, 0.0)
    return picked / jnp.sum(picked, 0, keepdims=True)


def _moe_kernel(xb_ref, x_ref, logits_ref, rb_ref, earlier_ref, wgu_hbm, wd_hbm, g_ref, b_ref, o_ref,
                xs_ref, ys_ref, wgu_ref, wd_ref, gu_stage, d_stage, sem, start_ref, chunks_ref,
                *, alpha, layer):
    tm = xb_ref.shape[0]
    R = xs_ref.shape[0]
    big = float(4 * R)

    @pl.when(pl.program_id(0) == 0)
    def _():
        def expert_copies(e, slot):
            return (pltpu.make_async_copy(wgu_hbm.at[layer, e], gu_stage.at[slot], sem.at[0, slot]),
                    pltpu.make_async_copy(wd_hbm.at[layer, e], d_stage.at[slot], sem.at[1, slot]))

        for copy in expert_copies(0, 0):
            copy.start()
        for e in range(N_EXPERTS):
            slot = e % 2
            if e + 1 < N_EXPERTS:
                for copy in expert_copies(e + 1, 1 - slot):
                    copy.start()
            for copy in expert_copies(e, slot):
                copy.wait()
            wgu_ref[e] = gu_stage[slot].astype(BF16)
            wd_ref[e] = d_stage[slot].astype(BF16)

    experts_t = _router_gates_t(logits_ref[...].T[:N_EXPERTS], rb_ref[...])
    gates_t = jnp.concatenate([experts_t, jnp.zeros((LANES - N_EXPERTS, tm), F32)], 0)
    gates = gates_t.T

    def strict_upper(n):
        return lax.broadcasted_iota(jnp.int32, (n, n), 0) < lax.broadcasted_iota(jnp.int32, (n, n), 1)

    def seg_pad(count):
        return jnp.floor((count + (MOE_SEG_ALIGN - 1.0)) * (1.0 / MOE_SEG_ALIGN)) * MOE_SEG_ALIGN

    one = lambda mask: jnp.where(mask, 1.0, 0.0).astype(BF16)

    sel = (gates > 0.0) & (lax.broadcasted_iota(jnp.int32, gates.shape, 1) < N_EXPERTS)
    earlier = earlier_ref[...]
    rank = _dot(earlier, one(sel))
    count = jnp.sum(jnp.where(sel, 1.0, 0.0), 0, keepdims=True)
    pad = jnp.broadcast_to(seg_pad(count), (SUBLANES, LANES)).astype(BF16)
    offset = _dot(pad, one(strict_upper(LANES)))[0:1]
    slot = offset + rank
    slot_a = jnp.min(jnp.where(sel, slot, big), -1, keepdims=True)
    slot_b = jnp.max(jnp.where(sel, slot, -1.0), -1, keepdims=True)
    gate_a = jnp.sum(jnp.where(sel & (slot == slot_a), gates, 0.0), -1, keepdims=True)
    gate_b = jnp.sum(jnp.where(sel & (slot == slot_b), gates, 0.0), -1, keepdims=True)

    sel_t = experts_t > 0.0
    rank_t = _dot_nt(one(sel_t), earlier)
    on_diag = (lax.broadcasted_iota(jnp.int32, (N_EXPERTS, LANES), 0)
               == lax.broadcasted_iota(jnp.int32, (N_EXPERTS, LANES), 1))
    offset_col = jnp.sum(jnp.where(on_diag, offset, 0.0), -1, keepdims=True)
    slot_t = offset_col + rank_t
    slot_a_t = jnp.min(jnp.where(sel_t, slot_t, big), 0, keepdims=True)
    slot_b_t = jnp.max(jnp.where(sel_t, slot_t, -1.0), 0, keepdims=True)

    row_id = lax.broadcasted_iota(jnp.int32, (R, tm), 0)
    sort = one((row_id == slot_a_t.astype(jnp.int32)) | (row_id == slot_b_t.astype(jnp.int32)))
    xs_ref[...] = _dot(sort, xb_ref[...]).astype(BF16)

    @pl.when(pl.program_id(0) == 0)
    def _():
        ys_ref[...] = jnp.zeros_like(ys_ref)

    offset_i = offset.astype(jnp.int32)
    chunks_i = jnp.floor((count + (MOE_CHUNK - 1.0)) * (1.0 / MOE_CHUNK)).astype(jnp.int32)
    for e in range(N_EXPERTS):
        start_ref[e] = offset_i[0, e]
        chunks_ref[e] = chunks_i[0, e]

    def chunk_rows(e, c):
        return pl.ds(pl.multiple_of(start_ref[e] + c * MOE_CHUNK, MOE_SEG_ALIGN), MOE_CHUNK)

    def activation(e, rows):
        h = _dot(xs_ref[rows, :], wgu_ref[e])
        up, lin = h[:, :D_EXPERT], h[:, D_EXPERT:]
        return (up * jax.nn.sigmoid(up) * lin).astype(BF16)

    def project(e, rows, a):
        ys_ref[rows, :] = _dot(a, wd_ref[e]).astype(BF16)

    def later_chunks(e, _):
        def body(c, _):
            rows = chunk_rows(e, c)
            project(e, rows, activation(e, rows))
            return 0
        return lax.fori_loop(1, chunks_ref[e], body, 0)

    lax.fori_loop(0, N_EXPERTS, later_chunks, 0)

    a = activation(0, chunk_rows(0, 0))
    for e in range(N_EXPERTS):
        a_next = activation(e + 1, chunk_rows(e + 1, 0)) if e + 1 < N_EXPERTS else None
        project(e, chunk_rows(e, 0), a)
        a = a_next

    col_id = lax.broadcasted_iota(jnp.int32, (tm, R), 1)
    unsort = jnp.where(col_id == slot_a.astype(jnp.int32), gate_a,
                       jnp.where(col_id == slot_b.astype(jnp.int32), gate_b, 0.0)).astype(BF16)
    o_ref[...] = _layer_norm(alpha * x_ref[...] + _dot(unsort, ys_ref[...]), g_ref[...], b_ref[...])


def _moe(x1b, x1, logits, rb, wgu, wd, g, b, *, layer, alpha, tm=512):
    T = x1.shape[0]
    R = _moe_rows(tm)
    earlier = jnp.asarray(np.tril(np.ones((tm, tm), np.float32), -1), BF16)
    row = lambda w: pl.BlockSpec((tm, w), lambda i: (i, 0))
    full = lambda a: _layer_spec(a, layer)
    assert wgu.dtype == F32 and wd.dtype == F32, "the staging buffers are f32"
    in_hbm = pl.BlockSpec(memory_space=pl.ANY)
    return pl.pallas_call(
        functools.partial(_moe_kernel, alpha=alpha, layer=layer),
        grid=(T // tm,),
        in_specs=[row(D_MODEL), row(D_MODEL), row(LANES), pl.BlockSpec(rb.shape, lambda i: (0, 0)),
                  pl.BlockSpec(earlier.shape, lambda i: (0, 0)), in_hbm, in_hbm, full(g), full(b)],
        out_specs=row(D_MODEL),
        out_shape=jax.ShapeDtypeStruct((T, D_MODEL), F32),
        scratch_shapes=[pltpu.VMEM((R, D_MODEL), BF16), pltpu.VMEM((R, D_MODEL), BF16),
                        pltpu.VMEM(wgu.shape[1:], BF16), pltpu.VMEM(wd.shape[1:], BF16),
                        pltpu.VMEM((2,) + wgu.shape[2:], F32), pltpu.VMEM((2,) + wd.shape[2:], F32),
                        pltpu.SemaphoreType.DMA((2, 2)),
                        pltpu.SMEM((N_EXPERTS,), jnp.int32), pltpu.SMEM((N_EXPERTS,), jnp.int32)],
        compiler_params=_params("arbitrary"),
        name="moe",
    )(x1b, x1, logits, rb, earlier, wgu, wd, g, b)


def _rope_tables(seq):
    pos = np.arange(seq, dtype=np.float64)[:, None]
    lane = np.arange(LANES)

    def table(dim, active, offset):
        j = (lane - offset) % dim
        inv = ROPE_THETA ** (-(2.0 * (j % (dim // 2))) / dim)
        ang = pos * inv[None, :]
        sign = np.where(j < dim // 2, -1.0, 1.0)
        cos = np.where(active[None, :], np.cos(ang), 1.0)
        sin = np.where(active[None, :], np.sin(ang) * sign[None, :], 0.0)
        return jnp.asarray(cos, F32), jnp.asarray(sin, F32)

    cr, sr = table(RET_DK, np.ones(LANES, bool), 0)
    cm, sm = table(MLA_ROPE, (lane >= MLA_NOPE) & (lane < MLA_NOPE + MLA_ROPE), MLA_NOPE)
    return cr, sr, cm, sm


def _stacked_weights(w_in, mla_w_uq, mla_w_ukv, w_out):
    depth = w_in.shape[0]
    sizes = (256, 256, 256, 256, SWA_W, SWA_KV_W, SWA_KV_W, MLA_Q_RANK, MLA_KV_RANK, MLA_ROPE)
    o = [int(v) for v in np.concatenate([[0], np.cumsum(sizes)])]
    order = np.array(_SWA_HEAD_ORDER)
    pad_lanes = lambda a: jnp.pad(a, [(0, 0)] * (a.ndim - 1) + [(0, LANES - a.shape[-1])])

    sq = w_in[:, :, o[4]:o[5]].reshape(depth, D_MODEL, SWA_Q_HEADS, HEAD_DIM)[:, :, order]
    kr = jnp.pad(w_in[:, :, o[9]:o[10]], ((0, 0), (0, 0), (MLA_NOPE, LANES - MLA_NOPE - MLA_ROPE)))
    w1 = jnp.concatenate([w_in[:, :, :o[4]], sq.reshape(depth, D_MODEL, SWA_W), kr, w_in[:, :, o[5]:o[7]],
                          w_in[:, :, o[8]:o[9]], w_in[:, :, o[7]:o[8]]], 2).astype(BF16)

    dq = MLA_NOPE + MLA_ROPE
    wuq = pad_lanes(mla_w_uq.reshape(depth, MLA_Q_RANK, MLA_HEADS, dq)).reshape(depth, MLA_Q_RANK, MLA_PAD_W)
    ukv = mla_w_ukv.reshape(depth, MLA_KV_RANK, MLA_HEADS, MLA_NOPE + MLA_V)
    wuk = pad_lanes(ukv[..., :MLA_NOPE]).reshape(depth, MLA_KV_RANK, MLA_PAD_W)
    wuv = jnp.pad(ukv[..., MLA_NOPE:], ((0, 0), (0, 0), (0, 0), (0, MLA_VT_ROWS - MLA_V)))
    wuvt = jnp.swapaxes(wuv.reshape(depth, MLA_KV_RANK, MLA_VT_W), 1, 2)

    wr = w_out[:, :RET_W]
    ws = w_out[:, RET_W:RET_W + SWA_W].reshape(depth, SWA_Q_HEADS, HEAD_DIM, D_MODEL)[:, order]
    wm = w_out[:, RET_W + SWA_W:].reshape(depth, MLA_HEADS, MLA_V, D_MODEL)
    wm = jnp.pad(wm, ((0, 0), (0, 0), (0, LANES - MLA_V), (0, 0)))
    to_bf16 = lambda a: a.astype(BF16)
    return (w1, to_bf16(wuq), to_bf16(wuk), to_bf16(wuvt), to_bf16(wr),
            to_bf16(ws.reshape(depth, SWA_W, D_MODEL)), to_bf16(wm.reshape(depth, MLA_PAD_W, D_MODEL)))


def kernel(x, w_in, ret_gn_w, swa_sinks, mla_q_norm_w, mla_kv_norm_w, mla_w_uq, mla_w_ukv, w_out,
           ln1_g, ln1_b, router_w, router_bias, exp_w_gate_up, exp_w_down, ln2_g, ln2_b):
    batch, seq, d_model = x.shape
    depth = w_in.shape[0]
    assert d_model == D_MODEL and seq % (2 * MLA_TILE) == 0, (x.shape, MLA_TILE)
    assert w_in.shape == (depth, D_MODEL, _C_END - LANES + MLA_ROPE), w_in.shape
    alpha = (2 * depth) ** 0.25
    cr, sr, cm, sm = _rope_tables(seq)

    rw = jnp.pad(router_w, ((0, 0), (0, LANES - N_EXPERTS)))
    rwh = rw.astype(BF16)
    rw = jnp.concatenate([rwh, (rw - rwh.astype(F32)).astype(BF16)], 1)
    rb = router_bias.astype(F32).reshape(N_EXPERTS, 1)

    w1, wuq, wuk, wuvt, wr, ws, wm = _stacked_weights(w_in, mla_w_uq, mla_w_ukv, w_out)
    rows = lambda a: a.astype(F32).reshape(depth, 1, -1)
    qnw, kvnw, gnw = rows(mla_q_norm_w), rows(mla_kv_norm_w), rows(ret_gn_w)
    g1, b1, g2, b2 = rows(ln1_g), rows(ln1_b), rows(ln2_g), rows(ln2_b)
    sink_row = rows(jnp.repeat(swa_sinks, SWA_WINDOW, axis=1))

    t = x.reshape(batch * seq, D_MODEL)
    for l in range(depth):
        rq, rk, rv, rg, sq, sk, mq, mk, svt, mvt = _inproj(
            t, w1, cr, sr, cm, sm, qnw, kvnw, wuq, wuk, wuvt, layer=l, seq=seq, tile=MLA_TILE)
        ret_o = _retention(rq, rk, rv, rg, gnw, layer=l, batch=batch, seq=seq)
        swa_o = _swa(sq, sk, svt, sink_row, layer=l, seq=seq, blk=2 * MLA_TILE)
        mla_o = _mla(mq, mk, mvt, batch=batch, seq=seq, tk=MLA_TILE)
        x1, x1b, logits = _outproj(ret_o, swa_o, mla_o, t, wr, ws, wm, g1, b1, rw, layer=l, alpha=alpha)
        t = _moe(x1b, x1, logits, rb, exp_w_gate_up, exp_w_down, g2, b2, layer=l, alpha=alpha)
    return t.reshape(batch, seq, D_MODEL)
```

```python
import functools
import math

import numpy as np
import jax
import jax.numpy as jnp
from jax import lax
from jax.experimental import pallas as pl
from jax.experimental.pallas import tpu as pltpu

D_MODEL = 1024
HEAD_DIM = 64
ROPE_THETA = 10000.0

RET_HEADS = 4
RET_DK = 64
RET_DV = 64
RET_W = RET_HEADS * RET_DV

SWA_Q_HEADS = 6
SWA_KV_HEADS = 2
SWA_WINDOW = 128
SWA_W = SWA_Q_HEADS * HEAD_DIM
SWA_KV_W = SWA_KV_HEADS * HEAD_DIM

MLA_HEADS = 6
MLA_Q_RANK = 384
MLA_KV_RANK = 256
MLA_NOPE = 64
MLA_ROPE = 32
MLA_V = 64

N_EXPERTS = 16
N_GROUPS = 4
EXPERTS_PER_GROUP = N_EXPERTS // N_GROUPS
D_EXPERT = 256

LN_EPS = 1e-5
RMS_EPS = 1e-6
NEG_INF = -1e30

LANES = 128
SUBLANES = 8
MLA_PAD_W = MLA_HEADS * LANES
MLA_TILE = 512
VMEM_LIMIT = 56 * 1024 * 1024

F32 = jnp.float32
BF16 = jnp.bfloat16

_C_RQ, _C_RK, _C_RV, _C_RG = 0, 256, 512, 768
_C_SQ, _C_KR, _C_SK, _C_SV = 1024, 1408, 1536, 1664
_C_CKV, _C_CQ, _C_END = 1792, 2048, 2432

_SWA_HEAD_ORDER = (0, 3, 1, 4, 2, 5)


def _params(*sem, flags=None):
    return pltpu.CompilerParams(dimension_semantics=sem, vmem_limit_bytes=VMEM_LIMIT, flags=flags)


def _layer_spec(a, layer, **kwargs):
    return pl.BlockSpec((None,) + a.shape[1:], lambda *_: (layer,) + (0,) * (a.ndim - 1), **kwargs)


def _dot(a, b):
    return jnp.dot(a, b, preferred_element_type=F32)


def _dot_nt(a, b):
    return lax.dot_general(a, b, (((1,), (1,)), ((), ())), preferred_element_type=F32)


def _dot_tn(a, b):
    return lax.dot_general(a, b, (((0,), (0,)), ((), ())), preferred_element_type=F32)


def _split_dot(x, w_bf16):
    hi = x.astype(BF16)
    lo = (x - hi.astype(F32)).astype(BF16)
    return _dot(hi, w_bf16) + _dot(lo, w_bf16)


def _rope(h, cos, sin_signed, half):
    fwd = pltpu.roll(h, LANES - half, 1)
    bwd = pltpu.roll(h, half, 1)
    lane = lax.broadcasted_iota(jnp.int32, h.shape, 1)
    rot = jnp.where((lane % (2 * half)) < half, fwd, bwd)
    return h * cos + rot * sin_signed


def _layer_norm(y, g, b):
    mu = jnp.mean(y, -1, keepdims=True)
    d = y - mu
    var = jnp.mean(d * d, -1, keepdims=True)
    return d * lax.rsqrt(var + LN_EPS) * g + b


def _rms_norm(x, g):
    return x * lax.rsqrt(jnp.mean(x * x, -1, keepdims=True) + RMS_EPS) * g


def _inproj_kernel(x_ref, w1_ref, cr_ref, sr_ref, cm_ref, sm_ref, qnw_ref, kvnw_ref,
                   wuq_ref, wuk_ref, wuvt_ref,
                   rq_ref, rk_ref, rv_ref, rg_ref, sq_ref, sk_ref, mq_ref, mk_ref, svt_ref, mvt_ref,
                   *, mla_q_scale):
    tile = mvt_ref.shape[2]
    for s in range(x_ref.shape[0] // tile):
        _inproj_tile(slice(s * tile, (s + 1) * tile), s, x_ref, w1_ref, cr_ref, sr_ref, cm_ref, sm_ref,
                     qnw_ref, kvnw_ref, wuq_ref, wuk_ref, wuvt_ref, rq_ref, rk_ref, rv_ref, rg_ref,
                     sq_ref, sk_ref, mq_ref, mk_ref, svt_ref, mvt_ref, mla_q_scale)


def _inproj_tile(rows, s, x_ref, w1_ref, cr_ref, sr_ref, cm_ref, sm_ref, qnw_ref, kvnw_ref,
                 wuq_ref, wuk_ref, wuvt_ref, rq_ref, rk_ref, rv_ref, rg_ref, sq_ref, sk_ref, mq_ref,
                 mk_ref, svt_ref, mvt_ref, mla_q_scale):
    xb = x_ref[rows, :].astype(BF16)

    def proj(a, b):
        return _dot(xb, w1_ref[:, a:b])

    cq = proj(_C_CQ, _C_END)
    ckv = proj(_C_CKV, _C_CQ)
    rq, rk = proj(_C_RQ, _C_RK), proj(_C_RK, _C_RV)
    rv, rg = proj(_C_RV, _C_RG), proj(_C_RG, _C_SQ)
    sq_kr = proj(_C_SQ, _C_SK)
    sk_sv = proj(_C_SK, _C_CKV)
    cq = _rms_norm(cq, qnw_ref[...]).astype(BF16)
    ckv = _rms_norm(ckv, kvnw_ref[...]).astype(BF16)
    q = _dot(cq, wuq_ref[...])
    kn = _dot(ckv, wuk_ref[...])
    vt = _dot_nt(wuvt_ref[...], ckv)

    cr, sr = cr_ref[rows, :], sr_ref[rows, :]
    for g in range(RET_W // LANES):
        lanes = slice(g * LANES, (g + 1) * LANES)
        rq_ref[rows, lanes] = _rope(rq[:, lanes], cr, sr, RET_DK // 2).astype(BF16)
        rk_ref[rows, lanes] = (_rope(rk[:, lanes], cr, sr, RET_DK // 2) * (RET_DK ** -0.5)).astype(BF16)
    rv_ref[rows, :] = rv.astype(BF16)
    rg_ref[rows, :] = rg.astype(BF16)
    sq_ref[rows, :] = (sq_kr[:, :SWA_W] * (HEAD_DIM ** -0.5 * math.log2(math.e))).astype(BF16)
    sk_ref[rows, :] = sk_sv[:, :SWA_KV_W].astype(BF16)
    svt_ref[s] = sk_sv[:, SWA_KV_W:].T.astype(BF16)

    cm, sm = cm_ref[rows, :], sm_ref[rows, :]
    kpe = _rope(sq_kr[:, SWA_W:], cm, sm, MLA_ROPE // 2)
    for h in range(MLA_HEADS):
        lo, hi = h * LANES, (h + 1) * LANES
        mq_ref[rows, lo:hi] = (_rope(q[:, lo:hi], cm, sm, MLA_ROPE // 2) * mla_q_scale).astype(BF16)
        mk_ref[rows, lo:hi] = (kn[:, lo:hi] + kpe).astype(BF16)
    feat = lax.broadcasted_iota(jnp.int32, vt.shape, 0) % LANES
    mvt_ref[s] = (vt + (feat == MLA_V).astype(F32)).astype(BF16)


def _inproj(x, w1, cr, sr, cm, sm, qnw, kvnw, wuq, wuk, wuvt, *, layer, seq, tile, tiles_per_step=2):
    T = x.shape[0]
    tm = tile * tiles_per_step
    nt = T // tm
    npos = seq // tm
    row = lambda w: pl.BlockSpec((tm, w), lambda i: (i, 0))
    pos = lambda: pl.BlockSpec((tm, LANES), lambda i: (i % npos, 0))
    full = lambda a: _layer_spec(a, layer)
    widths = (RET_W, RET_W, RET_W, RET_W, SWA_W, SWA_KV_W, MLA_PAD_W, MLA_PAD_W)
    widths_t = (SWA_KV_W, MLA_PAD_W)
    mla_q_scale = (MLA_NOPE + MLA_ROPE) ** -0.5 * math.log2(math.e)
    return pl.pallas_call(
        functools.partial(_inproj_kernel, mla_q_scale=mla_q_scale),
        grid=(nt,),
        in_specs=[row(D_MODEL), full(w1), pos(), pos(), pos(), pos(), full(qnw), full(kvnw),
                  full(wuq), full(wuk), full(wuvt)],
        out_specs=[row(w) for w in widths]
        + [pl.BlockSpec((tiles_per_step, w, tile), lambda i: (i, 0, 0)) for w in widths_t],
        out_shape=[jax.ShapeDtypeStruct((T, w), BF16) for w in widths]
        + [jax.ShapeDtypeStruct((T // tile, w, tile), BF16) for w in widths_t],
        compiler_params=_params("parallel"),
        name="inproj",
    )(x, w1, cr, sr, cm, sm, qnw, kvnw, wuq, wuk, wuvt)


def _retention_kernel(q_ref, k_ref, v_ref, g_ref, gnw_ref, dec_ref, xi_ref, zeta_ref, dm_ref, bm_ref,
                      avg_ref, o_ref, state_ref, *, chunk, n_chunks):
    @pl.when(pl.program_id(1) == 0)
    def _():
        state_ref[...] = jnp.zeros_like(state_ref)

    lane_head = lax.broadcasted_iota(jnp.int32, (chunk, RET_W), 1) // RET_DV
    avg = avg_ref[...]
    chunks = [slice(c * chunk, (c + 1) * chunk) for c in range(n_chunks)]

    def raw_scores(rows):
        q = q_ref[rows, :]
        zero = jnp.zeros_like(q)
        q_heads = jnp.concatenate([jnp.where(lane_head == h, q, zero) for h in range(RET_HEADS)], 0)
        return _dot_nt(q_heads, k_ref[rows, :])

    scores = [raw_scores(rows) for rows in chunks]
    updates = [_dot_tn((k_ref[rows, :].astype(F32) * zeta_ref[...]).astype(BF16), v_ref[rows, :])
               for rows in chunks]
    inner = [_dot((s * dec_ref[...]).astype(BF16), v_ref[rows, :]) for s, rows in zip(scores, chunks)]

    state = state_ref[...]
    outs = []
    for rows, upd, inner_heads in zip(chunks, updates, inner):
        o = _dot((q_ref[rows, :].astype(F32) * xi_ref[...]).astype(BF16), state.astype(BF16))
        for h in range(RET_HEADS):
            o = o + jnp.where(lane_head == h, inner_heads[h * chunk:(h + 1) * chunk], 0.0)
        outs.append(o)
        state = dm_ref[...] * state + bm_ref[...] * upd
    state_ref[...] = state

    mus = [_split_dot(o, avg) for o in outs]
    devs = [o - mu for o, mu in zip(outs, mus)]
    variances = [_split_dot(d * d, avg) for d in devs]
    for rows, d, var in zip(chunks, devs, variances):
        on = d * lax.rsqrt(var + LN_EPS) * gnw_ref[...]
        gate = g_ref[rows, :].astype(F32)
        o_ref[rows, :] = (on * (gate * jax.nn.sigmoid(gate))).astype(BF16)


def _retention_tables(chunk):
    H = RET_HEADS
    gamma = 1.0 - 2.0 ** (-5.0 - np.arange(H, dtype=np.float64))
    log_g = np.log(gamma)
    idx = np.arange(chunk, dtype=np.float64)
    rel = idx[:, None] - idx[None, :]
    dec = np.where(rel[None] >= 0, np.exp(np.maximum(rel, 0.0)[None] * log_g[:, None, None]), 0.0)
    dec = dec.reshape(H * chunk, chunk)
    lane_head = np.arange(RET_W) // RET_DV
    xi = np.exp((idx[:, None] + 1.0) * log_g[lane_head][None, :])
    zeta = np.exp((chunk - 1.0 - idx[:, None]) * log_g[lane_head][None, :])
    same = lane_head[:, None] == lane_head[None, :]
    dm = np.where(same, np.exp(chunk * log_g)[lane_head][:, None], 0.0)
    bm = same.astype(np.float64)
    avg = bm / RET_DV
    f = lambda a: jnp.asarray(a, F32)
    return f(dec), f(xi), f(zeta), f(dm), f(bm), jnp.asarray(avg, BF16)


def _retention(rq, rk, rv, rg, gnw, *, layer, batch, seq, blk=2048, chunk=128):
    T = rq.shape[0]
    nb = seq // blk
    tables = _retention_tables(chunk)
    row = pl.BlockSpec((blk, RET_W), lambda b, i: (b * nb + i, 0))
    full = lambda a: pl.BlockSpec(a.shape, lambda b, i: (0, 0))
    return pl.pallas_call(
        functools.partial(_retention_kernel, chunk=chunk, n_chunks=blk // chunk),
        grid=(batch, nb),
        in_specs=[row, row, row, row, _layer_spec(gnw, layer)] + [full(t) for t in tables],
        out_specs=row,
        out_shape=jax.ShapeDtypeStruct((T, RET_W), BF16),
        scratch_shapes=[pltpu.VMEM((RET_W, RET_W), F32)],
        compiler_params=_params("parallel", "arbitrary"),
        name="retention",
    )(rq, rk, rv, rg, gnw, *tables)


def _swa_kernel(q_ref, kc_ref, kp_ref, vtc_ref, vtp_ref, sink_ref, o_ref, *, blk, seq):
    L = SWA_WINDOW
    n_pairs = SWA_Q_HEADS // 2
    n_sub = blk // L
    lane_lo = lax.broadcasted_iota(jnp.int32, (L, LANES), 1) < HEAD_DIM
    feat_lo = lax.broadcasted_iota(jnp.int32, (LANES, L), 0) < HEAD_DIM
    key = lax.broadcasted_iota(jnp.int32, (2 * L, SWA_Q_HEADS * L), 0)
    qry = lax.broadcasted_iota(jnp.int32, (2 * L, SWA_Q_HEADS * L), 1) % L
    band = (key > qry) & (key <= qry + L)
    sink = sink_ref[...] * math.log2(math.e)

    def scores(sb):
        r0 = sb * L
        k_prev = kp_ref[...] if sb == 0 else kc_ref[r0 - L:r0, :]
        k_win = jnp.concatenate([k_prev, kc_ref[r0:r0 + L, :]], 0)
        pairs = [q_ref[r0:r0 + L, g * LANES:(g + 1) * LANES] for g in range(n_pairs)]
        zero = jnp.zeros_like(pairs[0])
        q_heads = jnp.concatenate([jnp.where(lane_lo, p, zero) for p in pairs]
                                  + [jnp.where(lane_lo, zero, p) for p in pairs], 0)
        return _dot_nt(k_win, q_heads)

    def attend(sb, st):
        r0 = sb * L
        valid = band
        if sb == 0:
            pos0 = (pl.program_id(0) * blk) % seq
            valid = band & (key >= jnp.where(pos0 > 0, 0, L))
        st = jnp.where(valid, st, NEG_INF)
        m = jnp.maximum(jnp.max(st, 0, keepdims=True), sink)
        p = jnp.exp2(st - m)
        denom = jnp.sum(p, 0, keepdims=True) + jnp.exp2(sink - m)
        c, off = divmod(r0, vtc_ref.shape[2])
        if off == 0:
            vt_prev = vtp_ref[0] if c == 0 else vtc_ref[c - 1, :, vtc_ref.shape[2] - L:]
            vt_win = jnp.concatenate([vt_prev, vtc_ref[c, :, :L]], 1)
        else:
            vt_win = vtc_ref[c, :, off - L:off + L]
        ot = _dot(vt_win, p.astype(BF16)) * (1.0 / denom)
        for g in range(n_pairs):
            o_lo = ot[:, g * L:(g + 1) * L]
            o_hi = ot[:, (g + n_pairs) * L:(g + n_pairs + 1) * L]
            o_ref[r0:r0 + L, g * LANES:(g + 1) * LANES] = jnp.where(feat_lo, o_lo, o_hi).T.astype(BF16)

    st = scores(0)
    for sb in range(n_sub):
        st_next = scores(sb + 1) if sb + 1 < n_sub else None
        attend(sb, st)
        st = st_next


def _swa(sq, sk, svt, sink_row, *, layer, seq, blk):
    T = sq.shape[0]
    L = SWA_WINDOW
    per = blk // L
    tile = svt.shape[2]
    tiles = blk // tile
    cur = lambda w: pl.BlockSpec((blk, w), lambda i: (i, 0))
    k_prev = pl.BlockSpec((L, SWA_KV_W), lambda i: (jnp.maximum(i * per - 1, 0), 0))
    vt_cur = pl.BlockSpec((tiles, SWA_KV_W, tile), lambda i: (i, 0, 0))
    vt_prev = pl.BlockSpec((1, SWA_KV_W, L), lambda i: (jnp.maximum(i * tiles - 1, 0), 0, tile // L - 1))
    return pl.pallas_call(
        functools.partial(_swa_kernel, blk=blk, seq=seq),
        grid=(T // blk,),
        in_specs=[cur(SWA_W), cur(SWA_KV_W), k_prev, vt_cur, vt_prev, _layer_spec(sink_row, layer)],
        out_specs=cur(SWA_W),
        out_shape=jax.ShapeDtypeStruct((T, SWA_W), BF16),
        compiler_params=_params("parallel"),
        name="swa",
    )(sq, sk, sk, svt, svt, sink_row)


def _mla_kernel(q_ref, k_ref, vt_ref, o_ref, st0_ref, st1_ref, cmax0_ref, cmax1_ref, m_ref, acc_ref,
                *, tq, tk, heads):
    i = pl.program_id(2)
    head_lanes = [slice(h * LANES, (h + 1) * LANES) for h in range(heads)]
    st_refs = (st0_ref, st1_ref)
    cmax_refs = (cmax0_ref, cmax1_ref)
    all_q = slice(0, tq)
    late_q = slice(tk, tq)

    def scores(j, lanes, qs):
        start = pl.multiple_of(j * tk, tk)
        return _dot_nt(k_ref[pl.ds(start, tk), lanes], q_ref[qs, lanes])

    def step(j, cur, qs=all_q, diagonal=False, next_qs=all_q):
        for h, lanes in enumerate(head_lanes):
            if next_qs is not None:
                st_next = scores(j + 1, lanes, next_qs)
                st_refs[1 - cur][h, :, next_qs] = st_next
                if next_qs is all_q:
                    cmax_refs[1 - cur][h] = jnp.max(st_next, 0, keepdims=True)
            st = st_refs[cur][h, :, qs]
            if diagonal:
                key = lax.broadcasted_iota(jnp.int32, st.shape, 0)
                qry = lax.broadcasted_iota(jnp.int32, st.shape, 1)
                st = jnp.where(key <= qry, st, NEG_INF)
                col_max = jnp.max(st, 0, keepdims=True)
            else:
                col_max = cmax_refs[cur][h]
            m = m_ref[h][:, qs]
            m_new = jnp.maximum(m, col_max)
            pt = jnp.exp2(st - m_new).astype(BF16)
            acc_ref[h, :, qs] = jnp.exp2(m - m_new) * acc_ref[h, :, qs] + _dot(vt_ref[j, lanes, :], pt)
            if qs is all_q:
                m_ref[h] = m_new

    m_ref[...] = jnp.full(m_ref.shape, NEG_INF, F32)
    acc_ref[...] = jnp.zeros(acc_ref.shape, F32)
    for h, lanes in enumerate(head_lanes):
        st_first = scores(0, lanes, all_q)
        st0_ref[h] = st_first
        cmax0_ref[h] = jnp.max(st_first, 0, keepdims=True)

    def pair(p, _):
        step(2 * p, 0)
        step(2 * p + 1, 1)
        return 0

    lax.fori_loop(0, i, pair, 0)
    step(2 * i, 0, diagonal=True, next_qs=late_q)
    step(2 * i + 1, 1, qs=late_q, diagonal=True, next_qs=None)
    for h in range(heads):
        acc = acc_ref[h]
        o_ref[:, h * LANES:(h + 1) * LANES] = (acc / acc[MLA_V:MLA_V + 1, :]).T.astype(BF16)


def _mla(mq, mk, mvt, *, batch, seq, tk, heads=3):
    T = mq.shape[0]
    tq = 2 * tk
    nq = seq // tq
    qo = pl.BlockSpec((tq, heads * LANES), lambda b, h, i: (b * nq + i, h))
    k = pl.BlockSpec((seq, heads * LANES), lambda b, h, i: (b, h))
    vt = pl.BlockSpec((seq // tk, heads * LANES, tk), lambda b, h, i: (b, h, 0))
    return pl.pallas_call(
        functools.partial(_mla_kernel, tq=tq, tk=tk, heads=heads),
        grid=(batch, MLA_HEADS // heads, nq),
        in_specs=[qo, k, vt],
        out_specs=qo,
        out_shape=jax.ShapeDtypeStruct((T, MLA_PAD_W), BF16),
        scratch_shapes=[pltpu.VMEM((heads, tk, tq), F32), pltpu.VMEM((heads, tk, tq), F32),
                        pltpu.VMEM((heads, 1, tq), F32), pltpu.VMEM((heads, 1, tq), F32),
                        pltpu.VMEM((heads, 1, tq), F32), pltpu.VMEM((heads, LANES, tq), F32)],
        compiler_params=_params("parallel", "parallel", "arbitrary"),
        name="mla",
    )(mq, mk, mvt)


def _outproj_kernel(ret_ref, swa_ref, mla_ref, x_ref, wr_ref, ws_ref, wm_ref, g_ref, b_ref,
                    rw_ref, x1_ref, x1b_ref, logits_ref, *, alpha, n_sub):
    tm = x_ref.shape[0]
    subs = [slice(s * tm // n_sub, (s + 1) * tm // n_sub) for s in range(n_sub)]

    def mix(rows):
        return (_dot(ret_ref[rows, :], wr_ref[...]) + _dot(swa_ref[rows, :], ws_ref[...])
                + _dot(mla_ref[rows, :], wm_ref[...]))

    def finish(rows, mixed):
        x1 = _layer_norm(alpha * x_ref[rows, :] + mixed, g_ref[...], b_ref[...])
        x1_ref[rows, :] = x1
        hi = x1.astype(BF16)
        x1b_ref[rows, :] = hi
        lo = (x1 - hi.astype(F32)).astype(BF16)
        hi_part = _dot(hi, rw_ref[...])
        logits_ref[rows, :] = hi_part[:, :LANES] + hi_part[:, LANES:] + _dot(lo, rw_ref[:, :LANES])

    mixed = mix(subs[0])
    for s in range(n_sub):
        mixed_next = mix(subs[s + 1]) if s + 1 < n_sub else None
        finish(subs[s], mixed)
        mixed = mixed_next


def _outproj(ret_o, swa_o, mla_o, x, wr, ws, wm, g, b, rw, *, layer, alpha, tm=1024, n_sub=8):
    T = x.shape[0]
    row = lambda w: pl.BlockSpec((tm, w), lambda i: (i, 0))
    full = lambda a: pl.BlockSpec(a.shape, lambda i: (0, 0))
    per_layer = lambda a: _layer_spec(a, layer)
    return pl.pallas_call(
        functools.partial(_outproj_kernel, alpha=alpha, n_sub=n_sub),
        grid=(T // tm,),
        in_specs=[row(RET_W), row(SWA_W), row(MLA_PAD_W), row(D_MODEL), per_layer(wr), per_layer(ws),
                  per_layer(wm), per_layer(g), per_layer(b), full(rw)],
        out_specs=[row(D_MODEL), row(D_MODEL), row(LANES)],
        out_shape=[jax.ShapeDtypeStruct((T, D_MODEL), F32), jax.ShapeDtypeStruct((T, D_MODEL), BF16),
                   jax.ShapeDtypeStruct((T, LANES), F32)],
        compiler_params=_params("parallel"),
        name="outproj",
    )(ret_o, swa_o, mla_o, x, wr, ws, wm, g, b, rw)


MOE_CHUNK = 128
MOE_SEG_ALIGN = 16


def _moe_rows(tm):
    return -(-(2 * tm + N_EXPERTS * (MOE_SEG_ALIGN - 1) + MOE_CHUNK) // LANES) * LANES


def _ring_partner(x, d, width):
    n = x.shape[0]
    ahead = pltpu.roll(x, n - d, 0)
    if width == n:
        return ahead
    row = lax.broadcasted_iota(jnp.int32, x.shape, 0)
    return jnp.where((row % width) + d < width, ahead, pltpu.roll(x, width - d, 0))


def _router_gates_t(logits_t, bias_col):
    row = lax.broadcasted_iota(jnp.int32, logits_t.shape, 0)
    scores = jax.nn.sigmoid(logits_t)
    biased = scores + bias_col
    G = EXPERTS_PER_GROUP
    member = row % G
    rank = jnp.zeros(logits_t.shape, jnp.int32)
    for d in range(1, G):
        other = _ring_partner(biased, d, G)
        other_first = (member + d) % G < member
        rank = rank + ((other > biased) | ((other == biased) & other_first)).astype(jnp.int32)
    top2 = rank < 2
    part = jnp.where(top2, biased, 0.0)
    grp_score = part
    for d in range(1, G):
        grp_score = grp_score + _ring_partner(part, d, G)
    beaten = jnp.zeros(logits_t.shape, jnp.bool_)
    grp = row // G
    for dg in range(1, N_GROUPS):
        other = _ring_partner(grp_score, dg * G, N_EXPERTS)
        other_first = (grp + dg) % N_GROUPS < grp
        beaten = beaten | (other > grp_score) | ((other == grp_score) & other_first)
    picked = jnp.where(top2 & jnp.logical_not(beaten), scores, 0.0)
    return picked / jnp.sum(picked, 0, keepdims=True)


def _moe_kernel(xb_ref, x_ref, logits_ref, rb_ref, earlier_ref, wgu_hbm, wd_hbm, g_ref, b_ref, o_ref,
                xs_ref, ys_ref, wgu_ref, wd_ref, gu_stage, d_stage, sem, start_ref, chunks_ref,
                *, alpha, layer):
    tm = xb_ref.shape[0]
    R = xs_ref.shape[0]
    big = float(4 * R)

    @pl.when(pl.program_id(0) == 0)
    def _():
        def expert_copies(e, slot):
            return (pltpu.make_async_copy(wgu_hbm.at[layer, e], gu_stage.at[slot], sem.at[0, slot]),
                    pltpu.make_async_copy(wd_hbm.at[layer, e], d_stage.at[slot], sem.at[1, slot]))

        for copy in expert_copies(0, 0):
            copy.start()
        for e in range(N_EXPERTS):
            slot = e % 2
            if e + 1 < N_EXPERTS:
                for copy in expert_copies(e + 1, 1 - slot):
                    copy.start()
            for copy in expert_copies(e, slot):
                copy.wait()
            wgu_ref[e] = gu_stage[slot].astype(BF16)
            wd_ref[e] = d_stage[slot].astype(BF16)

    experts_t = _router_gates_t(logits_ref[...].T[:N_EXPERTS], rb_ref[...])
    gates_t = jnp.concatenate([experts_t, jnp.zeros((LANES - N_EXPERTS, tm), F32)], 0)
    gates = gates_t.T

    def strict_upper(n):
        return lax.broadcasted_iota(jnp.int32, (n, n), 0) < lax.broadcasted_iota(jnp.int32, (n, n), 1)

    def seg_pad(count):
        return jnp.floor((count + (MOE_SEG_ALIGN - 1.0)) * (1.0 / MOE_SEG_ALIGN)) * MOE_SEG_ALIGN

    one = lambda mask: jnp.where(mask, 1.0, 0.0).astype(BF16)

    sel = (gates > 0.0) & (lax.broadcasted_iota(jnp.int32, gates.shape, 1) < N_EXPERTS)
    earlier = earlier_ref[...]
    rank = _dot(earlier, one(sel))
    count = jnp.sum(jnp.where(sel, 1.0, 0.0), 0, keepdims=True)
    pad = jnp.broadcast_to(seg_pad(count), (SUBLANES, LANES)).astype(BF16)
    offset = _dot(pad, one(strict_upper(LANES)))[0:1]
    slot = offset + rank
    slot_a = jnp.min(jnp.where(sel, slot, big), -1, keepdims=True)
    slot_b = jnp.max(jnp.where(sel, slot, -1.0), -1, keepdims=True)
    gate_a = jnp.sum(jnp.where(sel & (slot == slot_a), gates, 0.0), -1, keepdims=True)
    gate_b = jnp.sum(jnp.where(sel & (slot == slot_b), gates, 0.0), -1, keepdims=True)

    sel_t = experts_t > 0.0
    rank_t = _dot_nt(one(sel_t), earlier)
    on_diag = (lax.broadcasted_iota(jnp.int32, (N_EXPERTS, LANES), 0)
               == lax.broadcasted_iota(jnp.int32, (N_EXPERTS, LANES), 1))
    offset_col = jnp.sum(jnp.where(on_diag, offset, 0.0), -1, keepdims=True)
    slot_t = offset_col + rank_t
    slot_a_t = jnp.min(jnp.where(sel_t, slot_t, big), 0, keepdims=True)
    slot_b_t = jnp.max(jnp.where(sel_t, slot_t, -1.0), 0, keepdims=True)

    row_id = lax.broadcasted_iota(jnp.int32, (R, tm), 0)
    sort = one((row_id == slot_a_t.astype(jnp.int32)) | (row_id == slot_b_t.astype(jnp.int32)))
    xs_ref[...] = _dot(sort, xb_ref[...]).astype(BF16)

    @pl.when(pl.program_id(0) == 0)
    def _():
        ys_ref[...] = jnp.zeros_like(ys_ref)

    offset_i = offset.astype(jnp.int32)
    chunks_i = jnp.floor((count + (MOE_CHUNK - 1.0)) * (1.0 / MOE_CHUNK)).astype(jnp.int32)
    for e in range(N_EXPERTS):
        start_ref[e] = offset_i[0, e]
        chunks_ref[e] = chunks_i[0, e]

    def chunk_rows(e, c):
        return pl.ds(pl.multiple_of(start_ref[e] + c * MOE_CHUNK, MOE_SEG_ALIGN), MOE_CHUNK)

    def activation(e, rows):
        h = _dot(xs_ref[rows, :], wgu_ref[e])
        up, lin = h[:, :D_EXPERT], h[:, D_EXPERT:]
        return (up * jax.nn.sigmoid(up) * lin).astype(BF16)

    def project(e, rows, a):
        ys_ref[rows, :] = _dot(a, wd_ref[e]).astype(BF16)

    def later_chunks(e, _):
        def body(c, _):
            rows = chunk_rows(e, c)
            project(e, rows, activation(e, rows))
            return 0
        return lax.fori_loop(1, chunks_ref[e], body, 0)

    lax.fori_loop(0, N_EXPERTS, later_chunks, 0)

    a = activation(0, chunk_rows(0, 0))
    for e in range(N_EXPERTS):
        a_next = activation(e + 1, chunk_rows(e + 1, 0)) if e + 1 < N_EXPERTS else None
        project(e, chunk_rows(e, 0), a)
        a = a_next

    col_id = lax.broadcasted_iota(jnp.int32, (tm, R), 1)
    unsort = jnp.where(col_id == slot_a.astype(jnp.int32), gate_a,
                       jnp.where(col_id == slot_b.astype(jnp.int32), gate_b, 0.0)).astype(BF16)
    o_ref[...] = _layer_norm(alpha * x_ref[...] + _dot(unsort, ys_ref[...]), g_ref[...], b_ref[...])


def _moe(x1b, x1, logits, rb, wgu, wd, g, b, *, layer, alpha, tm=512):
    T = x1.shape[0]
    R = _moe_rows(tm)
    earlier = jnp.asarray(np.tril(np.ones((tm, tm), np.float32), -1), BF16)
    row = lambda w: pl.BlockSpec((tm, w), lambda i: (i, 0))
    full = lambda a: _layer_spec(a, layer)
    assert wgu.dtype == F32 and wd.dtype == F32, "the staging buffers are f32"
    in_hbm = pl.BlockSpec(memory_space=pl.ANY)
    return pl.pallas_call(
        functools.partial(_moe_kernel, alpha=alpha, layer=layer),
        grid=(T // tm,),
        in_specs=[row(D_MODEL), row(D_MODEL), row(LANES), pl.BlockSpec(rb.shape, lambda i: (0, 0)),
                  pl.BlockSpec(earlier.shape, lambda i: (0, 0)), in_hbm, in_hbm, full(g), full(b)],
        out_specs=row(D_MODEL),
        out_shape=jax.ShapeDtypeStruct((T, D_MODEL), F32),
        scratch_shapes=[pltpu.VMEM((R, D_MODEL), BF16), pltpu.VMEM((R, D_MODEL), BF16),
                        pltpu.VMEM(wgu.shape[1:], BF16), pltpu.VMEM(wd.shape[1:], BF16),
                        pltpu.VMEM((2,) + wgu.shape[2:], F32), pltpu.VMEM((2,) + wd.shape[2:], F32),
                        pltpu.SemaphoreType.DMA((2, 2)),
                        pltpu.SMEM((N_EXPERTS,), jnp.int32), pltpu.SMEM((N_EXPERTS,), jnp.int32)],
        compiler_params=_params("arbitrary"),
        name="moe",
    )(x1b, x1, logits, rb, earlier, wgu, wd, g, b)


def _rope_tables(seq):
    pos = np.arange(seq, dtype=np.float64)[:, None]
    lane = np.arange(LANES)

    def table(dim, active, offset):
        j = (lane - offset) % dim
        inv = ROPE_THETA ** (-(2.0 * (j % (dim // 2))) / dim)
        ang = pos * inv[None, :]
        sign = np.where(j < dim // 2, -1.0, 1.0)
        cos = np.where(active[None, :], np.cos(ang), 1.0)
        sin = np.where(active[None, :], np.sin(ang) * sign[None, :], 0.0)
        return jnp.asarray(cos, F32), jnp.asarray(sin, F32)

    cr, sr = table(RET_DK, np.ones(LANES, bool), 0)
    cm, sm = table(MLA_ROPE, (lane >= MLA_NOPE) & (lane < MLA_NOPE + MLA_ROPE), MLA_NOPE)
    return cr, sr, cm, sm


def _stacked_weights(w_in, mla_w_uq, mla_w_ukv, w_out):
    depth = w_in.shape[0]
    sizes = (256, 256, 256, 256, SWA_W, SWA_KV_W, SWA_KV_W, MLA_Q_RANK, MLA_KV_RANK, MLA_ROPE)
    o = [int(v) for v in np.concatenate([[0], np.cumsum(sizes)])]
    order = np.array(_SWA_HEAD_ORDER)
    pad_lanes = lambda a: jnp.pad(a, [(0, 0)] * (a.ndim - 1) + [(0, LANES - a.shape[-1])])

    sq = w_in[:, :, o[4]:o[5]].reshape(depth, D_MODEL, SWA_Q_HEADS, HEAD_DIM)[:, :, order]
    kr = jnp.pad(w_in[:, :, o[9]:o[10]], ((0, 0), (0, 0), (MLA_NOPE, LANES - MLA_NOPE - MLA_ROPE)))
    w1 = jnp.concatenate([w_in[:, :, :o[4]], sq.reshape(depth, D_MODEL, SWA_W), kr, w_in[:, :, o[5]:o[7]],
                          w_in[:, :, o[8]:o[9]], w_in[:, :, o[7]:o[8]]], 2).astype(BF16)

    dq = MLA_NOPE + MLA_ROPE
    wuq = pad_lanes(mla_w_uq.reshape(depth, MLA_Q_RANK, MLA_HEADS, dq)).reshape(depth, MLA_Q_RANK, MLA_PAD_W)
    ukv = mla_w_ukv.reshape(depth, MLA_KV_RANK, MLA_HEADS, MLA_NOPE + MLA_V)
    wuk = pad_lanes(ukv[..., :MLA_NOPE]).reshape(depth, MLA_KV_RANK, MLA_PAD_W)
    wuvt = jnp.swapaxes(pad_lanes(ukv[..., MLA_NOPE:]).reshape(depth, MLA_KV_RANK, MLA_PAD_W), 1, 2)

    wr = w_out[:, :RET_W]
    ws = w_out[:, RET_W:RET_W + SWA_W].reshape(depth, SWA_Q_HEADS, HEAD_DIM, D_MODEL)[:, order]
    wm = w_out[:, RET_W + SWA_W:].reshape(depth, MLA_HEADS, MLA_V, D_MODEL)
    wm = jnp.pad(wm, ((0, 0), (0, 0), (0, LANES - MLA_V), (0, 0)))
    to_bf16 = lambda a: a.astype(BF16)
    return (w1, to_bf16(wuq), to_bf16(wuk), to_bf16(wuvt), to_bf16(wr),
            to_bf16(ws.reshape(depth, SWA_W, D_MODEL)), to_bf16(wm.reshape(depth, MLA_PAD_W, D_MODEL)))


def kernel(x, w_in, ret_gn_w, swa_sinks, mla_q_norm_w, mla_kv_norm_w, mla_w_uq, mla_w_ukv, w_out,
           ln1_g, ln1_b, router_w, router_bias, exp_w_gate_up, exp_w_down, ln2_g, ln2_b):
    batch, seq, d_model = x.shape
    depth = w_in.shape[0]
    assert d_model == D_MODEL and seq % (4 * MLA_TILE) == 0, (x.shape, MLA_TILE)
    assert w_in.shape == (depth, D_MODEL, _C_END - LANES + MLA_ROPE), w_in.shape
    alpha = (2 * depth) ** 0.25
    cr, sr, cm, sm = _rope_tables(seq)

    rw = jnp.pad(router_w, ((0, 0), (0, LANES - N_EXPERTS)))
    rwh = rw.astype(BF16)
    rw = jnp.concatenate([rwh, (rw - rwh.astype(F32)).astype(BF16)], 1)
    rb = router_bias.astype(F32).reshape(N_EXPERTS, 1)

    w1, wuq, wuk, wuvt, wr, ws, wm = _stacked_weights(w_in, mla_w_uq, mla_w_ukv, w_out)
    rows = lambda a: a.astype(F32).reshape(depth, 1, -1)
    qnw, kvnw, gnw = rows(mla_q_norm_w), rows(mla_kv_norm_w), rows(ret_gn_w)
    g1, b1, g2, b2 = rows(ln1_g), rows(ln1_b), rows(ln2_g), rows(ln2_b)
    sink_row = rows(jnp.repeat(swa_sinks, SWA_WINDOW, axis=1))

    t = x.reshape(batch * seq, D_MODEL)
    for l in range(depth):
        rq, rk, rv, rg, sq, sk, mq, mk, svt, mvt = _inproj(
            t, w1, cr, sr, cm, sm, qnw, kvnw, wuq, wuk, wuvt, layer=l, seq=seq, tile=MLA_TILE)
        ret_o = _retention(rq, rk, rv, rg, gnw, layer=l, batch=batch, seq=seq)
        swa_o = _swa(sq, sk, svt, sink_row, layer=l, seq=seq, blk=4 * MLA_TILE)
        mla_o = _mla(mq, mk, mvt, batch=batch, seq=seq, tk=MLA_TILE)
        x1, x1b, logits = _outproj(ret_o, swa_o, mla_o, t, wr, ws, wm, g1, b1, rw, layer=l, alpha=alpha)
        t = _moe(x1b, x1, logits, rb, exp_w_gate_up, exp_w_down, g2, b2, layer=l, alpha=alpha)
    return t.reshape(batch, seq, D_MODEL)
```

```python
import functools
import math

import numpy as np
import jax
import jax.numpy as jnp
from jax import lax
from jax.experimental import pallas as pl
from jax.experimental.pallas import tpu as pltpu

D_MODEL = 1024
HEAD_DIM = 64
ROPE_THETA = 10000.0

RET_HEADS = 4
RET_DK = 64
RET_DV = 64
RET_W = RET_HEADS * RET_DV

SWA_Q_HEADS = 6
SWA_KV_HEADS = 2
SWA_WINDOW = 128
SWA_W = SWA_Q_HEADS * HEAD_DIM
SWA_KV_W = SWA_KV_HEADS * HEAD_DIM

MLA_HEADS = 6
MLA_Q_RANK = 384
MLA_KV_RANK = 256
MLA_NOPE = 64
MLA_ROPE = 32
MLA_V = 64

N_EXPERTS = 16
N_GROUPS = 4
EXPERTS_PER_GROUP = N_EXPERTS // N_GROUPS
D_EXPERT = 256

LN_EPS = 1e-5
RMS_EPS = 1e-6
NEG_INF = -1e30

LANES = 128
SUBLANES = 8
MLA_PAD_W = MLA_HEADS * LANES
MLA_TILE = 512
VMEM_LIMIT = 56 * 1024 * 1024

F32 = jnp.float32
BF16 = jnp.bfloat16

_C_RQ, _C_RK, _C_RV, _C_RG = 0, 256, 512, 768
_C_SQ, _C_KR, _C_SK, _C_SV = 1024, 1408, 1536, 1664
_C_CKV, _C_CQ, _C_END = 1792, 2048, 2432

_SWA_HEAD_ORDER = (0, 3, 1, 4, 2, 5)


def _params(*sem, flags=None):
    return pltpu.CompilerParams(dimension_semantics=sem, vmem_limit_bytes=VMEM_LIMIT, flags=flags)


def _layer_spec(a, layer, **kwargs):
    return pl.BlockSpec((None,) + a.shape[1:], lambda *_: (layer,) + (0,) * (a.ndim - 1), **kwargs)


def _dot(a, b):
    return jnp.dot(a, b, preferred_element_type=F32)


def _dot_nt(a, b):
    return lax.dot_general(a, b, (((1,), (1,)), ((), ())), preferred_element_type=F32)


def _dot_tn(a, b):
    return lax.dot_general(a, b, (((0,), (0,)), ((), ())), preferred_element_type=F32)


def _split_dot(x, w_bf16):
    hi = x.astype(BF16)
    lo = (x - hi.astype(F32)).astype(BF16)
    return _dot(hi, w_bf16) + _dot(lo, w_bf16)


def _rope(h, cos, sin_signed, half):
    fwd = pltpu.roll(h, LANES - half, 1)
    bwd = pltpu.roll(h, half, 1)
    lane = lax.broadcasted_iota(jnp.int32, h.shape, 1)
    rot = jnp.where((lane % (2 * half)) < half, fwd, bwd)
    return h * cos + rot * sin_signed


def _layer_norm(y, g, b):
    mu = jnp.mean(y, -1, keepdims=True)
    d = y - mu
    var = jnp.mean(d * d, -1, keepdims=True)
    return d * lax.rsqrt(var + LN_EPS) * g + b


def _rms_norm(x, g):
    return x * lax.rsqrt(jnp.mean(x * x, -1, keepdims=True) + RMS_EPS) * g


def _inproj_kernel(x_ref, w1_ref, cr_ref, sr_ref, cm_ref, sm_ref, qnw_ref, kvnw_ref,
                   wuq_ref, wuk_ref, wuvt_ref,
                   rq_ref, rk_ref, rv_ref, rg_ref, sq_ref, sk_ref, mq_ref, mk_ref, svt_ref, mvt_ref,
                   *, mla_q_scale):
    tile = mvt_ref.shape[2]
    for s in range(x_ref.shape[0] // tile):
        _inproj_tile(slice(s * tile, (s + 1) * tile), s, x_ref, w1_ref, cr_ref, sr_ref, cm_ref, sm_ref,
                     qnw_ref, kvnw_ref, wuq_ref, wuk_ref, wuvt_ref, rq_ref, rk_ref, rv_ref, rg_ref,
                     sq_ref, sk_ref, mq_ref, mk_ref, svt_ref, mvt_ref, mla_q_scale)


def _inproj_tile(rows, s, x_ref, w1_ref, cr_ref, sr_ref, cm_ref, sm_ref, qnw_ref, kvnw_ref,
                 wuq_ref, wuk_ref, wuvt_ref, rq_ref, rk_ref, rv_ref, rg_ref, sq_ref, sk_ref, mq_ref,
                 mk_ref, svt_ref, mvt_ref, mla_q_scale):
    xb = x_ref[rows, :].astype(BF16)

    def proj(a, b):
        return _dot(xb, w1_ref[:, a:b])

    cq = proj(_C_CQ, _C_END)
    ckv = proj(_C_CKV, _C_CQ)
    rq, rk = proj(_C_RQ, _C_RK), proj(_C_RK, _C_RV)
    rv, rg = proj(_C_RV, _C_RG), proj(_C_RG, _C_SQ)
    sq_kr = proj(_C_SQ, _C_SK)
    sk_sv = proj(_C_SK, _C_CKV)
    cq = _rms_norm(cq, qnw_ref[...]).astype(BF16)
    ckv = _rms_norm(ckv, kvnw_ref[...]).astype(BF16)
    q = _dot(cq, wuq_ref[...])
    kn = _dot(ckv, wuk_ref[...])
    vt = _dot_nt(wuvt_ref[...], ckv)

    cr, sr = cr_ref[rows, :], sr_ref[rows, :]
    for g in range(RET_W // LANES):
        lanes = slice(g * LANES, (g + 1) * LANES)
        rq_ref[rows, lanes] = _rope(rq[:, lanes], cr, sr, RET_DK // 2).astype(BF16)
        rk_ref[rows, lanes] = (_rope(rk[:, lanes], cr, sr, RET_DK // 2) * (RET_DK ** -0.5)).astype(BF16)
    rv_ref[rows, :] = rv.astype(BF16)
    rg_ref[rows, :] = rg.astype(BF16)
    sq_ref[rows, :] = (sq_kr[:, :SWA_W] * (HEAD_DIM ** -0.5 * math.log2(math.e))).astype(BF16)
    sk_ref[rows, :] = sk_sv[:, :SWA_KV_W].astype(BF16)
    svt_ref[s] = sk_sv[:, SWA_KV_W:].T.astype(BF16)

    cm, sm = cm_ref[rows, :], sm_ref[rows, :]
    kpe = _rope(sq_kr[:, SWA_W:], cm, sm, MLA_ROPE // 2)
    for h in range(MLA_HEADS):
        lo, hi = h * LANES, (h + 1) * LANES
        mq_ref[rows, lo:hi] = (_rope(q[:, lo:hi], cm, sm, MLA_ROPE // 2) * mla_q_scale).astype(BF16)
        mk_ref[rows, lo:hi] = (kn[:, lo:hi] + kpe).astype(BF16)
    feat = lax.broadcasted_iota(jnp.int32, vt.shape, 0) % LANES
    mvt_ref[s] = (vt + (feat == MLA_V).astype(F32)).astype(BF16)


def _inproj(x, w1, cr, sr, cm, sm, qnw, kvnw, wuq, wuk, wuvt, *, layer, seq, tile, tiles_per_step=2):
    T = x.shape[0]
    tm = tile * tiles_per_step
    nt = T // tm
    npos = seq // tm
    row = lambda w: pl.BlockSpec((tm, w), lambda i: (i, 0))
    pos = lambda: pl.BlockSpec((tm, LANES), lambda i: (i % npos, 0))
    full = lambda a: _layer_spec(a, layer)
    widths = (RET_W, RET_W, RET_W, RET_W, SWA_W, SWA_KV_W, MLA_PAD_W, MLA_PAD_W)
    widths_t = (SWA_KV_W, MLA_PAD_W)
    mla_q_scale = (MLA_NOPE + MLA_ROPE) ** -0.5 * math.log2(math.e)
    return pl.pallas_call(
        functools.partial(_inproj_kernel, mla_q_scale=mla_q_scale),
        grid=(nt,),
        in_specs=[row(D_MODEL), full(w1), pos(), pos(), pos(), pos(), full(qnw), full(kvnw),
                  full(wuq), full(wuk), full(wuvt)],
        out_specs=[row(w) for w in widths]
        + [pl.BlockSpec((tiles_per_step, w, tile), lambda i: (i, 0, 0)) for w in widths_t],
        out_shape=[jax.ShapeDtypeStruct((T, w), BF16) for w in widths]
        + [jax.ShapeDtypeStruct((T // tile, w, tile), BF16) for w in widths_t],
        compiler_params=_params("parallel"),
        name="inproj",
    )(x, w1, cr, sr, cm, sm, qnw, kvnw, wuq, wuk, wuvt)


def _retention_kernel(q_ref, k_ref, v_ref, g_ref, gnw_ref, dec_ref, xi_ref, zeta_ref, dm_ref, bm_ref,
                      avg_ref, o_ref, state_ref, *, chunk, n_chunks):
    @pl.when(pl.program_id(1) == 0)
    def _():
        state_ref[...] = jnp.zeros_like(state_ref)

    lane_head = lax.broadcasted_iota(jnp.int32, (chunk, RET_W), 1) // RET_DV
    avg = avg_ref[...]
    chunks = [slice(c * chunk, (c + 1) * chunk) for c in range(n_chunks)]

    def raw_scores(rows):
        q = q_ref[rows, :]
        zero = jnp.zeros_like(q)
        q_heads = jnp.concatenate([jnp.where(lane_head == h, q, zero) for h in range(RET_HEADS)], 0)
        return _dot_nt(q_heads, k_ref[rows, :])

    scores = [raw_scores(rows) for rows in chunks]
    updates = [_dot_tn((k_ref[rows, :].astype(F32) * zeta_ref[...]).astype(BF16), v_ref[rows, :])
               for rows in chunks]
    inner = [_dot((s * dec_ref[...]).astype(BF16), v_ref[rows, :]) for s, rows in zip(scores, chunks)]

    state = state_ref[...]
    outs = []
    for rows, upd, inner_heads in zip(chunks, updates, inner):
        o = _dot((q_ref[rows, :].astype(F32) * xi_ref[...]).astype(BF16), state.astype(BF16))
        for h in range(RET_HEADS):
            o = o + jnp.where(lane_head == h, inner_heads[h * chunk:(h + 1) * chunk], 0.0)
        outs.append(o)
        state = dm_ref[...] * state + bm_ref[...] * upd
    state_ref[...] = state

    mus = [_split_dot(o, avg) for o in outs]
    devs = [o - mu for o, mu in zip(outs, mus)]
    variances = [_split_dot(d * d, avg) for d in devs]
    for rows, d, var in zip(chunks, devs, variances):
        on = d * lax.rsqrt(var + LN_EPS) * gnw_ref[...]
        gate = g_ref[rows, :].astype(F32)
        o_ref[rows, :] = (on * (gate * jax.nn.sigmoid(gate))).astype(BF16)


def _retention_tables(chunk):
    H = RET_HEADS
    gamma = 1.0 - 2.0 ** (-5.0 - np.arange(H, dtype=np.float64))
    log_g = np.log(gamma)
    idx = np.arange(chunk, dtype=np.float64)
    rel = idx[:, None] - idx[None, :]
    dec = np.where(rel[None] >= 0, np.exp(np.maximum(rel, 0.0)[None] * log_g[:, None, None]), 0.0)
    dec = dec.reshape(H * chunk, chunk)
    lane_head = np.arange(RET_W) // RET_DV
    xi = np.exp((idx[:, None] + 1.0) * log_g[lane_head][None, :])
    zeta = np.exp((chunk - 1.0 - idx[:, None]) * log_g[lane_head][None, :])
    same = lane_head[:, None] == lane_head[None, :]
    dm = np.where(same, np.exp(chunk * log_g)[lane_head][:, None], 0.0)
    bm = same.astype(np.float64)
    avg = bm / RET_DV
    f = lambda a: jnp.asarray(a, F32)
    return f(dec), f(xi), f(zeta), f(dm), f(bm), jnp.asarray(avg, BF16)


def _retention(rq, rk, rv, rg, gnw, *, layer, batch, seq, blk=1024, chunk=128):
    T = rq.shape[0]
    nb = seq // blk
    tables = _retention_tables(chunk)
    row = pl.BlockSpec((blk, RET_W), lambda b, i: (b * nb + i, 0))
    full = lambda a: pl.BlockSpec(a.shape, lambda b, i: (0, 0))
    return pl.pallas_call(
        functools.partial(_retention_kernel, chunk=chunk, n_chunks=blk // chunk),
        grid=(batch, nb),
        in_specs=[row, row, row, row, _layer_spec(gnw, layer)] + [full(t) for t in tables],
        out_specs=row,
        out_shape=jax.ShapeDtypeStruct((T, RET_W), BF16),
        scratch_shapes=[pltpu.VMEM((RET_W, RET_W), F32)],
        compiler_params=_params("parallel", "arbitrary"),
        name="retention",
    )(rq, rk, rv, rg, gnw, *tables)


def _swa_kernel(q_ref, kc_ref, kp_ref, vtc_ref, vtp_ref, sink_ref, o_ref, *, blk, seq):
    L = SWA_WINDOW
    n_pairs = SWA_Q_HEADS // 2
    n_sub = blk // L
    lane_lo = lax.broadcasted_iota(jnp.int32, (L, LANES), 1) < HEAD_DIM
    feat_lo = lax.broadcasted_iota(jnp.int32, (LANES, L), 0) < HEAD_DIM
    key = lax.broadcasted_iota(jnp.int32, (2 * L, SWA_Q_HEADS * L), 0)
    qry = lax.broadcasted_iota(jnp.int32, (2 * L, SWA_Q_HEADS * L), 1) % L
    band = (key > qry) & (key <= qry + L)
    sink = sink_ref[...] * math.log2(math.e)

    def scores(sb):
        r0 = sb * L
        k_prev = kp_ref[...] if sb == 0 else kc_ref[r0 - L:r0, :]
        k_win = jnp.concatenate([k_prev, kc_ref[r0:r0 + L, :]], 0)
        pairs = [q_ref[r0:r0 + L, g * LANES:(g + 1) * LANES] for g in range(n_pairs)]
        zero = jnp.zeros_like(pairs[0])
        q_heads = jnp.concatenate([jnp.where(lane_lo, p, zero) for p in pairs]
                                  + [jnp.where(lane_lo, zero, p) for p in pairs], 0)
        return _dot_nt(k_win, q_heads)

    def attend(sb, st):
        r0 = sb * L
        valid = band
        if sb == 0:
            pos0 = (pl.program_id(0) * blk) % seq
            valid = band & (key >= jnp.where(pos0 > 0, 0, L))
        st = jnp.where(valid, st, NEG_INF)
        m = jnp.maximum(jnp.max(st, 0, keepdims=True), sink)
        p = jnp.exp2(st - m)
        denom = jnp.sum(p, 0, keepdims=True) + jnp.exp2(sink - m)
        c, off = divmod(r0, vtc_ref.shape[2])
        if off == 0:
            vt_prev = vtp_ref[0] if c == 0 else vtc_ref[c - 1, :, vtc_ref.shape[2] - L:]
            vt_win = jnp.concatenate([vt_prev, vtc_ref[c, :, :L]], 1)
        else:
            vt_win = vtc_ref[c, :, off - L:off + L]
        ot = _dot(vt_win, p.astype(BF16)) * (1.0 / denom)
        for g in range(n_pairs):
            o_lo = ot[:, g * L:(g + 1) * L]
            o_hi = ot[:, (g + n_pairs) * L:(g + n_pairs + 1) * L]
            o_ref[r0:r0 + L, g * LANES:(g + 1) * LANES] = jnp.where(feat_lo, o_lo, o_hi).T.astype(BF16)

    st = scores(0)
    for sb in range(n_sub):
        st_next = scores(sb + 1) if sb + 1 < n_sub else None
        attend(sb, st)
        st = st_next


def _swa(sq, sk, svt, sink_row, *, layer, seq, blk):
    T = sq.shape[0]
    L = SWA_WINDOW
    per = blk // L
    tile = svt.shape[2]
    tiles = blk // tile
    cur = lambda w: pl.BlockSpec((blk, w), lambda i: (i, 0))
    k_prev = pl.BlockSpec((L, SWA_KV_W), lambda i: (jnp.maximum(i * per - 1, 0), 0))
    vt_cur = pl.BlockSpec((tiles, SWA_KV_W, tile), lambda i: (i, 0, 0))
    vt_prev = pl.BlockSpec((1, SWA_KV_W, L), lambda i: (jnp.maximum(i * tiles - 1, 0), 0, tile // L - 1))
    return pl.pallas_call(
        functools.partial(_swa_kernel, blk=blk, seq=seq),
        grid=(T // blk,),
        in_specs=[cur(SWA_W), cur(SWA_KV_W), k_prev, vt_cur, vt_prev, _layer_spec(sink_row, layer)],
        out_specs=cur(SWA_W),
        out_shape=jax.ShapeDtypeStruct((T, SWA_W), BF16),
        compiler_params=_params("parallel"),
        name="swa",
    )(sq, sk, sk, svt, svt, sink_row)


def _mla_kernel(q_ref, k_ref, vt_ref, o_ref, st0_ref, st1_ref, cmax0_ref, cmax1_ref, m_ref, acc_ref,
                *, tq, tk, heads):
    i = pl.program_id(2)
    head_lanes = [slice(h * LANES, (h + 1) * LANES) for h in range(heads)]
    st_refs = (st0_ref, st1_ref)
    cmax_refs = (cmax0_ref, cmax1_ref)
    all_q = slice(0, tq)
    late_q = slice(tk, tq)

    def scores(j, lanes, qs):
        start = pl.multiple_of(j * tk, tk)
        return _dot_nt(k_ref[pl.ds(start, tk), lanes], q_ref[qs, lanes])

    def step(j, cur, qs=all_q, diagonal=False, next_qs=all_q):
        for h, lanes in enumerate(head_lanes):
            if next_qs is not None:
                st_next = scores(j + 1, lanes, next_qs)
                st_refs[1 - cur][h, :, next_qs] = st_next
                if next_qs is all_q:
                    cmax_refs[1 - cur][h] = jnp.max(st_next, 0, keepdims=True)
            st = st_refs[cur][h, :, qs]
            if diagonal:
                key = lax.broadcasted_iota(jnp.int32, st.shape, 0)
                qry = lax.broadcasted_iota(jnp.int32, st.shape, 1)
                st = jnp.where(key <= qry, st, NEG_INF)
                col_max = jnp.max(st, 0, keepdims=True)
            else:
                col_max = cmax_refs[cur][h]
            m = m_ref[h][:, qs]
            m_new = jnp.maximum(m, col_max)
            pt = jnp.exp2(st - m_new).astype(BF16)
            acc_ref[h, :, qs] = jnp.exp2(m - m_new) * acc_ref[h, :, qs] + _dot(vt_ref[j, lanes, :], pt)
            if qs is all_q:
                m_ref[h] = m_new

    m_ref[...] = jnp.full(m_ref.shape, NEG_INF, F32)
    acc_ref[...] = jnp.zeros(acc_ref.shape, F32)
    for h, lanes in enumerate(head_lanes):
        st_first = scores(0, lanes, all_q)
        st0_ref[h] = st_first
        cmax0_ref[h] = jnp.max(st_first, 0, keepdims=True)

    def pair(p, _):
        step(2 * p, 0)
        step(2 * p + 1, 1)
        return 0

    lax.fori_loop(0, i, pair, 0)
    step(2 * i, 0, diagonal=True, next_qs=late_q)
    step(2 * i + 1, 1, qs=late_q, diagonal=True, next_qs=None)
    for h in range(heads):
        acc = acc_ref[h]
        o_ref[:, h * LANES:(h + 1) * LANES] = (acc / acc[MLA_V:MLA_V + 1, :]).T.astype(BF16)


def _mla(mq, mk, mvt, *, batch, seq, tk, heads=3):
    T = mq.shape[0]
    tq = 2 * tk
    nq = seq // tq
    qo = pl.BlockSpec((tq, heads * LANES), lambda b, h, i: (b * nq + i, h))
    k = pl.BlockSpec((seq, heads * LANES), lambda b, h, i: (b, h))
    vt = pl.BlockSpec((seq // tk, heads * LANES, tk), lambda b, h, i: (b, h, 0))
    return pl.pallas_call(
        functools.partial(_mla_kernel, tq=tq, tk=tk, heads=heads),
        grid=(batch, MLA_HEADS // heads, nq),
        in_specs=[qo, k, vt],
        out_specs=qo,
        out_shape=jax.ShapeDtypeStruct((T, MLA_PAD_W), BF16),
        scratch_shapes=[pltpu.VMEM((heads, tk, tq), F32), pltpu.VMEM((heads, tk, tq), F32),
                        pltpu.VMEM((heads, 1, tq), F32), pltpu.VMEM((heads, 1, tq), F32),
                        pltpu.VMEM((heads, 1, tq), F32), pltpu.VMEM((heads, LANES, tq), F32)],
        compiler_params=_params("parallel", "parallel", "arbitrary"),
        name="mla",
    )(mq, mk, mvt)


def _outproj_kernel(ret_ref, swa_ref, mla_ref, x_ref, wr_ref, ws_ref, wm_ref, g_ref, b_ref,
                    rw_ref, x1_ref, x1b_ref, logits_ref, *, alpha, n_sub):
    tm = x_ref.shape[0]
    subs = [slice(s * tm // n_sub, (s + 1) * tm // n_sub) for s in range(n_sub)]

    def mix(rows):
        return (_dot(ret_ref[rows, :], wr_ref[...]) + _dot(swa_ref[rows, :], ws_ref[...])
                + _dot(mla_ref[rows, :], wm_ref[...]))

    def finish(rows, mixed):
        x1 = _layer_norm(alpha * x_ref[rows, :] + mixed, g_ref[...], b_ref[...])
        x1_ref[rows, :] = x1
        hi = x1.astype(BF16)
        x1b_ref[rows, :] = hi
        lo = (x1 - hi.astype(F32)).astype(BF16)
        hi_part = _dot(hi, rw_ref[...])
        logits_ref[rows, :] = hi_part[:, :LANES] + hi_part[:, LANES:] + _dot(lo, rw_ref[:, :LANES])

    mixed = mix(subs[0])
    for s in range(n_sub):
        mixed_next = mix(subs[s + 1]) if s + 1 < n_sub else None
        finish(subs[s], mixed)
        mixed = mixed_next


def _outproj(ret_o, swa_o, mla_o, x, wr, ws, wm, g, b, rw, *, layer, alpha, tm=1024, n_sub=8):
    T = x.shape[0]
    row = lambda w: pl.BlockSpec((tm, w), lambda i: (i, 0))
    full = lambda a: pl.BlockSpec(a.shape, lambda i: (0, 0))
    per_layer = lambda a: _layer_spec(a, layer)
    return pl.pallas_call(
        functools.partial(_outproj_kernel, alpha=alpha, n_sub=n_sub),
        grid=(T // tm,),
        in_specs=[row(RET_W), row(SWA_W), row(MLA_PAD_W), row(D_MODEL), per_layer(wr), per_layer(ws),
                  per_layer(wm), per_layer(g), per_layer(b), full(rw)],
        out_specs=[row(D_MODEL), row(D_MODEL), row(LANES)],
        out_shape=[jax.ShapeDtypeStruct((T, D_MODEL), F32), jax.ShapeDtypeStruct((T, D_MODEL), BF16),
                   jax.ShapeDtypeStruct((T, LANES), F32)],
        compiler_params=_params("parallel"),
        name="outproj",
    )(ret_o, swa_o, mla_o, x, wr, ws, wm, g, b, rw)


MOE_CHUNK = 128
MOE_SEG_ALIGN = 16


def _moe_rows(tm):
    return -(-(2 * tm + N_EXPERTS * (MOE_SEG_ALIGN - 1) + MOE_CHUNK) // LANES) * LANES


def _ring_partner(x, d, width):
    n = x.shape[0]
    ahead = pltpu.roll(x, n - d, 0)
    if width == n:
        return ahead
    row = lax.broadcasted_iota(jnp.int32, x.shape, 0)
    return jnp.where((row % width) + d < width, ahead, pltpu.roll(x, width - d, 0))


def _router_gates_t(logits_t, bias_col):
    row = lax.broadcasted_iota(jnp.int32, logits_t.shape, 0)
    scores = jax.nn.sigmoid(logits_t)
    biased = scores + bias_col
    G = EXPERTS_PER_GROUP
    member = row % G
    rank = jnp.zeros(logits_t.shape, jnp.int32)
    for d in range(1, G):
        other = _ring_partner(biased, d, G)
        other_first = (member + d) % G < member
        rank = rank + ((other > biased) | ((other == biased) & other_first)).astype(jnp.int32)
    top2 = rank < 2
    part = jnp.where(top2, biased, 0.0)
    grp_score = part
    for d in range(1, G):
        grp_score = grp_score + _ring_partner(part, d, G)
    beaten = jnp.zeros(logits_t.shape, jnp.bool_)
    grp = row // G
    for dg in range(1, N_GROUPS):
        other = _ring_partner(grp_score, dg * G, N_EXPERTS)
        other_first = (grp + dg) % N_GROUPS < grp
        beaten = beaten | (other > grp_score) | ((other == grp_score) & other_first)
    picked = jnp.where(top2 & jnp.logical_not(beaten), scores, 0.0)
    return picked / jnp.sum(picked, 0, keepdims=True)


def _moe_kernel(xb_ref, x_ref, logits_ref, rb_ref, earlier_ref, wgu_hbm, wd_hbm, g_ref, b_ref, o_ref,
                xs_ref, ys_ref, wgu_ref, wd_ref, gu_stage, d_stage, sem, start_ref, chunks_ref,
                *, alpha, layer):
    tm = xb_ref.shape[0]
    R = xs_ref.shape[0]
    big = float(4 * R)

    @pl.when(pl.program_id(0) == 0)
    def _():
        def expert_copies(e, slot):
            return (pltpu.make_async_copy(wgu_hbm.at[layer, e], gu_stage.at[slot], sem.at[0, slot]),
                    pltpu.make_async_copy(wd_hbm.at[layer, e], d_stage.at[slot], sem.at[1, slot]))

        for copy in expert_copies(0, 0):
            copy.start()
        for e in range(N_EXPERTS):
            slot = e % 2
            if e + 1 < N_EXPERTS:
                for copy in expert_copies(e + 1, 1 - slot):
                    copy.start()
            for copy in expert_copies(e, slot):
                copy.wait()
            wgu_ref[e] = gu_stage[slot].astype(BF16)
            wd_ref[e] = d_stage[slot].astype(BF16)

    experts_t = _router_gates_t(logits_ref[...].T[:N_EXPERTS], rb_ref[...])
    gates_t = jnp.concatenate([experts_t, jnp.zeros((LANES - N_EXPERTS, tm), F32)], 0)
    gates = gates_t.T

    def strict_upper(n):
        return lax.broadcasted_iota(jnp.int32, (n, n), 0) < lax.broadcasted_iota(jnp.int32, (n, n), 1)

    def seg_pad(count):
        return jnp.floor((count + (MOE_SEG_ALIGN - 1.0)) * (1.0 / MOE_SEG_ALIGN)) * MOE_SEG_ALIGN

    one = lambda mask: jnp.where(mask, 1.0, 0.0).astype(BF16)

    sel = (gates > 0.0) & (lax.broadcasted_iota(jnp.int32, gates.shape, 1) < N_EXPERTS)
    earlier = earlier_ref[...]
    rank = _dot(earlier, one(sel))
    count = jnp.sum(jnp.where(sel, 1.0, 0.0), 0, keepdims=True)
    pad = jnp.broadcast_to(seg_pad(count), (SUBLANES, LANES)).astype(BF16)
    offset = _dot(pad, one(strict_upper(LANES)))[0:1]
    slot = offset + rank
    slot_a = jnp.min(jnp.where(sel, slot, big), -1, keepdims=True)
    slot_b = jnp.max(jnp.where(sel, slot, -1.0), -1, keepdims=True)
    gate_a = jnp.sum(jnp.where(sel & (slot == slot_a), gates, 0.0), -1, keepdims=True)
    gate_b = jnp.sum(jnp.where(sel & (slot == slot_b), gates, 0.0), -1, keepdims=True)

    sel_t = experts_t > 0.0
    rank_t = _dot_nt(one(sel_t), earlier)
    on_diag = (lax.broadcasted_iota(jnp.int32, (N_EXPERTS, LANES), 0)
               == lax.broadcasted_iota(jnp.int32, (N_EXPERTS, LANES), 1))
    offset_col = jnp.sum(jnp.where(on_diag, offset, 0.0), -1, keepdims=True)
    slot_t = offset_col + rank_t
    slot_a_t = jnp.min(jnp.where(sel_t, slot_t, big), 0, keepdims=True)
    slot_b_t = jnp.max(jnp.where(sel_t, slot_t, -1.0), 0, keepdims=True)

    row_id = lax.broadcasted_iota(jnp.int32, (R, tm), 0)
    sort = one((row_id == slot_a_t.astype(jnp.int32)) | (row_id == slot_b_t.astype(jnp.int32)))
    xs_ref[...] = _dot(sort, xb_ref[...]).astype(BF16)

    @pl.when(pl.program_id(0) == 0)
    def _():
        ys_ref[...] = jnp.zeros_like(ys_ref)

    offset_i = offset.astype(jnp.int32)
    chunks_i = jnp.floor((count + (MOE_CHUNK - 1.0)) * (1.0 / MOE_CHUNK)).astype(jnp.int32)
    for e in range(N_EXPERTS):
        start_ref[e] = offset_i[0, e]
        chunks_ref[e] = chunks_i[0, e]

    def chunk_rows(e, c):
        return pl.ds(pl.multiple_of(start_ref[e] + c * MOE_CHUNK, MOE_SEG_ALIGN), MOE_CHUNK)

    def activation(e, rows):
        h = _dot(xs_ref[rows, :], wgu_ref[e])
        up, lin = h[:, :D_EXPERT], h[:, D_EXPERT:]
        return (up * jax.nn.sigmoid(up) * lin).astype(BF16)

    def project(e, rows, a):
        ys_ref[rows, :] = _dot(a, wd_ref[e]).astype(BF16)

    def later_chunks(e, _):
        def body(c, _):
            rows = chunk_rows(e, c)
            project(e, rows, activation(e, rows))
            return 0
        return lax.fori_loop(1, chunks_ref[e], body, 0)

    lax.fori_loop(0, N_EXPERTS, later_chunks, 0)

    a = activation(0, chunk_rows(0, 0))
    for e in range(N_EXPERTS):
        a_next = activation(e + 1, chunk_rows(e + 1, 0)) if e + 1 < N_EXPERTS else None
        project(e, chunk_rows(e, 0), a)
        a = a_next

    col_id = lax.broadcasted_iota(jnp.int32, (tm, R), 1)
    unsort = jnp.where(col_id == slot_a.astype(jnp.int32), gate_a,
                       jnp.where(col_id == slot_b.astype(jnp.int32), gate_b, 0.0)).astype(BF16)
    o_ref[...] = _layer_norm(alpha * x_ref[...] + _dot(unsort, ys_ref[...]), g_ref[...], b_ref[...])


def _moe(x1b, x1, logits, rb, wgu, wd, g, b, *, layer, alpha, tm=512):
    T = x1.shape[0]
    R = _moe_rows(tm)
    earlier = jnp.asarray(np.tril(np.ones((tm, tm), np.float32), -1), BF16)
    row = lambda w: pl.BlockSpec((tm, w), lambda i: (i, 0))
    full = lambda a: _layer_spec(a, layer)
    assert wgu.dtype == F32 and wd.dtype == F32, "the staging buffers are f32"
    in_hbm = pl.BlockSpec(memory_space=pl.ANY)
    return pl.pallas_call(
        functools.partial(_moe_kernel, alpha=alpha, layer=layer),
        grid=(T // tm,),
        in_specs=[row(D_MODEL), row(D_MODEL), row(LANES), pl.BlockSpec(rb.shape, lambda i: (0, 0)),
                  pl.BlockSpec(earlier.shape, lambda i: (0, 0)), in_hbm, in_hbm, full(g), full(b)],
        out_specs=row(D_MODEL),
        out_shape=jax.ShapeDtypeStruct((T, D_MODEL), F32),
        scratch_shapes=[pltpu.VMEM((R, D_MODEL), BF16), pltpu.VMEM((R, D_MODEL), BF16),
                        pltpu.VMEM(wgu.shape[1:], BF16), pltpu.VMEM(wd.shape[1:], BF16),
                        pltpu.VMEM((2,) + wgu.shape[2:], F32), pltpu.VMEM((2,) + wd.shape[2:], F32),
                        pltpu.SemaphoreType.DMA((2, 2)),
                        pltpu.SMEM((N_EXPERTS,), jnp.int32), pltpu.SMEM((N_EXPERTS,), jnp.int32)],
        compiler_params=_params("arbitrary"),
        name="moe",
    )(x1b, x1, logits, rb, earlier, wgu, wd, g, b)


def _rope_tables(seq):
    pos = np.arange(seq, dtype=np.float64)[:, None]
    lane = np.arange(LANES)

    def table(dim, active, offset):
        j = (lane - offset) % dim
        inv = ROPE_THETA ** (-(2.0 * (j % (dim // 2))) / dim)
        ang = pos * inv[None, :]
        sign = np.where(j < dim // 2, -1.0, 1.0)
        cos = np.where(active[None, :], np.cos(ang), 1.0)
        sin = np.where(active[None, :], np.sin(ang) * sign[None, :], 0.0)
        return jnp.asarray(cos, F32), jnp.asarray(sin, F32)

    cr, sr = table(RET_DK, np.ones(LANES, bool), 0)
    cm, sm = table(MLA_ROPE, (lane >= MLA_NOPE) & (lane < MLA_NOPE + MLA_ROPE), MLA_NOPE)
    return cr, sr, cm, sm


def _stacked_weights(w_in, mla_w_uq, mla_w_ukv, w_out):
    depth = w_in.shape[0]
    sizes = (256, 256, 256, 256, SWA_W, SWA_KV_W, SWA_KV_W, MLA_Q_RANK, MLA_KV_RANK, MLA_ROPE)
    o = [int(v) for v in np.concatenate([[0], np.cumsum(sizes)])]
    order = np.array(_SWA_HEAD_ORDER)
    pad_lanes = lambda a: jnp.pad(a, [(0, 0)] * (a.ndim - 1) + [(0, LANES - a.shape[-1])])

    sq = w_in[:, :, o[4]:o[5]].reshape(depth, D_MODEL, SWA_Q_HEADS, HEAD_DIM)[:, :, order]
    kr = jnp.pad(w_in[:, :, o[9]:o[10]], ((0, 0), (0, 0), (MLA_NOPE, LANES - MLA_NOPE - MLA_ROPE)))
    w1 = jnp.concatenate([w_in[:, :, :o[4]], sq.reshape(depth, D_MODEL, SWA_W), kr, w_in[:, :, o[5]:o[7]],
                          w_in[:, :, o[8]:o[9]], w_in[:, :, o[7]:o[8]]], 2).astype(BF16)

    dq = MLA_NOPE + MLA_ROPE
    wuq = pad_lanes(mla_w_uq.reshape(depth, MLA_Q_RANK, MLA_HEADS, dq)).reshape(depth, MLA_Q_RANK, MLA_PAD_W)
    ukv = mla_w_ukv.reshape(depth, MLA_KV_RANK, MLA_HEADS, MLA_NOPE + MLA_V)
    wuk = pad_lanes(ukv[..., :MLA_NOPE]).reshape(depth, MLA_KV_RANK, MLA_PAD_W)
    wuvt = jnp.swapaxes(pad_lanes(ukv[..., MLA_NOPE:]).reshape(depth, MLA_KV_RANK, MLA_PAD_W), 1, 2)

    wr = w_out[:, :RET_W]
    ws = w_out[:, RET_W:RET_W + SWA_W].reshape(depth, SWA_Q_HEADS, HEAD_DIM, D_MODEL)[:, order]
    wm = w_out[:, RET_W + SWA_W:].reshape(depth, MLA_HEADS, MLA_V, D_MODEL)
    wm = jnp.pad(wm, ((0, 0), (0, 0), (0, LANES - MLA_V), (0, 0)))
    to_bf16 = lambda a: a.astype(BF16)
    return (w1, to_bf16(wuq), to_bf16(wuk), to_bf16(wuvt), to_bf16(wr),
            to_bf16(ws.reshape(depth, SWA_W, D_MODEL)), to_bf16(wm.reshape(depth, MLA_PAD_W, D_MODEL)))


def kernel(x, w_in, ret_gn_w, swa_sinks, mla_q_norm_w, mla_kv_norm_w, mla_w_uq, mla_w_ukv, w_out,
           ln1_g, ln1_b, router_w, router_bias, exp_w_gate_up, exp_w_down, ln2_g, ln2_b):
    batch, seq, d_model = x.shape
    depth = w_in.shape[0]
    assert d_model == D_MODEL and seq % (8 * MLA_TILE) == 0, (x.shape, MLA_TILE)
    assert w_in.shape == (depth, D_MODEL, _C_END - LANES + MLA_ROPE), w_in.shape
    alpha = (2 * depth) ** 0.25
    cr, sr, cm, sm = _rope_tables(seq)

    rw = jnp.pad(router_w, ((0, 0), (0, LANES - N_EXPERTS)))
    rwh = rw.astype(BF16)
    rw = jnp.concatenate([rwh, (rw - rwh.astype(F32)).astype(BF16)], 1)
    rb = router_bias.astype(F32).reshape(N_EXPERTS, 1)

    w1, wuq, wuk, wuvt, wr, ws, wm = _stacked_weights(w_in, mla_w_uq, mla_w_ukv, w_out)
    rows = lambda a: a.astype(F32).reshape(depth, 1, -1)
    qnw, kvnw, gnw = rows(mla_q_norm_w), rows(mla_kv_norm_w), rows(ret_gn_w)
    g1, b1, g2, b2 = rows(ln1_g), rows(ln1_b), rows(ln2_g), rows(ln2_b)
    sink_row = rows(jnp.repeat(swa_sinks, SWA_WINDOW, axis=1))

    t = x.reshape(batch * seq, D_MODEL)
    for l in range(depth):
        rq, rk, rv, rg, sq, sk, mq, mk, svt, mvt = _inproj(
            t, w1, cr, sr, cm, sm, qnw, kvnw, wuq, wuk, wuvt, layer=l, seq=seq, tile=MLA_TILE)
        ret_o = _retention(rq, rk, rv, rg, gnw, layer=l, batch=batch, seq=seq)
        swa_o = _swa(sq, sk, svt, sink_row, layer=l, seq=seq, blk=8 * MLA_TILE)
        mla_o = _mla(mq, mk, mvt, batch=batch, seq=seq, tk=MLA_TILE)
        x1, x1b, logits = _outproj(ret_o, swa_o, mla_o, t, wr, ws, wm, g1, b1, rw, layer=l, alpha=alpha)
        t = _moe(x1b, x1, logits, rb, exp_w_gate_up, exp_w_down, g2, b2, layer=l, alpha=alpha)
    return t.reshape(batch, seq, D_MODEL)
```

```python
import functools
import math

import numpy as np
import jax
import jax.numpy as jnp
from jax import lax
from jax.experimental import pallas as pl
from jax.experimental.pallas import tpu as pltpu

D_MODEL = 1024
HEAD_DIM = 64
ROPE_THETA = 10000.0

RET_HEADS = 4
RET_DK = 64
RET_DV = 64
RET_W = RET_HEADS * RET_DV

SWA_Q_HEADS = 6
SWA_KV_HEADS = 2
SWA_WINDOW = 128
SWA_W = SWA_Q_HEADS * HEAD_DIM
SWA_KV_W = SWA_KV_HEADS * HEAD_DIM

MLA_HEADS = 6
MLA_Q_RANK = 384
MLA_KV_RANK = 256
MLA_NOPE = 64
MLA_ROPE = 32
MLA_V = 64

N_EXPERTS = 16
N_GROUPS = 4
EXPERTS_PER_GROUP = N_EXPERTS // N_GROUPS
D_EXPERT = 256

LN_EPS = 1e-5
RMS_EPS = 1e-6
NEG_INF = -1e30

LANES = 128
SUBLANES = 8
MLA_PAD_W = MLA_HEADS * LANES
MLA_TILE = 512
VMEM_LIMIT = 56 * 1024 * 1024

F32 = jnp.float32
BF16 = jnp.bfloat16

_C_RQ, _C_RK, _C_RV, _C_RG = 0, 256, 512, 768
_C_SQ, _C_KR, _C_SK, _C_SV = 1024, 1408, 1536, 1664
_C_CKV, _C_CQ, _C_END = 1792, 2048, 2432

_SWA_HEAD_ORDER = (0, 3, 1, 4, 2, 5)


def _params(*sem, flags=None):
    return pltpu.CompilerParams(dimension_semantics=sem, vmem_limit_bytes=VMEM_LIMIT, flags=flags)


def _layer_spec(a, layer, **kwargs):
    return pl.BlockSpec((None,) + a.shape[1:], lambda *_: (layer,) + (0,) * (a.ndim - 1), **kwargs)


def _dot(a, b):
    return jnp.dot(a, b, preferred_element_type=F32)


def _dot_nt(a, b):
    return lax.dot_general(a, b, (((1,), (1,)), ((), ())), preferred_element_type=F32)


def _dot_tn(a, b):
    return lax.dot_general(a, b, (((0,), (0,)), ((), ())), preferred_element_type=F32)


def _split_dot(x, w_bf16):
    hi = x.astype(BF16)
    lo = (x - hi.astype(F32)).astype(BF16)
    return _dot(hi, w_bf16) + _dot(lo, w_bf16)


def _rope(h, cos, sin_signed, half):
    fwd = pltpu.roll(h, LANES - half, 1)
    bwd = pltpu.roll(h, half, 1)
    lane = lax.broadcasted_iota(jnp.int32, h.shape, 1)
    rot = jnp.where((lane % (2 * half)) < half, fwd, bwd)
    return h * cos + rot * sin_signed


def _layer_norm(y, g, b):
    mu = jnp.mean(y, -1, keepdims=True)
    d = y - mu
    var = jnp.mean(d * d, -1, keepdims=True)
    return d * lax.rsqrt(var + LN_EPS) * g + b


def _rms_norm(x, g):
    return x * lax.rsqrt(jnp.mean(x * x, -1, keepdims=True) + RMS_EPS) * g


def _inproj_kernel(x_ref, w1_ref, cr_ref, sr_ref, cm_ref, sm_ref, qnw_ref, kvnw_ref,
                   wuq_ref, wuk_ref, wuvt_ref,
                   rq_ref, rk_ref, rv_ref, rg_ref, sq_ref, sk_ref, mq_ref, mk_ref, svt_ref, mvt_ref,
                   *, mla_q_scale):
    tile = mvt_ref.shape[2]
    for s in range(x_ref.shape[0] // tile):
        _inproj_tile(slice(s * tile, (s + 1) * tile), s, x_ref, w1_ref, cr_ref, sr_ref, cm_ref, sm_ref,
                     qnw_ref, kvnw_ref, wuq_ref, wuk_ref, wuvt_ref, rq_ref, rk_ref, rv_ref, rg_ref,
                     sq_ref, sk_ref, mq_ref, mk_ref, svt_ref, mvt_ref, mla_q_scale)


def _inproj_tile(rows, s, x_ref, w1_ref, cr_ref, sr_ref, cm_ref, sm_ref, qnw_ref, kvnw_ref,
                 wuq_ref, wuk_ref, wuvt_ref, rq_ref, rk_ref, rv_ref, rg_ref, sq_ref, sk_ref, mq_ref,
                 mk_ref, svt_ref, mvt_ref, mla_q_scale):
    xb = x_ref[rows, :].astype(BF16)

    def proj(a, b):
        return _dot(xb, w1_ref[:, a:b])

    cq = proj(_C_CQ, _C_END)
    ckv = proj(_C_CKV, _C_CQ)
    rq, rk = proj(_C_RQ, _C_RK), proj(_C_RK, _C_RV)
    rv, rg = proj(_C_RV, _C_RG), proj(_C_RG, _C_SQ)
    sq_kr = proj(_C_SQ, _C_SK)
    sk_sv = proj(_C_SK, _C_CKV)
    cq = _rms_norm(cq, qnw_ref[...]).astype(BF16)
    ckv = _rms_norm(ckv, kvnw_ref[...]).astype(BF16)
    q = _dot(cq, wuq_ref[...])
    kn = _dot(ckv, wuk_ref[...])
    vt = _dot_nt(wuvt_ref[...], ckv)

    cr, sr = cr_ref[rows, :], sr_ref[rows, :]
    for g in range(RET_W // LANES):
        lanes = slice(g * LANES, (g + 1) * LANES)
        rq_ref[rows, lanes] = _rope(rq[:, lanes], cr, sr, RET_DK // 2).astype(BF16)
        rk_ref[rows, lanes] = (_rope(rk[:, lanes], cr, sr, RET_DK // 2) * (RET_DK ** -0.5)).astype(BF16)
    rv_ref[rows, :] = rv.astype(BF16)
    rg_ref[rows, :] = rg.astype(BF16)
    sq_ref[rows, :] = (sq_kr[:, :SWA_W] * (HEAD_DIM ** -0.5 * math.log2(math.e))).astype(BF16)
    sk_ref[rows, :] = sk_sv[:, :SWA_KV_W].astype(BF16)
    svt_ref[s] = sk_sv[:, SWA_KV_W:].T.astype(BF16)

    cm, sm = cm_ref[rows, :], sm_ref[rows, :]
    kpe = _rope(sq_kr[:, SWA_W:], cm, sm, MLA_ROPE // 2)
    for h in range(MLA_HEADS):
        lo, hi = h * LANES, (h + 1) * LANES
        mq_ref[rows, lo:hi] = (_rope(q[:, lo:hi], cm, sm, MLA_ROPE // 2) * mla_q_scale).astype(BF16)
        mk_ref[rows, lo:hi] = (kn[:, lo:hi] + kpe).astype(BF16)
    feat = lax.broadcasted_iota(jnp.int32, vt.shape, 0) % LANES
    mvt_ref[s] = (vt + (feat == MLA_V).astype(F32)).astype(BF16)


def _inproj(x, w1, cr, sr, cm, sm, qnw, kvnw, wuq, wuk, wuvt, *, layer, seq, tile, tiles_per_step=2):
    T = x.shape[0]
    tm = tile * tiles_per_step
    nt = T // tm
    npos = seq // tm
    row = lambda w: pl.BlockSpec((tm, w), lambda i: (i, 0))
    pos = lambda: pl.BlockSpec((tm, LANES), lambda i: (i % npos, 0))
    full = lambda a: _layer_spec(a, layer)
    widths = (RET_W, RET_W, RET_W, RET_W, SWA_W, SWA_KV_W, MLA_PAD_W, MLA_PAD_W)
    widths_t = (SWA_KV_W, MLA_PAD_W)
    mla_q_scale = (MLA_NOPE + MLA_ROPE) ** -0.5 * math.log2(math.e)
    return pl.pallas_call(
        functools.partial(_inproj_kernel, mla_q_scale=mla_q_scale),
        grid=(nt,),
        in_specs=[row(D_MODEL), full(w1), pos(), pos(), pos(), pos(), full(qnw), full(kvnw),
                  full(wuq), full(wuk), full(wuvt)],
        out_specs=[row(w) for w in widths]
        + [pl.BlockSpec((tiles_per_step, w, tile), lambda i: (i, 0, 0)) for w in widths_t],
        out_shape=[jax.ShapeDtypeStruct((T, w), BF16) for w in widths]
        + [jax.ShapeDtypeStruct((T // tile, w, tile), BF16) for w in widths_t],
        compiler_params=_params("parallel"),
        name="inproj",
    )(x, w1, cr, sr, cm, sm, qnw, kvnw, wuq, wuk, wuvt)


def _retention_kernel(q_ref, k_ref, v_ref, g_ref, gnw_ref, dec_ref, xi_ref, zeta_ref, dm_ref, bm_ref,
                      avg_ref, o_ref, state_ref, *, chunk, n_chunks):
    @pl.when(pl.program_id(1) == 0)
    def _():
        state_ref[...] = jnp.zeros_like(state_ref)

    lane_head = lax.broadcasted_iota(jnp.int32, (chunk, RET_W), 1) // RET_DV
    avg = avg_ref[...]
    chunks = [slice(c * chunk, (c + 1) * chunk) for c in range(n_chunks)]

    def raw_scores(rows):
        q = q_ref[rows, :]
        zero = jnp.zeros_like(q)
        q_heads = jnp.concatenate([jnp.where(lane_head == h, q, zero) for h in range(RET_HEADS)], 0)
        return _dot_nt(q_heads, k_ref[rows, :])

    scores = [raw_scores(rows) for rows in chunks]
    updates = [_dot_tn((k_ref[rows, :].astype(F32) * zeta_ref[...]).astype(BF16), v_ref[rows, :])
               for rows in chunks]
    inner = [_dot((s * dec_ref[...]).astype(BF16), v_ref[rows, :]) for s, rows in zip(scores, chunks)]

    state = state_ref[...]
    outs = []
    for rows, upd, inner_heads in zip(chunks, updates, inner):
        o = _dot((q_ref[rows, :].astype(F32) * xi_ref[...]).astype(BF16), state.astype(BF16))
        for h in range(RET_HEADS):
            o = o + jnp.where(lane_head == h, inner_heads[h * chunk:(h + 1) * chunk], 0.0)
        outs.append(o)
        state = dm_ref[...] * state + bm_ref[...] * upd
    state_ref[...] = state

    mus = [_split_dot(o, avg) for o in outs]
    devs = [o - mu for o, mu in zip(outs, mus)]
    variances = [_split_dot(d * d, avg) for d in devs]
    for rows, d, var in zip(chunks, devs, variances):
        on = d * lax.rsqrt(var + LN_EPS) * gnw_ref[...]
        gate = g_ref[rows, :].astype(F32)
        o_ref[rows, :] = (on * (gate * jax.nn.sigmoid(gate))).astype(BF16)


def _retention_tables(chunk):
    H = RET_HEADS
    gamma = 1.0 - 2.0 ** (-5.0 - np.arange(H, dtype=np.float64))
    log_g = np.log(gamma)
    idx = np.arange(chunk, dtype=np.float64)
    rel = idx[:, None] - idx[None, :]
    dec = np.where(rel[None] >= 0, np.exp(np.maximum(rel, 0.0)[None] * log_g[:, None, None]), 0.0)
    dec = dec.reshape(H * chunk, chunk)
    lane_head = np.arange(RET_W) // RET_DV
    xi = np.exp((idx[:, None] + 1.0) * log_g[lane_head][None, :])
    zeta = np.exp((chunk - 1.0 - idx[:, None]) * log_g[lane_head][None, :])
    same = lane_head[:, None] == lane_head[None, :]
    dm = np.where(same, np.exp(chunk * log_g)[lane_head][:, None], 0.0)
    bm = same.astype(np.float64)
    avg = bm / RET_DV
    f = lambda a: jnp.asarray(a, F32)
    return f(dec), f(xi), f(zeta), f(dm), f(bm), jnp.asarray(avg, BF16)


def _retention(rq, rk, rv, rg, gnw, *, layer, batch, seq, blk=2048, chunk=128):
    T = rq.shape[0]
    nb = seq // blk
    tables = _retention_tables(chunk)
    row = pl.BlockSpec((blk, RET_W), lambda b, i: (b * nb + i, 0))
    full = lambda a: pl.BlockSpec(a.shape, lambda b, i: (0, 0))
    return pl.pallas_call(
        functools.partial(_retention_kernel, chunk=chunk, n_chunks=blk // chunk),
        grid=(batch, nb),
        in_specs=[row, row, row, row, _layer_spec(gnw, layer)] + [full(t) for t in tables],
        out_specs=row,
        out_shape=jax.ShapeDtypeStruct((T, RET_W), BF16),
        scratch_shapes=[pltpu.VMEM((RET_W, RET_W), F32)],
        compiler_params=_params("parallel", "arbitrary"),
        name="retention",
    )(rq, rk, rv, rg, gnw, *tables)


def _swa_kernel(q_ref, kc_ref, kp_ref, vtc_ref, vtp_ref, sink_ref, o_ref, *, blk, seq):
    L = SWA_WINDOW
    n_pairs = SWA_Q_HEADS // 2
    n_sub = blk // L
    lane_lo = lax.broadcasted_iota(jnp.int32, (L, LANES), 1) < HEAD_DIM
    feat_lo = lax.broadcasted_iota(jnp.int32, (LANES, L), 0) < HEAD_DIM
    key = lax.broadcasted_iota(jnp.int32, (2 * L, SWA_Q_HEADS * L), 0)
    qry = lax.broadcasted_iota(jnp.int32, (2 * L, SWA_Q_HEADS * L), 1) % L
    band = (key > qry) & (key <= qry + L)
    sink = sink_ref[...] * math.log2(math.e)

    def scores(sb):
        r0 = sb * L
        k_prev = kp_ref[...] if sb == 0 else kc_ref[r0 - L:r0, :]
        k_win = jnp.concatenate([k_prev, kc_ref[r0:r0 + L, :]], 0)
        pairs = [q_ref[r0:r0 + L, g * LANES:(g + 1) * LANES] for g in range(n_pairs)]
        zero = jnp.zeros_like(pairs[0])
        q_heads = jnp.concatenate([jnp.where(lane_lo, p, zero) for p in pairs]
                                  + [jnp.where(lane_lo, zero, p) for p in pairs], 0)
        return _dot_nt(k_win, q_heads)

    def attend(sb, st):
        r0 = sb * L
        valid = band
        if sb == 0:
            pos0 = (pl.program_id(0) * blk) % seq
            valid = band & (key >= jnp.where(pos0 > 0, 0, L))
        st = jnp.where(valid, st, NEG_INF)
        m = jnp.maximum(jnp.max(st, 0, keepdims=True), sink)
        p = jnp.exp2(st - m)
        denom = jnp.sum(p, 0, keepdims=True) + jnp.exp2(sink - m)
        c, off = divmod(r0, vtc_ref.shape[2])
        if off == 0:
            vt_prev = vtp_ref[0] if c == 0 else vtc_ref[c - 1, :, vtc_ref.shape[2] - L:]
            vt_win = jnp.concatenate([vt_prev, vtc_ref[c, :, :L]], 1)
        else:
            vt_win = vtc_ref[c, :, off - L:off + L]
        ot = _dot(vt_win, p.astype(BF16)) * (1.0 / denom)
        for g in range(n_pairs):
            o_lo = ot[:, g * L:(g + 1) * L]
            o_hi = ot[:, (g + n_pairs) * L:(g + n_pairs + 1) * L]
            o_ref[r0:r0 + L, g * LANES:(g + 1) * LANES] = jnp.where(feat_lo, o_lo, o_hi).T.astype(BF16)

    st = scores(0)
    for sb in range(n_sub):
        st_next = scores(sb + 1) if sb + 1 < n_sub else None
        attend(sb, st)
        st = st_next


def _swa(sq, sk, svt, sink_row, *, layer, seq, blk):
    T = sq.shape[0]
    L = SWA_WINDOW
    per = blk // L
    tile = svt.shape[2]
    tiles = blk // tile
    cur = lambda w: pl.BlockSpec((blk, w), lambda i: (i, 0))
    k_prev = pl.BlockSpec((L, SWA_KV_W), lambda i: (jnp.maximum(i * per - 1, 0), 0))
    vt_cur = pl.BlockSpec((tiles, SWA_KV_W, tile), lambda i: (i, 0, 0))
    vt_prev = pl.BlockSpec((1, SWA_KV_W, L), lambda i: (jnp.maximum(i * tiles - 1, 0), 0, tile // L - 1))
    return pl.pallas_call(
        functools.partial(_swa_kernel, blk=blk, seq=seq),
        grid=(T // blk,),
        in_specs=[cur(SWA_W), cur(SWA_KV_W), k_prev, vt_cur, vt_prev, _layer_spec(sink_row, layer)],
        out_specs=cur(SWA_W),
        out_shape=jax.ShapeDtypeStruct((T, SWA_W), BF16),
        compiler_params=_params("parallel"),
        name="swa",
    )(sq, sk, sk, svt, svt, sink_row)


def _mla_kernel(q_ref, k_ref, vt_ref, o_ref, st0_ref, st1_ref, cmax0_ref, cmax1_ref, m_ref, acc_ref,
                *, tq, tk, heads):
    i = pl.program_id(2)
    head_lanes = [slice(h * LANES, (h + 1) * LANES) for h in range(heads)]
    st_refs = (st0_ref, st1_ref)
    cmax_refs = (cmax0_ref, cmax1_ref)
    all_q = slice(0, tq)
    late_q = slice(tk, tq)

    def scores(j, lanes, qs):
        start = pl.multiple_of(j * tk, tk)
        return _dot_nt(k_ref[pl.ds(start, tk), lanes], q_ref[qs, lanes])

    def step(j, cur, qs=all_q, diagonal=False, next_qs=all_q):
        for h, lanes in enumerate(head_lanes):
            if next_qs is not None:
                st_next = scores(j + 1, lanes, next_qs)
                st_refs[1 - cur][h, :, next_qs] = st_next
                if next_qs is all_q:
                    cmax_refs[1 - cur][h] = jnp.max(st_next, 0, keepdims=True)
            if qs is all_q and not diagonal:
                m_all, cmax_all, m_halves = m_ref[h], cmax_refs[cur][h], []
                for cols in (slice(0, tk), slice(tk, tq)):
                    m = m_all[:, cols]
                    m_new = jnp.maximum(m, cmax_all[:, cols])
                    pt = jnp.exp2(st_refs[cur][h, :, cols] - m_new).astype(BF16)
                    acc_ref[h, :, cols] = (jnp.exp2(m - m_new) * acc_ref[h, :, cols]
                                           + _dot(vt_ref[j, lanes, :], pt))
                    m_halves.append(m_new)
                m_ref[h] = jnp.concatenate(m_halves, 1)
                continue
            st = st_refs[cur][h, :, qs]
            if diagonal:
                key = lax.broadcasted_iota(jnp.int32, st.shape, 0)
                qry = lax.broadcasted_iota(jnp.int32, st.shape, 1)
                st = jnp.where(key <= qry, st, NEG_INF)
                col_max = jnp.max(st, 0, keepdims=True)
            else:
                col_max = cmax_refs[cur][h]
            m = m_ref[h][:, qs]
            m_new = jnp.maximum(m, col_max)
            pt = jnp.exp2(st - m_new).astype(BF16)
            acc_ref[h, :, qs] = jnp.exp2(m - m_new) * acc_ref[h, :, qs] + _dot(vt_ref[j, lanes, :], pt)
            if qs is all_q:
                m_ref[h] = m_new

    m_ref[...] = jnp.full(m_ref.shape, NEG_INF, F32)
    acc_ref[...] = jnp.zeros(acc_ref.shape, F32)
    for h, lanes in enumerate(head_lanes):
        st_first = scores(0, lanes, all_q)
        st0_ref[h] = st_first
        cmax0_ref[h] = jnp.max(st_first, 0, keepdims=True)

    def pair(p, _):
        step(2 * p, 0)
        step(2 * p + 1, 1)
        return 0

    lax.fori_loop(0, i, pair, 0)
    step(2 * i, 0, diagonal=True, next_qs=late_q)
    step(2 * i + 1, 1, qs=late_q, diagonal=True, next_qs=None)
    for h in range(heads):
        acc = acc_ref[h]
        o_ref[:, h * LANES:(h + 1) * LANES] = (acc / acc[MLA_V:MLA_V + 1, :]).T.astype(BF16)


def _mla(mq, mk, mvt, *, batch, seq, tk, heads=3):
    T = mq.shape[0]
    tq = 2 * tk
    nq = seq // tq
    qo = pl.BlockSpec((tq, heads * LANES), lambda b, h, i: (b * nq + i, h))
    k = pl.BlockSpec((seq, heads * LANES), lambda b, h, i: (b, h))
    vt = pl.BlockSpec((seq // tk, heads * LANES, tk), lambda b, h, i: (b, h, 0))
    return pl.pallas_call(
        functools.partial(_mla_kernel, tq=tq, tk=tk, heads=heads),
        grid=(batch, MLA_HEADS // heads, nq),
        in_specs=[qo, k, vt],
        out_specs=qo,
        out_shape=jax.ShapeDtypeStruct((T, MLA_PAD_W), BF16),
        scratch_shapes=[pltpu.VMEM((heads, tk, tq), F32), pltpu.VMEM((heads, tk, tq), F32),
                        pltpu.VMEM((heads, 1, tq), F32), pltpu.VMEM((heads, 1, tq), F32),
                        pltpu.VMEM((heads, 1, tq), F32), pltpu.VMEM((heads, LANES, tq), F32)],
        compiler_params=_params("parallel", "parallel", "arbitrary"),
        name="mla",
    )(mq, mk, mvt)


def _outproj_kernel(ret_ref, swa_ref, mla_ref, x_ref, wr_ref, ws_ref, wm_ref, g_ref, b_ref,
                    rw_ref, x1_ref, x1b_ref, logits_ref, *, alpha, n_sub):
    tm = x_ref.shape[0]
    subs = [slice(s * tm // n_sub, (s + 1) * tm // n_sub) for s in range(n_sub)]

    def mix(rows):
        return (_dot(ret_ref[rows, :], wr_ref[...]) + _dot(swa_ref[rows, :], ws_ref[...])
                + _dot(mla_ref[rows, :], wm_ref[...]))

    def finish(rows, mixed):
        x1 = _layer_norm(alpha * x_ref[rows, :] + mixed, g_ref[...], b_ref[...])
        x1_ref[rows, :] = x1
        hi = x1.astype(BF16)
        x1b_ref[rows, :] = hi
        lo = (x1 - hi.astype(F32)).astype(BF16)
        hi_part = _dot(hi, rw_ref[...])
        logits_ref[rows, :] = hi_part[:, :LANES] + hi_part[:, LANES:] + _dot(lo, rw_ref[:, :LANES])

    mixed = mix(subs[0])
    for s in range(n_sub):
        mixed_next = mix(subs[s + 1]) if s + 1 < n_sub else None
        finish(subs[s], mixed)
        mixed = mixed_next


def _outproj(ret_o, swa_o, mla_o, x, wr, ws, wm, g, b, rw, *, layer, alpha, tm=1024, n_sub=8):
    T = x.shape[0]
    row = lambda w: pl.BlockSpec((tm, w), lambda i: (i, 0))
    full = lambda a: pl.BlockSpec(a.shape, lambda i: (0, 0))
    per_layer = lambda a: _layer_spec(a, layer)
    return pl.pallas_call(
        functools.partial(_outproj_kernel, alpha=alpha, n_sub=n_sub),
        grid=(T // tm,),
        in_specs=[row(RET_W), row(SWA_W), row(MLA_PAD_W), row(D_MODEL), per_layer(wr), per_layer(ws),
                  per_layer(wm), per_layer(g), per_layer(b), full(rw)],
        out_specs=[row(D_MODEL), row(D_MODEL), row(LANES)],
        out_shape=[jax.ShapeDtypeStruct((T, D_MODEL), F32), jax.ShapeDtypeStruct((T, D_MODEL), BF16),
                   jax.ShapeDtypeStruct((T, LANES), F32)],
        compiler_params=_params("parallel"),
        name="outproj",
    )(ret_o, swa_o, mla_o, x, wr, ws, wm, g, b, rw)


MOE_CHUNK = 128
MOE_SEG_ALIGN = 16


def _moe_rows(tm):
    return -(-(2 * tm + N_EXPERTS * (MOE_SEG_ALIGN - 1) + MOE_CHUNK) // LANES) * LANES


def _ring_partner(x, d, width):
    n = x.shape[0]
    ahead = pltpu.roll(x, n - d, 0)
    if width == n:
        return ahead
    row = lax.broadcasted_iota(jnp.int32, x.shape, 0)
    return jnp.where((row % width) + d < width, ahead, pltpu.roll(x, width - d, 0))


def _router_gates_t(logits_t, bias_col):
    row = lax.broadcasted_iota(jnp.int32, logits_t.shape, 0)
    scores = jax.nn.sigmoid(logits_t)
    biased = scores + bias_col
    G = EXPERTS_PER_GROUP
    member = row % G
    rank = jnp.zeros(logits_t.shape, jnp.int32)
    for d in range(1, G):
        other = _ring_partner(biased, d, G)
        other_first = (member + d) % G < member
        rank = rank + ((other > biased) | ((other == biased) & other_first)).astype(jnp.int32)
    top2 = rank < 2
    part = jnp.where(top2, biased, 0.0)
    grp_score = part
    for d in range(1, G):
        grp_score = grp_score + _ring_partner(part, d, G)
    beaten = jnp.zeros(logits_t.shape, jnp.bool_)
    grp = row // G
    for dg in range(1, N_GROUPS):
        other = _ring_partner(grp_score, dg * G, N_EXPERTS)
        other_first = (grp + dg) % N_GROUPS < grp
        beaten = beaten | (other > grp_score) | ((other == grp_score) & other_first)
    picked = jnp.where(top2 & jnp.logical_not(beaten), scores, 0.0)
    return picked / jnp.sum(picked, 0, keepdims=True)


def _moe_kernel(xb_ref, x_ref, logits_ref, rb_ref, earlier_ref, wgu_hbm, wd_hbm, g_ref, b_ref, o_ref,
                xs_ref, ys_ref, wgu_ref, wd_ref, gu_stage, d_stage, sem, start_ref, chunks_ref,
                *, alpha, layer):
    tm = xb_ref.shape[0]
    R = xs_ref.shape[0]
    big = float(4 * R)

    @pl.when(pl.program_id(0) == 0)
    def _():
        def expert_copies(e, slot):
            return (pltpu.make_async_copy(wgu_hbm.at[layer, e], gu_stage.at[slot], sem.at[0, slot]),
                    pltpu.make_async_copy(wd_hbm.at[layer, e], d_stage.at[slot], sem.at[1, slot]))

        for copy in expert_copies(0, 0):
            copy.start()
        for e in range(N_EXPERTS):
            slot = e % 2
            if e + 1 < N_EXPERTS:
                for copy in expert_copies(e + 1, 1 - slot):
                    copy.start()
            for copy in expert_copies(e, slot):
                copy.wait()
            wgu_ref[e] = gu_stage[slot].astype(BF16)
            wd_ref[e] = d_stage[slot].astype(BF16)

    experts_t = _router_gates_t(logits_ref[...].T[:N_EXPERTS], rb_ref[...])
    gates_t = jnp.concatenate([experts_t, jnp.zeros((LANES - N_EXPERTS, tm), F32)], 0)
    gates = gates_t.T

    def strict_upper(n):
        return lax.broadcasted_iota(jnp.int32, (n, n), 0) < lax.broadcasted_iota(jnp.int32, (n, n), 1)

    def seg_pad(count):
        return jnp.floor((count + (MOE_SEG_ALIGN - 1.0)) * (1.0 / MOE_SEG_ALIGN)) * MOE_SEG_ALIGN

    one = lambda mask: jnp.where(mask, 1.0, 0.0).astype(BF16)

    sel = (gates > 0.0) & (lax.broadcasted_iota(jnp.int32, gates.shape, 1) < N_EXPERTS)
    earlier = earlier_ref[...]
    rank = _dot(earlier, one(sel))
    count = jnp.sum(jnp.where(sel, 1.0, 0.0), 0, keepdims=True)
    pad = jnp.broadcast_to(seg_pad(count), (SUBLANES, LANES)).astype(BF16)
    offset = _dot(pad, one(strict_upper(LANES)))[0:1]
    slot = offset + rank
    slot_a = jnp.min(jnp.where(sel, slot, big), -1, keepdims=True)
    slot_b = jnp.max(jnp.where(sel, slot, -1.0), -1, keepdims=True)
    gate_a = jnp.sum(jnp.where(sel & (slot == slot_a), gates, 0.0), -1, keepdims=True)
    gate_b = jnp.sum(jnp.where(sel & (slot == slot_b), gates, 0.0), -1, keepdims=True)

    sel_t = experts_t > 0.0
    rank_t = _dot_nt(one(sel_t), earlier)
    on_diag = (lax.broadcasted_iota(jnp.int32, (N_EXPERTS, LANES), 0)
               == lax.broadcasted_iota(jnp.int32, (N_EXPERTS, LANES), 1))
    offset_col = jnp.sum(jnp.where(on_diag, offset, 0.0), -1, keepdims=True)
    slot_t = offset_col + rank_t
    slot_a_t = jnp.min(jnp.where(sel_t, slot_t, big), 0, keepdims=True)
    slot_b_t = jnp.max(jnp.where(sel_t, slot_t, -1.0), 0, keepdims=True)

    row_id = lax.broadcasted_iota(jnp.int32, (R, tm), 0)
    sort = one((row_id == slot_a_t.astype(jnp.int32)) | (row_id == slot_b_t.astype(jnp.int32)))
    xs_ref[...] = _dot(sort, xb_ref[...]).astype(BF16)

    @pl.when(pl.program_id(0) == 0)
    def _():
        ys_ref[...] = jnp.zeros_like(ys_ref)

    offset_i = offset.astype(jnp.int32)
    chunks_i = jnp.floor((count + (MOE_CHUNK - 1.0)) * (1.0 / MOE_CHUNK)).astype(jnp.int32)
    for e in range(N_EXPERTS):
        start_ref[e] = offset_i[0, e]
        chunks_ref[e] = chunks_i[0, e]

    def chunk_rows(e, c):
        return pl.ds(pl.multiple_of(start_ref[e] + c * MOE_CHUNK, MOE_SEG_ALIGN), MOE_CHUNK)

    def activation(e, rows):
        h = _dot(xs_ref[rows, :], wgu_ref[e])
        up, lin = h[:, :D_EXPERT], h[:, D_EXPERT:]
        return (up * jax.nn.sigmoid(up) * lin).astype(BF16)

    def project(e, rows, a):
        ys_ref[rows, :] = _dot(a, wd_ref[e]).astype(BF16)

    def later_chunks(e, _):
        def body(c, _):
            rows = chunk_rows(e, c)
            project(e, rows, activation(e, rows))
            return 0
        return lax.fori_loop(1, chunks_ref[e], body, 0)

    lax.fori_loop(0, N_EXPERTS, later_chunks, 0)

    a = activation(0, chunk_rows(0, 0))
    for e in range(N_EXPERTS):
        a_next = activation(e + 1, chunk_rows(e + 1, 0)) if e + 1 < N_EXPERTS else None
        project(e, chunk_rows(e, 0), a)
        a = a_next

    col_id = lax.broadcasted_iota(jnp.int32, (tm, R), 1)
    unsort = jnp.where(col_id == slot_a.astype(jnp.int32), gate_a,
                       jnp.where(col_id == slot_b.astype(jnp.int32), gate_b, 0.0)).astype(BF16)
    o_ref[...] = _layer_norm(alpha * x_ref[...] + _dot(unsort, ys_ref[...]), g_ref[...], b_ref[...])


def _moe(x1b, x1, logits, rb, wgu, wd, g, b, *, layer, alpha, tm=512):
    T = x1.shape[0]
    R = _moe_rows(tm)
    earlier = jnp.asarray(np.tril(np.ones((tm, tm), np.float32), -1), BF16)
    row = lambda w: pl.BlockSpec((tm, w), lambda i: (i, 0))
    full = lambda a: _layer_spec(a, layer)
    assert wgu.dtype == F32 and wd.dtype == F32, "the staging buffers are f32"
    in_hbm = pl.BlockSpec(memory_space=pl.ANY)
    return pl.pallas_call(
        functools.partial(_moe_kernel, alpha=alpha, layer=layer),
        grid=(T // tm,),
        in_specs=[row(D_MODEL), row(D_MODEL), row(LANES), pl.BlockSpec(rb.shape, lambda i: (0, 0)),
                  pl.BlockSpec(earlier.shape, lambda i: (0, 0)), in_hbm, in_hbm, full(g), full(b)],
        out_specs=row(D_MODEL),
        out_shape=jax.ShapeDtypeStruct((T, D_MODEL), F32),
        scratch_shapes=[pltpu.VMEM((R, D_MODEL), BF16), pltpu.VMEM((R, D_MODEL), BF16),
                        pltpu.VMEM(wgu.shape[1:], BF16), pltpu.VMEM(wd.shape[1:], BF16),
                        pltpu.VMEM((2,) + wgu.shape[2:], F32), pltpu.VMEM((2,) + wd.shape[2:], F32),
                        pltpu.SemaphoreType.DMA((2, 2)),
                        pltpu.SMEM((N_EXPERTS,), jnp.int32), pltpu.SMEM((N_EXPERTS,), jnp.int32)],
        compiler_params=_params("arbitrary"),
        name="moe",
    )(x1b, x1, logits, rb, earlier, wgu, wd, g, b)


def _rope_tables(seq):
    pos = np.arange(seq, dtype=np.float64)[:, None]
    lane = np.arange(LANES)

    def table(dim, active, offset):
        j = (lane - offset) % dim
        inv = ROPE_THETA ** (-(2.0 * (j % (dim // 2))) / dim)
        ang = pos * inv[None, :]
        sign = np.where(j < dim // 2, -1.0, 1.0)
        cos = np.where(active[None, :], np.cos(ang), 1.0)
        sin = np.where(active[None, :], np.sin(ang) * sign[None, :], 0.0)
        return jnp.asarray(cos, F32), jnp.asarray(sin, F32)

    cr, sr = table(RET_DK, np.ones(LANES, bool), 0)
    cm, sm = table(MLA_ROPE, (lane >= MLA_NOPE) & (lane < MLA_NOPE + MLA_ROPE), MLA_NOPE)
    return cr, sr, cm, sm


def _stacked_weights(w_in, mla_w_uq, mla_w_ukv, w_out):
    depth = w_in.shape[0]
    sizes = (256, 256, 256, 256, SWA_W, SWA_KV_W, SWA_KV_W, MLA_Q_RANK, MLA_KV_RANK, MLA_ROPE)
    o = [int(v) for v in np.concatenate([[0], np.cumsum(sizes)])]
    order = np.array(_SWA_HEAD_ORDER)
    pad_lanes = lambda a: jnp.pad(a, [(0, 0)] * (a.ndim - 1) + [(0, LANES - a.shape[-1])])

    sq = w_in[:, :, o[4]:o[5]].reshape(depth, D_MODEL, SWA_Q_HEADS, HEAD_DIM)[:, :, order]
    kr = jnp.pad(w_in[:, :, o[9]:o[10]], ((0, 0), (0, 0), (MLA_NOPE, LANES - MLA_NOPE - MLA_ROPE)))
    w1 = jnp.concatenate([w_in[:, :, :o[4]], sq.reshape(depth, D_MODEL, SWA_W), kr, w_in[:, :, o[5]:o[7]],
                          w_in[:, :, o[8]:o[9]], w_in[:, :, o[7]:o[8]]], 2).astype(BF16)

    dq = MLA_NOPE + MLA_ROPE
    wuq = pad_lanes(mla_w_uq.reshape(depth, MLA_Q_RANK, MLA_HEADS, dq)).reshape(depth, MLA_Q_RANK, MLA_PAD_W)
    ukv = mla_w_ukv.reshape(depth, MLA_KV_RANK, MLA_HEADS, MLA_NOPE + MLA_V)
    wuk = pad_lanes(ukv[..., :MLA_NOPE]).reshape(depth, MLA_KV_RANK, MLA_PAD_W)
    wuvt = jnp.swapaxes(pad_lanes(ukv[..., MLA_NOPE:]).reshape(depth, MLA_KV_RANK, MLA_PAD_W), 1, 2)

    wr = w_out[:, :RET_W]
    ws = w_out[:, RET_W:RET_W + SWA_W].reshape(depth, SWA_Q_HEADS, HEAD_DIM, D_MODEL)[:, order]
    wm = w_out[:, RET_W + SWA_W:].reshape(depth, MLA_HEADS, MLA_V, D_MODEL)
    wm = jnp.pad(wm, ((0, 0), (0, 0), (0, LANES - MLA_V), (0, 0)))
    to_bf16 = lambda a: a.astype(BF16)
    return (w1, to_bf16(wuq), to_bf16(wuk), to_bf16(wuvt), to_bf16(wr),
            to_bf16(ws.reshape(depth, SWA_W, D_MODEL)), to_bf16(wm.reshape(depth, MLA_PAD_W, D_MODEL)))


def kernel(x, w_in, ret_gn_w, swa_sinks, mla_q_norm_w, mla_kv_norm_w, mla_w_uq, mla_w_ukv, w_out,
           ln1_g, ln1_b, router_w, router_bias, exp_w_gate_up, exp_w_down, ln2_g, ln2_b):
    batch, seq, d_model = x.shape
    depth = w_in.shape[0]
    assert d_model == D_MODEL and seq % (4 * MLA_TILE) == 0, (x.shape, MLA_TILE)
    assert w_in.shape == (depth, D_MODEL, _C_END - LANES + MLA_ROPE), w_in.shape
    alpha = (2 * depth) ** 0.25
    cr, sr, cm, sm = _rope_tables(seq)

    rw = jnp.pad(router_w, ((0, 0), (0, LANES - N_EXPERTS)))
    rwh = rw.astype(BF16)
    rw = jnp.concatenate([rwh, (rw - rwh.astype(F32)).astype(BF16)], 1)
    rb = router_bias.astype(F32).reshape(N_EXPERTS, 1)

    w1, wuq, wuk, wuvt, wr, ws, wm = _stacked_weights(w_in, mla_w_uq, mla_w_ukv, w_out)
    rows = lambda a: a.astype(F32).reshape(depth, 1, -1)
    qnw, kvnw, gnw = rows(mla_q_norm_w), rows(mla_kv_norm_w), rows(ret_gn_w)
    g1, b1, g2, b2 = rows(ln1_g), rows(ln1_b), rows(ln2_g), rows(ln2_b)
    sink_row = rows(jnp.repeat(swa_sinks, SWA_WINDOW, axis=1))

    t = x.reshape(batch * seq, D_MODEL)
    for l in range(depth):
        rq, rk, rv, rg, sq, sk, mq, mk, svt, mvt = _inproj(
            t, w1, cr, sr, cm, sm, qnw, kvnw, wuq, wuk, wuvt, layer=l, seq=seq, tile=MLA_TILE)
        ret_o = _retention(rq, rk, rv, rg, gnw, layer=l, batch=batch, seq=seq)
        swa_o = _swa(sq, sk, svt, sink_row, layer=l, seq=seq, blk=4 * MLA_TILE)
        mla_o = _mla(mq, mk, mvt, batch=batch, seq=seq, tk=MLA_TILE)
        x1, x1b, logits = _outproj(ret_o, swa_o, mla_o, t, wr, ws, wm, g1, b1, rw, layer=l, alpha=alpha)
        t = _moe(x1b, x1, logits, rb, exp_w_gate_up, exp_w_down, g2, b2, layer=l, alpha=alpha)
    return t.reshape(batch, seq, D_MODEL)
```
